```python
import math
import jax, jax.numpy as jnp
from jax import lax
import numpy as np

D_MODEL = 2048
BATCH = 4
SEQ = 2048
DEPTH = 4

GRID_W = 64
CTX_LEN = 256

D_MIX = D_MODEL
MLA_V = 128
MLA_NOPE = 128
MLA_ROPE = 64
MLA_W = D_MIX // 2
MLA_HEADS = MLA_W // MLA_V
MLA_Q_RANK = 512
MLA_KV_RANK = 256
MLA_SCALE = (MLA_NOPE + MLA_ROPE) ** -0.5
MLA_QB = 128
LRU_W = D_MIX // 4
LRU_BLOCKS = 8
LRU_BW = LRU_W // LRU_BLOCKS
LRU_CONV = 4
LRU_C = 8.0
RET_HEADS = 4
RET_DH = 128
RET_W = RET_HEADS * RET_DH
RET_CHUNK = 128
RET_K_SCALE = RET_DH ** -0.5

ROPE_BASE = 10000.0
LN_EPS = 1e-5
RMS_EPS = 1e-6
ALPHA = (2 * DEPTH) ** 0.25
BETA = (8 * DEPTH) ** -0.25

SPLITS = (MLA_Q_RANK, MLA_KV_RANK, MLA_ROPE, MLA_W,
          LRU_W, LRU_W,
          RET_W, RET_W, RET_W, RET_W)
MIX_IN = sum(SPLITS)

kernel_name = 'hybrid_mla_rglru_retention_dit'


def layer_norm(x, g=None, b=None):
    xf = x.astype(jnp.float32)
    mu = jnp.mean(xf, axis=-1, keepdims=True)
    var = jnp.mean(jnp.square(xf - mu), axis=-1, keepdims=True)
    y = (xf - mu) * lax.rsqrt(var + LN_EPS)
    if g is not None:
        y = y * g.astype(jnp.float32) + b.astype(jnp.float32)
    return y.astype(x.dtype)


def rms_norm(x, g):
    xf = x.astype(jnp.float32)
    y = xf * lax.rsqrt(jnp.mean(jnp.square(xf), axis=-1, keepdims=True) + RMS_EPS)
    return (y * g.astype(jnp.float32)).astype(x.dtype)


def split_cols(u):
    offsets = np.cumsum(SPLITS)[:-1].tolist()
    return jnp.split(u, offsets, axis=-1)


def ada_mod(cond, w, b):
    m = jax.nn.silu(cond) @ w + b
    return jnp.split(m, 3, axis=-1)


def axial_rope_tables(rows, dim, dtype):
    row = jnp.repeat(jnp.arange(rows, dtype=jnp.float32), GRID_W)
    col = jnp.tile(jnp.arange(GRID_W, dtype=jnp.float32), rows)
    quarter = dim // 4
    inv = ROPE_BASE ** (-jnp.arange(quarter, dtype=jnp.float32) / quarter)
    ang = jnp.stack([row[:, None] * inv, col[:, None] * inv], axis=1)
    return jnp.cos(ang).astype(dtype), jnp.sin(ang).astype(dtype)


def apply_rope(x, cos, sin):
    shp = x.shape
    xr = x.reshape(shp[:-1] + (2, 2, shp[-1] // 4))
    if x.ndim == 4:
        cos, sin = cos[:, None], sin[:, None]
    x1, x2 = xr[..., 0, :], xr[..., 1, :]
    out = jnp.stack([x1 * cos - x2 * sin, x2 * cos + x1 * sin], axis=-2)
    return out.reshape(shp)


def mla_queries(q_lat, g, w_uq):
    B, T, _ = q_lat.shape
    q = (rms_norm(q_lat, g) @ w_uq).reshape(B, T, MLA_HEADS, MLA_NOPE + MLA_ROPE)
    return q[..., :MLA_NOPE], q[..., MLA_NOPE:]


def mla_kv(kv_lat, g, w_ukv):
    B, T, _ = kv_lat.shape
    kv = (rms_norm(kv_lat, g) @ w_ukv).reshape(B, T, MLA_HEADS, MLA_NOPE + MLA_V)
    return kv[..., :MLA_NOPE], kv[..., MLA_NOPE:]


def mla_scores(qn, qr, kn, kr):
    return jnp.einsum('bqhd,bkhd->bhqk', qn, kn) + jnp.einsum('bqhd,bkd->bhqk', qr, kr)


def softmax_attend(s, v):
    p = jax.nn.softmax(s.astype(jnp.float32) * MLA_SCALE, axis=-1).astype(v.dtype)
    return jnp.einsum('bhqk,bkhd->bqhd', p, v)


def mla_latent(qn, qr, kn_all, kr_all, v_all):
    B, T = qn.shape[:2]
    nb = T // MLA_QB

    def to_blocks(a):
        return a.reshape((B, nb, MLA_QB) + a.shape[2:]).swapaxes(0, 1)

    def block(args):
        bqn, bqr = args
        return softmax_attend(mla_scores(bqn, bqr, kn_all, kr_all), v_all)

    o = lax.map(block, (to_blocks(qn), to_blocks(qr)))
    return o.swapaxes(0, 1).reshape(B, T, MLA_W)


def mla_mixer(q_lat, kv_lat, k_rope, gate, q_lat_c, kv_lat_c, k_rope_c, gate_c,
              q_norm_g, kv_norm_g, w_uq, w_ukv, cos, sin, need_ctx_out):
    kn_c, v_c = mla_kv(kv_lat_c, kv_norm_g, w_ukv)
    kn, v = mla_kv(kv_lat, kv_norm_g, w_ukv)
    kr = apply_rope(k_rope, cos, sin)
    qn, qr = mla_queries(q_lat, q_norm_g, w_uq)
    qr = apply_rope(qr, cos, sin)
    kn_all = jnp.concatenate([kn_c, kn], axis=1)
    kr_all = jnp.concatenate([k_rope_c, kr], axis=1)
    v_all = jnp.concatenate([v_c, v], axis=1)
    y = mla_latent(qn, qr, kn_all, kr_all, v_all) * jax.nn.silu(gate)
    yc = None
    if need_ctx_out:
        B, L = q_lat_c.shape[:2]
        qn_c, qr_c = mla_queries(q_lat_c, q_norm_g, w_uq)
        o_c = softmax_attend(mla_scores(qn_c, qr_c, kn_c, k_rope_c), v_c)
        yc = o_c.reshape(B, L, MLA_W) * jax.nn.silu(gate_c)
    return y, yc


def short_conv(x, w, b):
    T = x.shape[1]
    left = LRU_CONV // 2
    xp = jnp.pad(x, ((0, 0), (left, LRU_CONV - 1 - left), (0, 0)))
    return sum(xp[:, k:k + T] * w[k] for k in range(LRU_CONV)) + b


def block_diag(x, w, b):
    B, T, _ = x.shape
    y = jnp.einsum('btgi,gij->btgj', x.reshape(B, T, LRU_BLOCKS, LRU_BW), w)
    return y.reshape(B, T, LRU_W) + b


def rglru_coeffs(x, w_r, b_r, w_i, b_i, lam):
    r = jax.nn.sigmoid(block_diag(x, w_r, b_r))
    i = jax.nn.sigmoid(block_diag(x, w_i, b_i))
    log_a = -LRU_C * r * jax.nn.softplus(-lam)
    a = jnp.exp(log_a)
    return a, jnp.sqrt(-jnp.expm1(2.0 * log_a)) * (i * x)


def _lin_combine(e1, e2):
    a1, b1 = e1
    a2, b2 = e2
    return a1 * a2, a2 * b1 + b2


def linear_scan(a, b, h0, reverse):
    if reverse:
        b = b.at[:, -1].add(a[:, -1] * h0)
    else:
        b = b.at[:, 0].add(a[:, 0] * h0)
    _, h = lax.associative_scan(_lin_combine, (a, b), axis=1, reverse=reverse)
    return h


def rglru_mixer(x_lat, gate, x_ctx, gate_c, conv_w, conv_b, w_r, b_r, w_i, b_i, lam, need_ctx_out):
    xl = short_conv(x_lat, conv_w, conv_b)
    xc = short_conv(x_ctx, conv_w, conv_b)
    outs, outs_c = [], []
    for d, reverse in ((0, False), (1, True)):
        a_c, b_c = rglru_coeffs(xc, w_r[d], b_r[d], w_i[d], b_i[d], lam[d])
        h_c = linear_scan(a_c, b_c, jnp.zeros_like(xc[:, 0]), reverse)
        h_c_last = h_c[:, 0] if reverse else h_c[:, -1]
        a, b = rglru_coeffs(xl, w_r[d], b_r[d], w_i[d], b_i[d], lam[d])
        outs.append(linear_scan(a, b, h_c_last, reverse))
        outs_c.append(h_c)
    y = (outs[0] + outs[1]) * jax.nn.silu(gate)
    yc = (outs_c[0] + outs_c[1]) * jax.nn.silu(gate_c) if need_ctx_out else None
    return y, yc


def retention_dir(q, k, v, log_g, r0, include_diag):
    B, T, H, dk = q.shape
    nc = T // RET_CHUNK
    qc = q.reshape(B, nc, RET_CHUNK, H, dk)
    kc = k.reshape(B, nc, RET_CHUNK, H, dk)
    vc = v.reshape(B, nc, RET_CHUNK, H, v.shape[-1])
    idx = jnp.arange(RET_CHUNK, dtype=jnp.float32)
    diff = idx[:, None] - idx[None, :]
    mask = diff >= 0 if include_diag else diff > 0
    decay = jnp.where(mask[None], jnp.exp(jnp.where(mask, diff, 0.0)[None] * log_g[:, None, None]), 0.0)
    decay = decay.astype(q.dtype)
    s = jnp.einsum('bcqhd,bckhd->bchqk', qc, kc) * decay
    inner = jnp.einsum('bchqk,bckhe->bcqhe', s, vc)
    zeta = jnp.exp((RET_CHUNK - 1 - idx)[:, None] * log_g[None]).astype(q.dtype)
    xi = jnp.exp((idx + 1)[:, None] * log_g[None]).astype(q.dtype)
    g_chunk = jnp.exp(RET_CHUNK * log_g).astype(q.dtype)[:, None, None]
    kv_chunk = jnp.einsum('bckhd,kh,bckhe->bchde', kc, zeta, vc)

    def step(r, kv):
        return g_chunk * r + kv, r

    _, r_prev = lax.scan(step, r0, kv_chunk.swapaxes(0, 1))
    cross = jnp.einsum('bcqhd,qh,cbhde->bcqhe', qc, xi, r_prev)
    return (inner + cross).reshape(B, T, H, v.shape[-1])


def retention_ctx_state(k, v, log_g, reverse):
    L = k.shape[1]
    pos = jnp.arange(L, dtype=jnp.float32)
    expo = pos if reverse else (L - 1 - pos)
    w = jnp.exp(expo[:, None] * log_g[None]).astype(k.dtype)
    return jnp.einsum('blhd,lh,blhe->bhde', k, w, v)


def bidir_retention(q, k, v, log_g, r_f, r_b):
    flip = lambda a: jnp.flip(a, axis=1)
    fwd = retention_dir(q, k, v, log_g[0], r_f, True)
    bwd = flip(retention_dir(flip(q), flip(k), flip(v), log_g[1], r_b, False))
    return fwd + bwd


def head_norm(o):
    B, T = o.shape[:2]
    return layer_norm(o).reshape(B, T, RET_W)


def retention_mixer(q, k, v, gate, q_c, k_c, v_c, gate_c, decay_raw, cos, sin, need_ctx_out):
    log_g = jax.nn.log_sigmoid(decay_raw.astype(jnp.float32))
    heads = lambda a: a.reshape(a.shape[0], a.shape[1], RET_HEADS, RET_DH)
    q = apply_rope(heads(q), cos, sin)
    k = apply_rope(heads(k), cos, sin) * RET_K_SCALE
    v = heads(v)
    kc = heads(k_c) * RET_K_SCALE
    vc = heads(v_c)
    r_f = retention_ctx_state(kc, vc, log_g[0], False)
    r_b = retention_ctx_state(kc, vc, log_g[1], True)
    y = head_norm(bidir_retention(q, k, v, log_g, r_f, r_b)) * jax.nn.silu(gate)
    yc = None
    if need_ctx_out:
        zeros = jnp.zeros_like(r_f)
        yc = head_norm(bidir_retention(heads(q_c), kc, vc, log_g, zeros, zeros)) * jax.nn.silu(gate_c)
    return y, yc


def setup_inputs(seed: int = 0) -> dict:
    key = jax.random.key(seed)
    ks = jax.random.split(key, 22)
    f32 = jnp.float32
    nrm = lambda k, shape, s: jax.random.normal(k, shape, f32) * s
    L = DEPTH
    u = jax.random.uniform(ks[17], (L, 2, LRU_W), f32, 0.9, 0.999)
    a0 = u ** (1.0 / LRU_C)
    lru_lambda = jnp.log(a0) - jnp.log1p(-a0)
    gamma0 = 1.0 - 2.0 ** (-5.0 - jnp.arange(RET_HEADS, dtype=f32))
    ret_decay = jnp.log(gamma0) - jnp.log1p(-gamma0) + nrm(ks[18], (L, 2, RET_HEADS), 0.05)
    return {
        'x': nrm(ks[0], (BATCH, SEQ, D_MODEL), 1.0),
        'c': nrm(ks[1], (BATCH, D_MODEL), 1.0),
        'ctx': nrm(ks[2], (BATCH, CTX_LEN, D_MODEL), 1.0),
        'c_ctx': nrm(ks[3], (D_MODEL,), 1.0),
        'w_ada': nrm(ks[4], (L, D_MODEL, 3 * D_MODEL), 0.5 * D_MODEL ** -0.5),
        'b_ada': nrm(ks[5], (L, 3 * D_MODEL), 0.02),
        'w_in': nrm(ks[6], (L, D_MODEL, MIX_IN), D_MODEL ** -0.5),
        'mla_q_norm_g': 1.0 + nrm(ks[7], (L, MLA_Q_RANK), 0.02),
        'mla_kv_norm_g': 1.0 + nrm(ks[8], (L, MLA_KV_RANK), 0.02),
        'mla_w_uq': nrm(ks[9], (L, MLA_Q_RANK, MLA_HEADS * (MLA_NOPE + MLA_ROPE)), MLA_Q_RANK ** -0.5),
        'mla_w_ukv': nrm(ks[10], (L, MLA_KV_RANK, MLA_HEADS * (MLA_NOPE + MLA_V)), MLA_KV_RANK ** -0.5),
        'lru_conv_w': nrm(ks[11], (L, LRU_CONV, LRU_W), LRU_CONV ** -0.5),
        'lru_conv_b': nrm(ks[12], (L, LRU_W), 0.02),
        'lru_w_r': nrm(ks[13], (L, 2, LRU_BLOCKS, LRU_BW, LRU_BW), LRU_BW ** -0.5),
        'lru_b_r': nrm(ks[14], (L, 2, LRU_W), 0.02),
        'lru_w_i': nrm(ks[15], (L, 2, LRU_BLOCKS, LRU_BW, LRU_BW), LRU_BW ** -0.5),
        'lru_b_i': nrm(ks[16], (L, 2, LRU_W), 0.02),
        'lru_lambda': lru_lambda,
        'ret_decay': ret_decay,
        'w_out': nrm(ks[19], (L, D_MIX, D_MODEL), BETA * D_MIX ** -0.5),
        'ln_g': 1.0 + nrm(ks[20], (L, D_MODEL), 0.02),
        'ln_b': nrm(ks[21], (L, D_MODEL), 0.02),
    }


def reference(x, c, ctx, c_ctx, w_ada, b_ada, w_in, mla_q_norm_g, mla_kv_norm_g, mla_w_uq, mla_w_ukv,
              lru_conv_w, lru_conv_b, lru_w_r, lru_b_r, lru_w_i, lru_b_i, lru_lambda, ret_decay,
              w_out, ln_g, ln_b):
    rows = x.shape[1] // GRID_W
    cos_m, sin_m = axial_rope_tables(rows, MLA_ROPE, x.dtype)
    cos_r, sin_r = axial_rope_tables(rows, RET_DH, x.dtype)
    h = layer_norm(x)
    hc = layer_norm(ctx)
    for l in range(DEPTH):
        need_ctx_out = l < DEPTH - 1
        sh, sc, gt = ada_mod(c, w_ada[l], b_ada[l])
        shc, scc, gtc = ada_mod(c_ctx, w_ada[l], b_ada[l])
        u = split_cols((h * (1.0 + sc[:, None]) + sh[:, None]) @ w_in[l])
        uc = split_cols((hc * (1.0 + scc) + shc) @ w_in[l])
        y_mla, yc_mla = mla_mixer(u[0], u[1], u[2], u[3], uc[0], uc[1], uc[2], uc[3],
                                  mla_q_norm_g[l], mla_kv_norm_g[l], mla_w_uq[l], mla_w_ukv[l],
                                  cos_m, sin_m, need_ctx_out)
        y_lru, yc_lru = rglru_mixer(u[4], u[5], uc[4], uc[5], lru_conv_w[l], lru_conv_b[l],
                                    lru_w_r[l], lru_b_r[l], lru_w_i[l], lru_b_i[l], lru_lambda[l],
                                    need_ctx_out)
        y_ret, yc_ret = retention_mixer(u[6], u[7], u[8], u[9], uc[6], uc[7], uc[8], uc[9],
                                        ret_decay[l], cos_r, sin_r, need_ctx_out)
        y = jnp.concatenate([y_mla, y_lru, y_ret], axis=-1) @ w_out[l]
        h_new = layer_norm(ALPHA * h + gt[:, None] * y, ln_g[l], ln_b[l])
        if need_ctx_out:
            yc = jnp.concatenate([yc_mla, yc_lru, yc_ret], axis=-1) @ w_out[l]
            hc = layer_norm(ALPHA * hc + gtc * yc, ln_g[l], ln_b[l])
        h = h_new
    return h
```

```python
import functools

import jax
import jax.numpy as jnp
from jax import lax
from jax.experimental import pallas as pl
from jax.experimental.pallas import tpu as pltpu

F32 = jnp.float32
BF16 = jnp.bfloat16

GRID_W = 64
MLA_V = 128
MLA_NOPE = 128
MLA_ROPE = 64
MLA_HEADS = 8
MLA_W = MLA_HEADS * MLA_V
MLA_Q_RANK = 512
MLA_KV_RANK = 256
MLA_SCALE = (MLA_NOPE + MLA_ROPE) ** -0.5
MLA_HP = 256
LRU_W = 512
LRU_BLOCKS = 8
LRU_BW = LRU_W // LRU_BLOCKS
LRU_CONV = 4
LRU_C = 8.0
LRU_CB = 256
RET_HEADS = 4
RET_DH = 128
RET_W = RET_HEADS * RET_DH
RET_CHUNK = 128
RET_K_SCALE = RET_DH ** -0.5
ROPE_BASE = 10000.0
LN_EPS = 1e-5
RMS_EPS = 1e-6
LANES = 128

U_QLAT = 0
U_LRU_X = 512
U_LRU_G = 1024
U_RET_Q = 1536
U_RET_K = 2048
U_RET_V = 2560
U_RET_G = 3072
U_MLA_G = 3584
U_KVLAT = 4608
U_KR = 4864
U_W = 4992
U_CHUNKS = 3
U_TN = U_W // U_CHUNKS

VMEM_LIMIT = 56 * 1024 * 1024


def _cparams(sem):
    return pltpu.CompilerParams(dimension_semantics=sem, vmem_limit_bytes=VMEM_LIMIT)


def _silu(x):
    return x * jax.nn.sigmoid(x)


def _softplus(x):
    return jnp.maximum(x, 0.0) + jnp.log1p(jnp.exp(-jnp.abs(x)))


def _pair_swap(x, half):
    n = x.shape[-1]
    lane = lax.broadcasted_iota(jnp.int32, x.shape, x.ndim - 1)
    up = pltpu.roll(x, n - half, axis=x.ndim - 1)
    down = pltpu.roll(x, half, axis=x.ndim - 1)
    return jnp.where(lane % (2 * half) < half, up, down)


def _rope(x, cos, sin, half):
    return x * cos + _pair_swap(x, half) * sin


def _ada_kernel(cond_ref, w_ref, b_ref, o_ref):
    c = cond_ref[...]
    a = _silu(c).astype(BF16)
    o_ref[...] = jnp.dot(a, w_ref[...].astype(BF16), preferred_element_type=F32) + b_ref[...]


def _ada_call(cond, w_ada, b_ada):
    L, D, D3 = w_ada.shape
    tn = 1024
    return pl.pallas_call(
        _ada_kernel,
        out_shape=jax.ShapeDtypeStruct((L, 8, D3), F32),
        grid=(L, D3 // tn),
        in_specs=[
            pl.BlockSpec((8, D), lambda l, n: (0, 0)),
            pl.BlockSpec((None, D, tn), lambda l, n: (l, 0, n)),
            pl.BlockSpec((None, 1, tn), lambda l, n: (l, 0, n)),
        ],
        out_specs=pl.BlockSpec((None, 8, tn), lambda l, n: (l, 0, n)),
        compiler_params=_cparams(("parallel", "parallel")),
        name="ada_mod",
    )(cond, w_ada, b_ada.reshape(L, 1, D3))


def _ln0_kernel(x_ref, o_ref):
    x = x_ref[...]
    mu = jnp.mean(x, axis=-1, keepdims=True)
    xc = x - mu
    var = jnp.mean(xc * xc, axis=-1, keepdims=True)
    o_ref[...] = xc * lax.rsqrt(var + LN_EPS)


def _ln0_call(xs):
    B, S, D = xs.shape
    tm = 256
    return pl.pallas_call(
        _ln0_kernel,
        out_shape=jax.ShapeDtypeStruct((B, S, D), F32),
        grid=(B, S // tm),
        in_specs=[pl.BlockSpec((None, tm, D), lambda b, j: (b, j, 0))],
        out_specs=pl.BlockSpec((None, tm, D), lambda b, j: (b, j, 0)),
        compiler_params=_cparams(("parallel", "parallel")),
        name="ln_entry",
    )(xs)


def _inproj_kernel(h_ref, mod_ref, modc_ref, w_ref, u_ref, xs_ref, *, tm, ctx):
    j = pl.program_id(1)
    n = pl.program_id(2)

    @pl.when(n == 0)
    def _():
        rows = j * tm + lax.broadcasted_iota(jnp.int32, (tm, 1), 0)
        is_ctx = rows < ctx
        sh = jnp.where(is_ctx, modc_ref[0:1, :], mod_ref[0:1, :])
        sc = jnp.where(is_ctx, modc_ref[1:2, :], mod_ref[1:2, :])
        xs_ref[...] = (h_ref[...] * (1.0 + sc) + sh).astype(BF16)

    u_ref[...] = jnp.dot(xs_ref[...], w_ref[n], preferred_element_type=F32).astype(u_ref.dtype)


def _inproj_call(h, mod, w3, ctx, tm):
    B, S, D = h.shape
    nch, _, tn = w3.shape
    return pl.pallas_call(
        functools.partial(_inproj_kernel, tm=tm, ctx=ctx),
        out_shape=jax.ShapeDtypeStruct((B, S, nch * tn), F32),
        grid=(B, S // tm, nch),
        in_specs=[
            pl.BlockSpec((None, tm, D), lambda b, j, n: (b, j, 0)),
            pl.BlockSpec((None, 8, D), lambda b, j, n: (b, 0, 0)),
            pl.BlockSpec((None, 8, D), lambda b, j, n: (B, 0, 0)),
            pl.BlockSpec((nch, D, tn), lambda b, j, n: (0, 0, 0), pipeline_mode=pl.Buffered(1)),
        ],
        out_specs=pl.BlockSpec((None, tm, tn), lambda b, j, n: (b, j, n)),
        scratch_shapes=[pltpu.VMEM((tm, D), BF16)],
        compiler_params=_cparams(("parallel", "parallel", "arbitrary")),
        name="in_proj",
    )(h, mod, mod, w3)


def _rms(x, g):
    return x * lax.rsqrt(jnp.mean(x * x, axis=-1, keepdims=True) + RMS_EPS) * g


def _mla_prep_kernel(ql_ref, kvl_ref, kr_ref, gq_ref, gkv_ref, wuq_ref, wk_ref, wvt_ref,
                     cos_ref, sin_ref, q_ref, k_ref, vt_ref):
    cos = cos_ref[...]
    sin = sin_ref[...]
    half = MLA_ROPE // 4

    zq = _rms(ql_ref[...].astype(F32), gq_ref[...]).astype(BF16)
    q = jnp.dot(zq, wuq_ref[...], preferred_element_type=F32)
    for hd in range(MLA_HEADS):
        c0 = hd * MLA_HP
        q_ref[:, c0:c0 + MLA_NOPE] = (q[:, c0:c0 + MLA_NOPE] * MLA_SCALE).astype(q_ref.dtype)
        qr = _rope(q[:, c0 + MLA_NOPE:c0 + MLA_HP], cos, sin, half)
        q_ref[:, c0 + MLA_NOPE:c0 + MLA_HP] = (qr * MLA_SCALE).astype(q_ref.dtype)

    zk = _rms(kvl_ref[...].astype(F32), gkv_ref[...]).astype(BF16)
    kn = jnp.dot(zk, wk_ref[...], preferred_element_type=F32)
    kr = _rope(kr_ref[...].astype(F32), cos, sin, half).astype(k_ref.dtype)
    for hd in range(MLA_HEADS):
        c0 = hd * MLA_HP
        k_ref[:, c0:c0 + MLA_NOPE] = kn[:, hd * MLA_NOPE:(hd + 1) * MLA_NOPE].astype(k_ref.dtype)
        k_ref[:, c0 + MLA_NOPE:c0 + MLA_HP] = kr
    vt = lax.dot_general(wvt_ref[...], zk, (((1,), (1,)), ((), ())), preferred_element_type=F32)
    vt_ref[...] = vt.astype(vt_ref.dtype)


def _mla_prep_call(u, gq, gkv, wuq, wk, wvt, cos, sin):
    B, S, _ = u.shape
    tp = 256
    HW = MLA_HEADS * MLA_HP
    return pl.pallas_call(
        _mla_prep_kernel,
        out_shape=(
            jax.ShapeDtypeStruct((B, S, HW), BF16),
            jax.ShapeDtypeStruct((B, S, HW), BF16),
            jax.ShapeDtypeStruct((B, MLA_W, S), BF16),
        ),
        grid=(B, S // tp),
        in_specs=[
            pl.BlockSpec((None, tp, MLA_Q_RANK), lambda b, j: (b, j, U_QLAT // MLA_Q_RANK)),
            pl.BlockSpec((None, tp, MLA_KV_RANK), lambda b, j: (b, j, U_KVLAT // MLA_KV_RANK)),
            pl.BlockSpec((None, tp, LANES), lambda b, j: (b, j, U_KR // LANES)),
            pl.BlockSpec((1, MLA_Q_RANK), lambda b, j: (0, 0)),
            pl.BlockSpec((1, MLA_KV_RANK), lambda b, j: (0, 0)),
            pl.BlockSpec((MLA_Q_RANK, HW), lambda b, j: (0, 0)),
            pl.BlockSpec((MLA_KV_RANK, MLA_W), lambda b, j: (0, 0)),
            pl.BlockSpec((MLA_W, MLA_KV_RANK), lambda b, j: (0, 0)),
            pl.BlockSpec((tp, LANES), lambda b, j: (j, 0)),
            pl.BlockSpec((tp, LANES), lambda b, j: (j, 0)),
        ],
        out_specs=(
            pl.BlockSpec((None, tp, HW), lambda b, j: (b, j, 0)),
            pl.BlockSpec((None, tp, HW), lambda b, j: (b, j, 0)),
            pl.BlockSpec((None, MLA_W, tp), lambda b, j: (b, 0, j)),
        ),
        compiler_params=_cparams(("parallel", "parallel")),
        name="mla_prep",
    )(u, u, u, gq, gkv, wuq, wk, wvt, cos, sin)


def _attn_kernel(q_ref, k_ref, vt_ref, g_ref, y_ref, *, ctx, skip_ctx):
    q = q_ref[...]

    def attend(nk):
        st = lax.dot_general(k_ref[0:nk, :], q, (((1,), (1,)), ((), ())), preferred_element_type=F32)
        m = jnp.max(st, axis=0, keepdims=True)
        p = jnp.exp(st - m)
        l = jnp.sum(p, axis=0, keepdims=True)
        ot = jnp.dot(vt_ref[:, 0:nk], p.astype(BF16), preferred_element_type=F32)
        o = (ot / l).T
        y_ref[...] = (o * _silu(g_ref[...].astype(F32))).astype(y_ref.dtype)

    if skip_ctx:
        attend(k_ref.shape[0])
    else:
        j = pl.program_id(2)

        @pl.when(j == 0)
        def _():
            attend(ctx)

        @pl.when(j > 0)
        def _():
            attend(k_ref.shape[0])


def _attn_call(q, k, vt, u, ctx, skip_ctx):
    B, S, _ = q.shape
    tq = 256
    assert ctx == tq
    off = 1 if skip_ctx else 0
    nq = S // tq - off
    gcol = U_MLA_G // MLA_V
    return pl.pallas_call(
        functools.partial(_attn_kernel, ctx=ctx, skip_ctx=skip_ctx),
        out_shape=jax.ShapeDtypeStruct((B, S, MLA_W), BF16),
        grid=(B, MLA_HEADS, nq),
        in_specs=[
            pl.BlockSpec((None, tq, MLA_HP), lambda b, h, j: (b, j + off, h)),
            pl.BlockSpec((None, S, MLA_HP), lambda b, h, j: (b, 0, h)),
            pl.BlockSpec((None, MLA_V, S), lambda b, h, j: (b, h, 0)),
            pl.BlockSpec((None, tq, MLA_V), lambda b, h, j: (b, j + off, gcol + h)),
        ],
        out_specs=pl.BlockSpec((None, tq, MLA_V), lambda b, h, j: (b, j + off, h)),
        compiler_params=_cparams(("parallel", "parallel", "arbitrary")),
        name="mla_attn",
    )(q, k, vt, u)


def _lru_kernel(x_ref, g_ref, cw_ref, cb_ref, wg_ref, bg_ref, lam_ref, y_ref,
                af_ref, bf_ref, ab_ref, bb_ref, hf_ref, hb_ref, *, ctx):
    S, CB = x_ref.shape
    x = x_ref[...].astype(F32)
    rows = lax.broadcasted_iota(jnp.int32, (S, 1), 0)
    in_ctx = rows < ctx

    xc = x * cw_ref[2:3, :] + cb_ref[...]
    for tap, off in ((0, -2), (1, -1), (3, 1)):
        xs = pltpu.roll(x, (-off) % S, axis=0)
        src = rows + off
        valid = (src >= 0) & (src < S) & ((src < ctx) == in_ctx)
        xc = xc + jnp.where(valid, xs, 0.0) * cw_ref[tap:tap + 1, :]

    xb = xc.astype(BF16)
    for d, (a_ref, b_ref) in enumerate(((af_ref, bf_ref), (ab_ref, bb_ref))):
        pre = jnp.dot(xb, wg_ref[d], preferred_element_type=F32) + bg_ref[d]
        r = jax.nn.sigmoid(pre[:, :CB])
        i = jax.nn.sigmoid(pre[:, CB:])
        log_a = -LRU_C * r * _softplus(-lam_ref[d:d + 1, :])
        a = jnp.exp(log_a)
        a_ref[...] = a
        b_ref[...] = jnp.sqrt((1.0 - a) * (1.0 + a)) * (i * xc)

    def step(tf, tb, hf, hb):
        hf = af_ref[pl.ds(tf, 1), :] * hf + bf_ref[pl.ds(tf, 1), :]
        hb = ab_ref[pl.ds(tb, 1), :] * hb + bb_ref[pl.ds(tb, 1), :]
        hf_ref[pl.ds(tf, 1), :] = hf
        hb_ref[pl.ds(tb, 1), :] = hb
        return hf, hb

    zero = jnp.zeros((1, CB), F32)
    carry = lax.fori_loop(0, ctx, lambda t, c: step(t, ctx - 1 - t, *c), (zero, zero), unroll=8)
    lax.fori_loop(0, S - ctx, lambda t, c: step(ctx + t, S - 1 - t, *c), carry, unroll=8)

    y = (hf_ref[...] + hb_ref[...]) * _silu(g_ref[...].astype(F32))
    y_ref[...] = y.astype(y_ref.dtype)


def _lru_call(u, cw, cb, wg, bg, lam, ctx):
    B, S, _ = u.shape
    ncb = LRU_W // LRU_CB
    return pl.pallas_call(
        functools.partial(_lru_kernel, ctx=ctx),
        out_shape=jax.ShapeDtypeStruct((B, S, LRU_W), BF16),
        grid=(B, ncb),
        in_specs=[
            pl.BlockSpec((None, S, LRU_CB), lambda b, c: (b, 0, U_LRU_X // LRU_CB + c)),
            pl.BlockSpec((None, S, LRU_CB), lambda b, c: (b, 0, U_LRU_G // LRU_CB + c)),
            pl.BlockSpec((LRU_CONV, LRU_CB), lambda b, c: (0, c)),
            pl.BlockSpec((1, LRU_CB), lambda b, c: (0, c)),
            pl.BlockSpec((2, None, LRU_CB, 2 * LRU_CB), lambda b, c: (0, c, 0, 0)),
            pl.BlockSpec((2, None, 1, 2 * LRU_CB), lambda b, c: (0, c, 0, 0)),
            pl.BlockSpec((2, LRU_CB), lambda b, c: (0, c)),
        ],
        out_specs=pl.BlockSpec((None, S, LRU_CB), lambda b, c: (b, 0, c)),
        scratch_shapes=[pltpu.VMEM((S, LRU_CB), F32) for _ in range(6)],
        compiler_params=_cparams(("parallel", "parallel")),
        name="rg_lru",
    )(u, u, cw, cb, wg, bg, lam)


def _ret_kernel(lg_ref, q_ref, k_ref, v_ref, g_ref, cos_ref, sin_ref, y_ref, rb_ref, *, n_ctx):
    C = RET_CHUNK
    S = q_ref.shape[0]
    n_all = S // C
    hd = pl.program_id(1)
    lgf = lg_ref[0, hd]
    lgb = lg_ref[1, hd]

    ri = lax.broadcasted_iota(jnp.int32, (C, C), 0).astype(F32)
    ci = lax.broadcasted_iota(jnp.int32, (C, C), 1).astype(F32)
    diff = ri - ci
    decay = jnp.where(diff >= 0, jnp.exp(jnp.maximum(diff, 0.0) * lgf),
                      jnp.exp(jnp.maximum(-diff, 0.0) * lgb))
    zeta_f = jnp.exp((C - 1 - ri) * lgf)
    xi_f = jnp.exp((ri + 1) * lgf)
    gc_f = jnp.exp(jnp.full((C, C), C, F32) * lgf)
    zeta_b = jnp.exp(ri * lgb)
    xi_b = jnp.exp((C - ri) * lgb)
    gc_b = jnp.exp(jnp.full((C, C), C, F32) * lgb)
    half = RET_DH // 4

    def rows_of(c):
        return pl.ds(pl.multiple_of(c * C, C), C)

    def load_kv(sl):
        k = _rope(k_ref[sl, :].astype(F32), cos_ref[sl, :], sin_ref[sl, :], half) * RET_K_SCALE
        return k, v_ref[sl, :].astype(BF16)

    def kv_state(k, zeta, v):
        return lax.dot_general((k * zeta).astype(BF16), v, (((0,), (0,)), ((), ())),
                               preferred_element_type=F32)

    def back(i, r):
        c = jnp.where(i < n_ctx, n_ctx - 1 - i, n_all - 1 - (i - n_ctx))
        k, v = load_kv(rows_of(c))
        rb_ref[c] = r
        return gc_b * r + kv_state(k, zeta_b, v)

    lax.fori_loop(0, n_all, back, jnp.zeros((C, C), F32))

    def fwd(c, r):
        sl = rows_of(c)
        k, v = load_kv(sl)
        q = _rope(q_ref[sl, :].astype(F32), cos_ref[sl, :], sin_ref[sl, :], half)
        s = lax.dot_general(q.astype(BF16), k.astype(BF16), (((1,), (1,)), ((), ())),
                            preferred_element_type=F32)
        o = jnp.dot((s * decay).astype(BF16), v, preferred_element_type=F32)
        o = o + jnp.dot((q * xi_f).astype(BF16), r.astype(BF16), preferred_element_type=F32)
        o = o + jnp.dot((q * xi_b).astype(BF16), rb_ref[c].astype(BF16), preferred_element_type=F32)
        mu = jnp.mean(o, axis=-1, keepdims=True)
        oc = o - mu
        var = jnp.mean(oc * oc, axis=-1, keepdims=True)
        y = oc * lax.rsqrt(var + LN_EPS) * _silu(g_ref[sl, :].astype(F32))
        y_ref[sl, :] = y.astype(y_ref.dtype)
        return gc_f * r + kv_state(k, zeta_f, v)

    lax.fori_loop(0, n_all, fwd, jnp.zeros((C, C), F32))


def _ret_call(u, log_g, cos, sin, ctx):
    B, S, _ = u.shape
    n_all = S // RET_CHUNK

    def col(base):
        return lambda b, h: (b, 0, base // RET_DH + h)

    return pl.pallas_call(
        functools.partial(_ret_kernel, n_ctx=ctx // RET_CHUNK),
        out_shape=jax.ShapeDtypeStruct((B, S, RET_W), BF16),
        grid=(B, RET_HEADS),
        in_specs=[
            pl.BlockSpec(memory_space=pltpu.SMEM),
            pl.BlockSpec((None, S, RET_DH), col(U_RET_Q)),
            pl.BlockSpec((None, S, RET_DH), col(U_RET_K)),
            pl.BlockSpec((None, S, RET_DH), col(U_RET_V)),
            pl.BlockSpec((None, S, RET_DH), col(U_RET_G)),
            pl.BlockSpec((S, RET_DH), lambda b, h: (0, 0)),
            pl.BlockSpec((S, RET_DH), lambda b, h: (0, 0)),
        ],
        out_specs=pl.BlockSpec((None, S, RET_DH), lambda b, h: (b, 0, h)),
        scratch_shapes=[pltpu.VMEM((n_all, RET_CHUNK, RET_CHUNK), F32)],
        compiler_params=_cparams(("parallel", "parallel")),
        name="retention",
    )(log_g, u, u, u, u, cos, sin)


def _outproj_kernel(ym_ref, yl_ref, yr_ref, h_ref, mod_ref, modc_ref, w_ref, lng_ref, lnb_ref, o_ref,
                    *, tm, ctx, alpha):
    j = pl.program_id(1)
    rows = j * tm + lax.broadcasted_iota(jnp.int32, (tm, 1), 0)
    gt = jnp.where(rows < ctx, modc_ref[2:3, :], mod_ref[2:3, :])
    acc = jnp.dot(ym_ref[...], w_ref[0:MLA_W, :], preferred_element_type=F32)
    acc = acc + jnp.dot(yl_ref[...], w_ref[MLA_W:MLA_W + LRU_W, :], preferred_element_type=F32)
    acc = acc + jnp.dot(yr_ref[...], w_ref[MLA_W + LRU_W:, :], preferred_element_type=F32)
    z = alpha * h_ref[...] + gt * acc
    mu = jnp.mean(z, axis=-1, keepdims=True)
    zc = z - mu
    var = jnp.mean(zc * zc, axis=-1, keepdims=True)
    o_ref[...] = zc * lax.rsqrt(var + LN_EPS) * lng_ref[...] + lnb_ref[...]


def _outproj_call(ym, yl, yr, h, mod, w, lng, lnb, ctx, tm, alpha):
    B, S, D = h.shape
    return pl.pallas_call(
        functools.partial(_outproj_kernel, tm=tm, ctx=ctx, alpha=alpha),
        out_shape=jax.ShapeDtypeStruct((B, S, D), F32),
        grid=(B, S // tm),
        in_specs=[
            pl.BlockSpec((None, tm, MLA_W), lambda b, j: (b, j, 0)),
            pl.BlockSpec((None, tm, LRU_W), lambda b, j: (b, j, 0)),
            pl.BlockSpec((None, tm, RET_W), lambda b, j: (b, j, 0)),
            pl.BlockSpec((None, tm, D), lambda b, j: (b, j, 0)),
            pl.BlockSpec((None, 8, D), lambda b, j: (b, 0, 0)),
            pl.BlockSpec((None, 8, D), lambda b, j: (B, 0, 0)),
            pl.BlockSpec((D, D), lambda b, j: (0, 0), pipeline_mode=pl.Buffered(1)),
            pl.BlockSpec((1, D), lambda b, j: (0, 0)),
            pl.BlockSpec((1, D), lambda b, j: (0, 0)),
        ],
        out_specs=pl.BlockSpec((None, tm, D), lambda b, j: (b, j, 0)),
        compiler_params=_cparams(("parallel", "parallel")),
        name="out_proj_ln",
    )(ym, yl, yr, h, mod, mod, w, lng, lnb)


def _rope_tables(rows, dim, ctx):
    row = jnp.repeat(jnp.arange(rows, dtype=F32), GRID_W)
    col = jnp.tile(jnp.arange(GRID_W, dtype=F32), rows)
    quarter = dim // 4
    inv = ROPE_BASE ** (-jnp.arange(quarter, dtype=F32) / quarter)
    ang = jnp.stack([row[:, None] * inv, col[:, None] * inv], axis=1)
    cos, sin = jnp.cos(ang), jnp.sin(ang)
    cos_l = jnp.concatenate([cos[:, 0], cos[:, 0], cos[:, 1], cos[:, 1]], axis=-1)
    sin_l = jnp.concatenate([-sin[:, 0], sin[:, 0], -sin[:, 1], sin[:, 1]], axis=-1)
    pad = LANES - dim
    cos_l = jnp.pad(cos_l, ((0, 0), (0, pad)))
    sin_l = jnp.pad(sin_l, ((0, 0), (0, pad)))
    cos_c = jnp.pad(jnp.ones((ctx, dim), F32), ((0, 0), (0, pad)))
    sin_c = jnp.zeros((ctx, LANES), F32)
    return jnp.concatenate([cos_c, cos_l], axis=0), jnp.concatenate([sin_c, sin_l], axis=0)


def _layout_w_in(w_in):
    D = w_in.shape[0]
    o = 0
    parts = {}
    for name, width in (("qlat", 512), ("kvlat", 256), ("kr", 64), ("mla_g", 1024), ("lru_x", 512),
                        ("lru_g", 512), ("ret_q", 512), ("ret_k", 512), ("ret_v", 512), ("ret_g", 512)):
        parts[name] = w_in[:, o:o + width]
        o += width
    w = jnp.concatenate([parts["qlat"], parts["lru_x"], parts["lru_g"], parts["ret_q"], parts["ret_k"],
                         parts["ret_v"], parts["ret_g"], parts["mla_g"], parts["kvlat"], parts["kr"],
                         jnp.zeros((D, LANES - MLA_ROPE), w_in.dtype)], axis=1).astype(BF16)
    return w.reshape(D, U_CHUNKS, U_TN).transpose(1, 0, 2)


def _layout_w_uq(w_uq):
    w = w_uq.reshape(MLA_Q_RANK, MLA_HEADS, MLA_NOPE + MLA_ROPE)
    w = jnp.pad(w, ((0, 0), (0, 0), (0, MLA_HP - MLA_NOPE - MLA_ROPE)))
    return w.reshape(MLA_Q_RANK, MLA_HEADS * MLA_HP).astype(BF16)


def _layout_w_ukv(w_ukv):
    w = w_ukv.reshape(MLA_KV_RANK, MLA_HEADS, MLA_NOPE + MLA_V)
    wk = w[:, :, :MLA_NOPE].reshape(MLA_KV_RANK, MLA_W).astype(BF16)
    wvt = w[:, :, MLA_NOPE:].reshape(MLA_KV_RANK, MLA_W).T.astype(BF16)
    return wk, wvt


def _layout_lru_gates(w_r, b_r, w_i, b_i):
    eye = jnp.eye(LRU_BLOCKS, dtype=w_r.dtype)
    ncb = LRU_W // LRU_CB

    def dense(w):
        return jnp.einsum("dgij,gh->dgihj", w, eye).reshape(2, LRU_W, LRU_W)

    def diag_blocks(w):
        w = w.reshape(2, ncb, LRU_CB, ncb, LRU_CB)
        return jnp.stack([w[:, c, :, c, :] for c in range(ncb)], axis=1)

    wg = jnp.concatenate([diag_blocks(dense(w_r)), diag_blocks(dense(w_i))], axis=-1).astype(BF16)
    bg = jnp.concatenate([b_r.reshape(2, ncb, 1, LRU_CB), b_i.reshape(2, ncb, 1, LRU_CB)], axis=-1)
    return wg, bg


def kernel(x, c, ctx, c_ctx, w_ada, b_ada, w_in, mla_q_norm_g, mla_kv_norm_g, mla_w_uq, mla_w_ukv,
           lru_conv_w, lru_conv_b, lru_w_r, lru_b_r, lru_w_i, lru_b_i, lru_lambda, ret_decay,
           w_out, ln_g, ln_b):
    B, T, D = x.shape
    L = w_in.shape[0]
    n_ctx = ctx.shape[1]
    S = n_ctx + T
    assert D == 2 * MLA_W and w_in.shape[2] == 4928 and n_ctx % 256 == 0 and T % 256 == 0
    alpha = (2 * L) ** 0.25
    tm = 768 if S % 768 == 0 else 256

    cos_m, sin_m = _rope_tables(T // GRID_W, MLA_ROPE, n_ctx)
    cos_r, sin_r = _rope_tables(T // GRID_W, RET_DH, n_ctx)

    cond = jnp.zeros((8, D), F32).at[:B].set(c).at[B].set(c_ctx)
    mod_all = _ada_call(cond, w_ada, b_ada)
    mod_all = mod_all[:, :B + 1].reshape(L, B + 1, 3, D)
    mod_all = jnp.pad(mod_all, ((0, 0), (0, 0), (0, 5), (0, 0)))

    h = _ln0_call(jnp.concatenate([ctx, x], axis=1))
    log_g = jax.nn.log_sigmoid(ret_decay.astype(F32))

    for l in range(L):
        last = l == L - 1
        w3 = _layout_w_in(w_in[l])
        wuq = _layout_w_uq(mla_w_uq[l])
        wk, wvt = _layout_w_ukv(mla_w_ukv[l])
        wg, bg = _layout_lru_gates(lru_w_r[l], lru_b_r[l], lru_w_i[l], lru_b_i[l])

        u = _inproj_call(h, mod_all[l], w3, n_ctx, tm)
        q, k, vt = _mla_prep_call(u, mla_q_norm_g[l][None], mla_kv_norm_g[l][None], wuq, wk, wvt,
                                  cos_m, sin_m)
        y_mla = _attn_call(q, k, vt, u, n_ctx, last)
        y_lru = _lru_call(u, lru_conv_w[l], lru_conv_b[l][None], wg, bg, lru_lambda[l], n_ctx)
        y_ret = _ret_call(u, log_g[l], cos_r, sin_r, n_ctx)
        h = _outproj_call(y_mla, y_lru, y_ret, h, mod_all[l], w_out[l].astype(BF16),
                          ln_g[l][None], ln_b[l][None], n_ctx, tm, alpha)
    return h[:, n_ctx:, :]
```

```python
import functools

import jax
import jax.numpy as jnp
from jax import lax
from jax.experimental import pallas as pl
from jax.experimental.pallas import tpu as pltpu

F32 = jnp.float32
BF16 = jnp.bfloat16

GRID_W = 64
MLA_V = 128
MLA_NOPE = 128
MLA_ROPE = 64
MLA_HEADS = 8
MLA_W = MLA_HEADS * MLA_V
MLA_Q_RANK = 512
MLA_KV_RANK = 256
MLA_SCALE = (MLA_NOPE + MLA_ROPE) ** -0.5
MLA_HP = 256
LRU_W = 512
LRU_BLOCKS = 8
LRU_BW = LRU_W // LRU_BLOCKS
LRU_CONV = 4
LRU_C = 8.0
LRU_CB = 256
RET_HEADS = 4
RET_DH = 128
RET_W = RET_HEADS * RET_DH
RET_CHUNK = 128
RET_K_SCALE = RET_DH ** -0.5
ROPE_BASE = 10000.0
LN_EPS = 1e-5
RMS_EPS = 1e-6
LANES = 128

U_QLAT = 0
U_LRU_X = 512
U_LRU_G = 1024
U_RET_Q = 1536
U_RET_K = 2048
U_RET_V = 2560
U_RET_G = 3072
U_MLA_G = 3584
U_KVLAT = 4608
U_KR = 4864
U_W = 4992
U_CHUNKS = 3
U_TN = U_W // U_CHUNKS

VMEM_LIMIT = 56 * 1024 * 1024


def _cparams(sem):
    return pltpu.CompilerParams(dimension_semantics=sem, vmem_limit_bytes=VMEM_LIMIT)


def _sigmoid(x):
    return 0.5 * jnp.tanh(0.5 * x) + 0.5


def _silu(x):
    return x * _sigmoid(x)


def _softplus(x):
    return jnp.maximum(x, 0.0) + jnp.log1p(jnp.exp(-jnp.abs(x)))


def _pair_swap(x, half):
    n = x.shape[-1]
    lane = lax.broadcasted_iota(jnp.int32, x.shape, x.ndim - 1)
    up = pltpu.roll(x, n - half, axis=x.ndim - 1)
    down = pltpu.roll(x, half, axis=x.ndim - 1)
    return jnp.where(lane % (2 * half) < half, up, down)


def _rope(x, cos, sin, half):
    return x * cos + _pair_swap(x, half) * sin


def _ada_kernel(cond_ref, w_ref, b_ref, o_ref):
    c = cond_ref[...]
    a = _silu(c).astype(BF16)
    o_ref[...] = jnp.dot(a, w_ref[...].astype(BF16), preferred_element_type=F32) + b_ref[...]


def _ada_call(cond, w_ada, b_ada):
    L, D, D3 = w_ada.shape
    tn = 1024
    return pl.pallas_call(
        _ada_kernel,
        out_shape=jax.ShapeDtypeStruct((L, 8, D3), F32),
        grid=(L, D3 // tn),
        in_specs=[
            pl.BlockSpec((8, D), lambda l, n: (0, 0)),
            pl.BlockSpec((None, D, tn), lambda l, n: (l, 0, n)),
            pl.BlockSpec((None, 1, tn), lambda l, n: (l, 0, n)),
        ],
        out_specs=pl.BlockSpec((None, 8, tn), lambda l, n: (l, 0, n)),
        compiler_params=_cparams(("parallel", "parallel")),
        name="ada_mod",
    )(cond, w_ada, b_ada.reshape(L, 1, D3))


def _ln0_kernel(x_ref, o_ref):
    x = x_ref[...]
    mu = jnp.mean(x, axis=-1, keepdims=True)
    xc = x - mu
    var = jnp.mean(xc * xc, axis=-1, keepdims=True)
    o_ref[...] = xc * lax.rsqrt(var + LN_EPS)


def _ln0_call(xs):
    B, S, D = xs.shape
    tm = 256
    return pl.pallas_call(
        _ln0_kernel,
        out_shape=jax.ShapeDtypeStruct((B, S, D), F32),
        grid=(B, S // tm),
        in_specs=[pl.BlockSpec((None, tm, D), lambda b, j: (b, j, 0))],
        out_specs=pl.BlockSpec((None, tm, D), lambda b, j: (b, j, 0)),
        compiler_params=_cparams(("parallel", "parallel")),
        name="ln_entry",
    )(xs)


def _inproj_kernel(h_ref, mod_ref, modc_ref, w_ref, u_ref, xs_ref, *, tm, ctx):
    j = pl.program_id(1)
    n = pl.program_id(2)

    @pl.when(n == 0)
    def _():
        rows = j * tm + lax.broadcasted_iota(jnp.int32, (tm, 1), 0)
        is_ctx = rows < ctx
        sh = jnp.where(is_ctx, modc_ref[0:1, :], mod_ref[0:1, :])
        sc = jnp.where(is_ctx, modc_ref[1:2, :], mod_ref[1:2, :])
        xs_ref[...] = (h_ref[...] * (1.0 + sc) + sh).astype(BF16)

    u_ref[...] = jnp.dot(xs_ref[...], w_ref[n], preferred_element_type=F32).astype(u_ref.dtype)


def _inproj_call(h, mod, w3, ctx, tm):
    B, S, D = h.shape
    nch, _, tn = w3.shape
    return pl.pallas_call(
        functools.partial(_inproj_kernel, tm=tm, ctx=ctx),
        out_shape=jax.ShapeDtypeStruct((B, S, nch * tn), F32),
        grid=(B, S // tm, nch),
        in_specs=[
            pl.BlockSpec((None, tm, D), lambda b, j, n: (b, j, 0)),
            pl.BlockSpec((None, 8, D), lambda b, j, n: (b, 0, 0)),
            pl.BlockSpec((None, 8, D), lambda b, j, n: (B, 0, 0)),
            pl.BlockSpec((nch, D, tn), lambda b, j, n: (0, 0, 0), pipeline_mode=pl.Buffered(1)),
        ],
        out_specs=pl.BlockSpec((None, tm, tn), lambda b, j, n: (b, j, n)),
        scratch_shapes=[pltpu.VMEM((tm, D), BF16)],
        compiler_params=_cparams(("parallel", "parallel", "arbitrary")),
        name="in_proj",
    )(h, mod, mod, w3)


def _rms(x, g):
    return x * lax.rsqrt(jnp.mean(x * x, axis=-1, keepdims=True) + RMS_EPS) * g


def _mla_prep_kernel(ql_ref, kvl_ref, kr_ref, gq_ref, gkv_ref, wuq_ref, wk_ref, wvt_ref,
                     cos_ref, sin_ref, q_ref, k_ref, vt_ref):
    cos = cos_ref[...]
    sin = sin_ref[...]
    half = MLA_ROPE // 4

    zq = _rms(ql_ref[...].astype(F32), gq_ref[...]).astype(BF16)
    q = jnp.dot(zq, wuq_ref[...], preferred_element_type=F32)
    for hd in range(MLA_HEADS):
        c0 = hd * MLA_HP
        q_ref[:, c0:c0 + MLA_NOPE] = (q[:, c0:c0 + MLA_NOPE] * MLA_SCALE).astype(q_ref.dtype)
        qr = _rope(q[:, c0 + MLA_NOPE:c0 + MLA_HP], cos, sin, half)
        q_ref[:, c0 + MLA_NOPE:c0 + MLA_HP] = (qr * MLA_SCALE).astype(q_ref.dtype)

    zk = _rms(kvl_ref[...].astype(F32), gkv_ref[...]).astype(BF16)
    kn = jnp.dot(zk, wk_ref[...], preferred_element_type=F32)
    kr = _rope(kr_ref[...].astype(F32), cos, sin, half).astype(k_ref.dtype)
    for hd in range(MLA_HEADS):
        c0 = hd * MLA_HP
        k_ref[:, c0:c0 + MLA_NOPE] = kn[:, hd * MLA_NOPE:(hd + 1) * MLA_NOPE].astype(k_ref.dtype)
        k_ref[:, c0 + MLA_NOPE:c0 + MLA_HP] = kr
    vt = lax.dot_general(wvt_ref[...], zk, (((1,), (1,)), ((), ())), preferred_element_type=F32)
    vt_ref[...] = vt.astype(vt_ref.dtype)


def _mla_prep_call(u, gq, gkv, wuq, wk, wvt, cos, sin):
    B, S, _ = u.shape
    tp = 256
    HW = MLA_HEADS * MLA_HP
    return pl.pallas_call(
        _mla_prep_kernel,
        out_shape=(
            jax.ShapeDtypeStruct((B, S, HW), BF16),
            jax.ShapeDtypeStruct((B, S, HW), BF16),
            jax.ShapeDtypeStruct((B, MLA_W, S), BF16),
        ),
        grid=(B, S // tp),
        in_specs=[
            pl.BlockSpec((None, tp, MLA_Q_RANK), lambda b, j: (b, j, U_QLAT // MLA_Q_RANK)),
            pl.BlockSpec((None, tp, MLA_KV_RANK), lambda b, j: (b, j, U_KVLAT // MLA_KV_RANK)),
            pl.BlockSpec((None, tp, LANES), lambda b, j: (b, j, U_KR // LANES)),
            pl.BlockSpec((1, MLA_Q_RANK), lambda b, j: (0, 0)),
            pl.BlockSpec((1, MLA_KV_RANK), lambda b, j: (0, 0)),
            pl.BlockSpec((MLA_Q_RANK, HW), lambda b, j: (0, 0)),
            pl.BlockSpec((MLA_KV_RANK, MLA_W), lambda b, j: (0, 0)),
            pl.BlockSpec((MLA_W, MLA_KV_RANK), lambda b, j: (0, 0)),
            pl.BlockSpec((tp, LANES), lambda b, j: (j, 0)),
            pl.BlockSpec((tp, LANES), lambda b, j: (j, 0)),
        ],
        out_specs=(
            pl.BlockSpec((None, tp, HW), lambda b, j: (b, j, 0)),
            pl.BlockSpec((None, tp, HW), lambda b, j: (b, j, 0)),
            pl.BlockSpec((None, MLA_W, tp), lambda b, j: (b, 0, j)),
        ),
        compiler_params=_cparams(("parallel", "parallel")),
        name="mla_prep",
    )(u, u, u, gq, gkv, wuq, wk, wvt, cos, sin)


def _attn_kernel(q_ref, k_ref, vt_ref, g_ref, y_ref, st0_ref, st1_ref, m0_ref, m1_ref, *, ctx, tq):
    j = pl.program_id(2)
    nq = pl.num_programs(2)
    S = k_ref.shape[0]
    slots = ((st0_ref, m0_ref), (st1_ref, m1_ref))

    def scores(blk, nk):
        qb = q_ref[pl.ds(pl.multiple_of(blk * tq, tq), tq), :]
        return lax.dot_general(k_ref[0:nk, :], qb, (((1,), (1,)), ((), ())), preferred_element_type=F32)

    def finish(st, m, nk):
        p = jnp.exp(st - m)
        l = jnp.sum(p, axis=0, keepdims=True)
        ot = jnp.dot(vt_ref[:, 0:nk], p.astype(BF16), preferred_element_type=F32)
        o = (ot / l).T
        y_ref[...] = (o * _silu(g_ref[...].astype(F32))).astype(y_ref.dtype)

    def stage_scores(blk, slot):
        st_ref, m_ref = slots[slot]
        st = scores(blk, S)
        st_ref[...] = st
        m_ref[...] = jnp.max(st, axis=0, keepdims=True)

    def stage_finish(slot):
        st_ref, m_ref = slots[slot]
        finish(st_ref[...], m_ref[...], S)

    @pl.when(j == 0)
    def _():
        st = scores(0, ctx)
        finish(st, jnp.max(st, axis=0, keepdims=True), ctx)
        stage_scores(1, 1)

    for par in (0, 1):
        @pl.when((j > 0) & (j < nq - 1) & (j % 2 == par))
        def _():
            stage_scores(j + 1, 1 - par)
            stage_finish(par)

        @pl.when((j == nq - 1) & (j % 2 == par))
        def _():
            stage_finish(par)


def _attn_call(q, k, vt, u, ctx):
    B, S, _ = q.shape
    tq = 256
    assert ctx == tq and S // tq >= 2
    gcol = U_MLA_G // MLA_V
    return pl.pallas_call(
        functools.partial(_attn_kernel, ctx=ctx, tq=tq),
        out_shape=jax.ShapeDtypeStruct((B, S, MLA_W), BF16),
        grid=(B, MLA_HEADS, S // tq),
        in_specs=[
            pl.BlockSpec((None, S, MLA_HP), lambda b, h, j: (b, 0, h)),
            pl.BlockSpec((None, S, MLA_HP), lambda b, h, j: (b, 0, h)),
            pl.BlockSpec((None, MLA_V, S), lambda b, h, j: (b, h, 0)),
            pl.BlockSpec((None, tq, MLA_V), lambda b, h, j: (b, j, gcol + h)),
        ],
        out_specs=pl.BlockSpec((None, tq, MLA_V), lambda b, h, j: (b, j, h)),
        scratch_shapes=[pltpu.VMEM((S, tq), F32), pltpu.VMEM((S, tq), F32),
                        pltpu.VMEM((1, tq), F32), pltpu.VMEM((1, tq), F32)],
        compiler_params=_cparams(("parallel", "parallel", "arbitrary")),
        name="mla_attn",
    )(q, k, vt, u)


def _lru_kernel(x_ref, g_ref, cw_ref, cb_ref, wg_ref, bg_ref, lam_ref, y_ref,
                af_ref, bf_ref, ab_ref, bb_ref, hf_ref, hb_ref, *, ctx):
    S, CB = x_ref.shape
    x = x_ref[...].astype(F32)
    rows = lax.broadcasted_iota(jnp.int32, (S, 1), 0)
    in_ctx = rows < ctx

    xc = x * cw_ref[2:3, :] + cb_ref[...]
    for tap, off in ((0, -2), (1, -1), (3, 1)):
        xs = pltpu.roll(x, (-off) % S, axis=0)
        src = rows + off
        valid = (src >= 0) & (src < S) & ((src < ctx) == in_ctx)
        xc = xc + jnp.where(valid, xs, 0.0) * cw_ref[tap:tap + 1, :]

    xb = xc.astype(BF16)
    for d, (a_ref, b_ref) in enumerate(((af_ref, bf_ref), (ab_ref, bb_ref))):
        pre = jnp.dot(xb, wg_ref[d], preferred_element_type=F32) + bg_ref[d]
        r = _sigmoid(pre[:, :CB])
        i = _sigmoid(pre[:, CB:])
        a = jnp.exp(r * (-LRU_C * _softplus(-lam_ref[d:d + 1, :])))
        w = (1.0 - a) * (1.0 + a)
        a_ref[...] = a
        b_ref[...] = jnp.where(w > 0.0, w * lax.rsqrt(w), 0.0) * (i * xc)

    R = 8
    n_t, n_ctx_t = S // R, ctx // R
    row = lax.broadcasted_iota(jnp.int32, (R, CB), 0)

    def tile_prefix(a, b, rev):
        for d in (1, 2, 4):
            shift = R - d if rev else d
            keep = (row < R - d) if rev else (row >= d)
            a_s = pltpu.roll(a, shift, axis=0)
            b_s = pltpu.roll(b, shift, axis=0)
            b = jnp.where(keep, a * b_s, 0.0) + b
            a = jnp.where(keep, a * a_s, a)
        return a, b

    def tile_step(i, carry):
        hf, hb = carry
        sf = pl.ds(pl.multiple_of(i * R, R), R)
        ib = jnp.where(i < n_ctx_t, n_ctx_t - 1 - i, n_t - 1 - (i - n_ctx_t))
        sb = pl.ds(pl.multiple_of(ib * R, R), R)
        pa, pb = tile_prefix(af_ref[sf, :], bf_ref[sf, :], False)
        of = pa * hf + pb
        qa, qb = tile_prefix(ab_ref[sb, :], bb_ref[sb, :], True)
        ob = qa * hb + qb
        hf_ref[sf, :] = of
        hb_ref[sb, :] = ob
        return (jnp.broadcast_to(of[R - 1:R, :], (R, CB)), jnp.broadcast_to(ob[0:1, :], (R, CB)))

    zero = jnp.zeros((R, CB), F32)
    lax.fori_loop(0, n_t, tile_step, (zero, zero), unroll=4)

    y = (hf_ref[...] + hb_ref[...]) * _silu(g_ref[...].astype(F32))
    y_ref[...] = y.astype(y_ref.dtype)


def _lru_call(u, cw, cb, wg, bg, lam, ctx):
    B, S, _ = u.shape
    ncb = LRU_W // LRU_CB
    return pl.pallas_call(
        functools.partial(_lru_kernel, ctx=ctx),
        out_shape=jax.ShapeDtypeStruct((B, S, LRU_W), BF16),
        grid=(B, ncb),
        in_specs=[
            pl.BlockSpec((None, S, LRU_CB), lambda b, c: (b, 0, U_LRU_X // LRU_CB + c)),
            pl.BlockSpec((None, S, LRU_CB), lambda b, c: (b, 0, U_LRU_G // LRU_CB + c)),
            pl.BlockSpec((LRU_CONV, LRU_CB), lambda b, c: (0, c)),
            pl.BlockSpec((1, LRU_CB), lambda b, c: (0, c)),
            pl.BlockSpec((2, None, LRU_CB, 2 * LRU_CB), lambda b, c: (0, c, 0, 0)),
            pl.BlockSpec((2, None, 1, 2 * LRU_CB), lambda b, c: (0, c, 0, 0)),
            pl.BlockSpec((2, LRU_CB), lambda b, c: (0, c)),
        ],
        out_specs=pl.BlockSpec((None, S, LRU_CB), lambda b, c: (b, 0, c)),
        scratch_shapes=[pltpu.VMEM((S, LRU_CB), F32) for _ in range(6)],
        compiler_params=_cparams(("parallel", "parallel")),
        name="rg_lru",
    )(u, u, cw, cb, wg, bg, lam)


def _ret_kernel(lg_ref, q_ref, k_ref, v_ref, g_ref, cos_ref, sin_ref, y_ref, ks_ref, kv_ref, st_ref,
                *, n_ctx):
    C = RET_CHUNK
    S = q_ref.shape[0]
    n_all = S // C
    unroll = 3 if n_all % 3 == 0 else (2 if n_all % 2 == 0 else 1)
    hd = pl.program_id(1)
    lgf = lg_ref[0, hd]
    lgb = lg_ref[1, hd]

    ri = lax.broadcasted_iota(jnp.int32, (C, C), 0).astype(F32)
    ci = lax.broadcasted_iota(jnp.int32, (C, C), 1).astype(F32)
    diff = ri - ci
    decay = jnp.where(diff >= 0, jnp.exp(jnp.maximum(diff, 0.0) * lgf),
                      jnp.exp(jnp.maximum(-diff, 0.0) * lgb))
    zeta_f = jnp.exp((C - 1 - ri) * lgf)
    xi_f = jnp.exp((ri + 1) * lgf)
    gc_f = jnp.exp(jnp.full((C, C), C, F32) * lgf)
    zeta_b = jnp.exp(ri * lgb)
    xi_b = jnp.exp((C - ri) * lgb)
    gc_b = jnp.exp(jnp.full((C, C), C, F32) * lgb)
    half = RET_DH // 4

    def rows_of(c):
        return pl.ds(pl.multiple_of(c * C, C), C)

    def chunk_kv(c, carry):
        sl = rows_of(c)
        k = _rope(k_ref[sl, :].astype(F32), cos_ref[sl, :], sin_ref[sl, :], half) * RET_K_SCALE
        ks_ref[sl, :] = k.astype(BF16)
        kz = jnp.concatenate([k * zeta_f, k * zeta_b], axis=1).astype(BF16)
        kv_ref[c] = lax.dot_general(kz, v_ref[sl, :].astype(BF16), (((0,), (0,)), ((), ())),
                                    preferred_element_type=F32)
        return carry

    lax.fori_loop(0, n_all, chunk_kv, 0, unroll=unroll)

    def fwd_state(c, r):
        st_ref[c, 0:C, :] = r.astype(BF16)
        return gc_f * r + kv_ref[c, 0:C, :]

    def bwd_state(i, r):
        c = jnp.where(i < n_ctx, n_ctx - 1 - i, n_all - 1 - (i - n_ctx))
        st_ref[c, C:2 * C, :] = r.astype(BF16)
        return gc_b * r + kv_ref[c, C:2 * C, :]

    lax.fori_loop(0, n_all, fwd_state, jnp.zeros((C, C), F32))
    lax.fori_loop(0, n_all, bwd_state, jnp.zeros((C, C), F32))

    def chunk_out(c, carry):
        sl = rows_of(c)
        q = _rope(q_ref[sl, :].astype(F32), cos_ref[sl, :], sin_ref[sl, :], half)
        s = lax.dot_general(q.astype(BF16), ks_ref[sl, :], (((1,), (1,)), ((), ())),
                            preferred_element_type=F32)
        o = jnp.dot((s * decay).astype(BF16), v_ref[sl, :].astype(BF16), preferred_element_type=F32)
        qx = jnp.concatenate([q * xi_f, q * xi_b], axis=1).astype(BF16)
        o = o + jnp.dot(qx, st_ref[c], preferred_element_type=F32)
        mu = jnp.mean(o, axis=-1, keepdims=True)
        oc = o - mu
        var = jnp.mean(oc * oc, axis=-1, keepdims=True)
        y = oc * lax.rsqrt(var + LN_EPS) * _silu(g_ref[sl, :].astype(F32))
        y_ref[sl, :] = y.astype(y_ref.dtype)
        return carry

    lax.fori_loop(0, n_all, chunk_out, 0, unroll=unroll)


def _ret_call(u, log_g, cos, sin, ctx):
    B, S, _ = u.shape
    n_all = S // RET_CHUNK

    def col(base):
        return lambda b, h: (b, 0, base // RET_DH + h)

    return pl.pallas_call(
        functools.partial(_ret_kernel, n_ctx=ctx // RET_CHUNK),
        out_shape=jax.ShapeDtypeStruct((B, S, RET_W), BF16),
        grid=(B, RET_HEADS),
        in_specs=[
            pl.BlockSpec(memory_space=pltpu.SMEM),
            pl.BlockSpec((None, S, RET_DH), col(U_RET_Q)),
            pl.BlockSpec((None, S, RET_DH), col(U_RET_K)),
            pl.BlockSpec((None, S, RET_DH), col(U_RET_V)),
            pl.BlockSpec((None, S, RET_DH), col(U_RET_G)),
            pl.BlockSpec((S, RET_DH), lambda b, h: (0, 0)),
            pl.BlockSpec((S, RET_DH), lambda b, h: (0, 0)),
        ],
        out_specs=pl.BlockSpec((None, S, RET_DH), lambda b, h: (b, 0, h)),
        scratch_shapes=[pltpu.VMEM((S, RET_DH), BF16),
                        pltpu.VMEM((n_all, 2 * RET_CHUNK, RET_DH), F32),
                        pltpu.VMEM((n_all, 2 * RET_CHUNK, RET_DH), BF16)],
        compiler_params=_cparams(("parallel", "parallel")),
        name="retention",
    )(log_g, u, u, u, u, cos, sin)


def _outproj_kernel(ym_ref, yl_ref, yr_ref, h_ref, mod_ref, modc_ref, w_ref, lng_ref, lnb_ref, o_ref,
                    *, tm, ctx, alpha):
    j = pl.program_id(1)
    rows = j * tm + lax.broadcasted_iota(jnp.int32, (tm, 1), 0)
    gt = jnp.where(rows < ctx, modc_ref[2:3, :], mod_ref[2:3, :])
    acc = jnp.dot(ym_ref[...], w_ref[0:MLA_W, :], preferred_element_type=F32)
    acc = acc + jnp.dot(yl_ref[...], w_ref[MLA_W:MLA_W + LRU_W, :], preferred_element_type=F32)
    acc = acc + jnp.dot(yr_ref[...], w_ref[MLA_W + LRU_W:, :], preferred_element_type=F32)
    z = alpha * h_ref[...] + gt * acc
    mu = jnp.mean(z, axis=-1, keepdims=True)
    zc = z - mu
    var = jnp.mean(zc * zc, axis=-1, keepdims=True)
    o_ref[...] = zc * lax.rsqrt(var + LN_EPS) * lng_ref[...] + lnb_ref[...]


def _outproj_call(ym, yl, yr, h, mod, w, lng, lnb, ctx, tm, alpha):
    B, S, D = h.shape
    return pl.pallas_call(
        functools.partial(_outproj_kernel, tm=tm, ctx=ctx, alpha=alpha),
        out_shape=jax.ShapeDtypeStruct((B, S, D), F32),
        grid=(B, S // tm),
        in_specs=[
            pl.BlockSpec((None, tm, MLA_W), lambda b, j: (b, j, 0)),
            pl.BlockSpec((None, tm, LRU_W), lambda b, j: (b, j, 0)),
            pl.BlockSpec((None, tm, RET_W), lambda b, j: (b, j, 0)),
            pl.BlockSpec((None, tm, D), lambda b, j: (b, j, 0)),
            pl.BlockSpec((None, 8, D), lambda b, j: (b, 0, 0)),
            pl.BlockSpec((None, 8, D), lambda b, j: (B, 0, 0)),
            pl.BlockSpec((D, D), lambda b, j: (0, 0), pipeline_mode=pl.Buffered(1)),
            pl.BlockSpec((1, D), lambda b, j: (0, 0)),
            pl.BlockSpec((1, D), lambda b, j: (0, 0)),
        ],
        out_specs=pl.BlockSpec((None, tm, D), lambda b, j: (b, j, 0)),
        compiler_params=_cparams(("parallel", "parallel")),
        name="out_proj_ln",
    )(ym, yl, yr, h, mod, mod, w, lng, lnb)


def _rope_tables(rows, dim, ctx):
    row = jnp.repeat(jnp.arange(rows, dtype=F32), GRID_W)
    col = jnp.tile(jnp.arange(GRID_W, dtype=F32), rows)
    quarter = dim // 4
    inv = ROPE_BASE ** (-jnp.arange(quarter, dtype=F32) / quarter)
    ang = jnp.stack([row[:, None] * inv, col[:, None] * inv], axis=1)
    cos, sin = jnp.cos(ang), jnp.sin(ang)
    cos_l = jnp.concatenate([cos[:, 0], cos[:, 0], cos[:, 1], cos[:, 1]], axis=-1)
    sin_l = jnp.concatenate([-sin[:, 0], sin[:, 0], -sin[:, 1], sin[:, 1]], axis=-1)
    pad = LANES - dim
    cos_l = jnp.pad(cos_l, ((0, 0), (0, pad)))
    sin_l = jnp.pad(sin_l, ((0, 0), (0, pad)))
    cos_c = jnp.pad(jnp.ones((ctx, dim), F32), ((0, 0), (0, pad)))
    sin_c = jnp.zeros((ctx, LANES), F32)
    return jnp.concatenate([cos_c, cos_l], axis=0), jnp.concatenate([sin_c, sin_l], axis=0)


def _layout_w_in(w_in):
    D = w_in.shape[0]
    o = 0
    parts = {}
    for name, width in (("qlat", 512), ("kvlat", 256), ("kr", 64), ("mla_g", 1024), ("lru_x", 512),
                        ("lru_g", 512), ("ret_q", 512), ("ret_k", 512), ("ret_v", 512), ("ret_g", 512)):
        parts[name] = w_in[:, o:o + width]
        o += width
    w = jnp.concatenate([parts["qlat"], parts["lru_x"], parts["lru_g"], parts["ret_q"], parts["ret_k"],
                         parts["ret_v"], parts["ret_g"], parts["mla_g"], parts["kvlat"], parts["kr"],
                         jnp.zeros((D, LANES - MLA_ROPE), w_in.dtype)], axis=1).astype(BF16)
    return w.reshape(D, U_CHUNKS, U_TN).transpose(1, 0, 2)


def _layout_w_uq(w_uq):
    w = w_uq.reshape(MLA_Q_RANK, MLA_HEADS, MLA_NOPE + MLA_ROPE)
    w = jnp.pad(w, ((0, 0), (0, 0), (0, MLA_HP - MLA_NOPE - MLA_ROPE)))
    return w.reshape(MLA_Q_RANK, MLA_HEADS * MLA_HP).astype(BF16)


def _layout_w_ukv(w_ukv):
    w = w_ukv.reshape(MLA_KV_RANK, MLA_HEADS, MLA_NOPE + MLA_V)
    wk = w[:, :, :MLA_NOPE].reshape(MLA_KV_RANK, MLA_W).astype(BF16)
    wvt = w[:, :, MLA_NOPE:].reshape(MLA_KV_RANK, MLA_W).T.astype(BF16)
    return wk, wvt


def _layout_lru_gates(w_r, b_r, w_i, b_i):
    eye = jnp.eye(LRU_BLOCKS, dtype=w_r.dtype)
    ncb = LRU_W // LRU_CB

    def dense(w):
        return jnp.einsum("dgij,gh->dgihj", w, eye).reshape(2, LRU_W, LRU_W)

    def diag_blocks(w):
        w = w.reshape(2, ncb, LRU_CB, ncb, LRU_CB)
        return jnp.stack([w[:, c, :, c, :] for c in range(ncb)], axis=1)

    wg = jnp.concatenate([diag_blocks(dense(w_r)), diag_blocks(dense(w_i))], axis=-1).astype(BF16)
    bg = jnp.concatenate([b_r.reshape(2, ncb, 1, LRU_CB), b_i.reshape(2, ncb, 1, LRU_CB)], axis=-1)
    return wg, bg


def kernel(x, c, ctx, c_ctx, w_ada, b_ada, w_in, mla_q_norm_g, mla_kv_norm_g, mla_w_uq, mla_w_ukv,
           lru_conv_w, lru_conv_b, lru_w_r, lru_b_r, lru_w_i, lru_b_i, lru_lambda, ret_decay,
           w_out, ln_g, ln_b):
    B, T, D = x.shape
    L = w_in.shape[0]
    n_ctx = ctx.shape[1]
    S = n_ctx + T
    assert D == 2 * MLA_W and w_in.shape[2] == 4928 and n_ctx % 256 == 0 and T % 256 == 0
    alpha = (2 * L) ** 0.25
    tm = 768 if S % 768 == 0 else 256

    cos_m, sin_m = _rope_tables(T // GRID_W, MLA_ROPE, n_ctx)
    cos_r, sin_r = _rope_tables(T // GRID_W, RET_DH, n_ctx)

    cond = jnp.zeros((8, D), F32).at[:B].set(c).at[B].set(c_ctx)
    mod_all = _ada_call(cond, w_ada, b_ada)
    mod_all = mod_all[:, :B + 1].reshape(L, B + 1, 3, D)
    mod_all = jnp.pad(mod_all, ((0, 0), (0, 0), (0, 5), (0, 0)))

    h = _ln0_call(jnp.concatenate([ctx, x], axis=1))
    log_g = jax.nn.log_sigmoid(ret_decay.astype(F32))

    for l in range(L):
        last = l == L - 1
        w3 = _layout_w_in(w_in[l])
        wuq = _layout_w_uq(mla_w_uq[l])
        wk, wvt = _layout_w_ukv(mla_w_ukv[l])
        wg, bg = _layout_lru_gates(lru_w_r[l], lru_b_r[l], lru_w_i[l], lru_b_i[l])

        u = _inproj_call(h, mod_all[l], w3, n_ctx, tm)
        q, k, vt = _mla_prep_call(u, mla_q_norm_g[l][None], mla_kv_norm_g[l][None], wuq, wk, wvt,
                                  cos_m, sin_m)
        y_mla = _attn_call(q, k, vt, u, n_ctx)
        y_lru = _lru_call(u, lru_conv_w[l], lru_conv_b[l][None], wg, bg, lru_lambda[l], n_ctx)
        y_ret = _ret_call(u, log_g[l], cos_r, sin_r, n_ctx)
        h = _outproj_call(y_mla, y_lru, y_ret, h, mod_all[l], w_out[l].astype(BF16),
                          ln_g[l][None], ln_b[l][None], n_ctx, tm, alpha)
    return h[:, n_ctx:, :]
```

```python
import functools

import jax
import jax.numpy as jnp
from jax import lax
from jax.experimental import pallas as pl
from jax.experimental.pallas import tpu as pltpu

F32 = jnp.float32
BF16 = jnp.bfloat16

GRID_W = 64
MLA_V = 128
MLA_NOPE = 128
MLA_ROPE = 64
MLA_HEADS = 8
MLA_W = MLA_HEADS * MLA_V
MLA_Q_RANK = 512
MLA_KV_RANK = 256
MLA_SCALE = (MLA_NOPE + MLA_ROPE) ** -0.5
MLA_HP = 256
LRU_W = 512
LRU_BLOCKS = 8
LRU_BW = LRU_W // LRU_BLOCKS
LRU_CONV = 4
LRU_C = 8.0
LRU_CB = 256
RET_HEADS = 4
RET_DH = 128
RET_W = RET_HEADS * RET_DH
RET_CHUNK = 128
RET_K_SCALE = RET_DH ** -0.5
ROPE_BASE = 10000.0
LN_EPS = 1e-5
RMS_EPS = 1e-6
LANES = 128
SUB_ROWS = 256

U_QLAT = 0
U_LRU_X = 512
U_LRU_G = 1024
U_RET_Q = 1536
U_RET_K = 2048
U_RET_V = 2560
U_RET_G = 3072
U_MLA_G = 3584
U_KVLAT = 4608
U_KR = 4864
U_W = 4992
U_CHUNKS = 3
U_TN = U_W // U_CHUNKS

VMEM_LIMIT = 56 * 1024 * 1024


def _cparams(sem):
    return pltpu.CompilerParams(dimension_semantics=sem, vmem_limit_bytes=VMEM_LIMIT)


def _sigmoid(x):
    return 0.5 * jnp.tanh(0.5 * x) + 0.5


def _silu(x):
    return x * _sigmoid(x)


def _softplus(x):
    return jnp.maximum(x, 0.0) + jnp.log1p(jnp.exp(-jnp.abs(x)))


def _pair_swap(x, half):
    n = x.shape[-1]
    lane = lax.broadcasted_iota(jnp.int32, x.shape, x.ndim - 1)
    up = pltpu.roll(x, n - half, axis=x.ndim - 1)
    down = pltpu.roll(x, half, axis=x.ndim - 1)
    return jnp.where(lane % (2 * half) < half, up, down)


def _rope(x, cos, sin, half):
    return x * cos + _pair_swap(x, half) * sin


def _ada_kernel(cond_ref, w_ref, b_ref, o_ref):
    c = cond_ref[...]
    a = _silu(c).astype(BF16)
    o_ref[...] = jnp.dot(a, w_ref[...].astype(BF16), preferred_element_type=F32) + b_ref[...]


def _ada_call(cond, w_ada, b_ada):
    L, D, D3 = w_ada.shape
    tn = 1024
    return pl.pallas_call(
        _ada_kernel,
        out_shape=jax.ShapeDtypeStruct((L, 8, D3), F32),
        grid=(L, D3 // tn),
        in_specs=[
            pl.BlockSpec((8, D), lambda l, n: (0, 0)),
            pl.BlockSpec((None, D, tn), lambda l, n: (l, 0, n)),
            pl.BlockSpec((None, 1, tn), lambda l, n: (l, 0, n)),
        ],
        out_specs=pl.BlockSpec((None, 8, tn), lambda l, n: (l, 0, n)),
        compiler_params=_cparams(("parallel", "parallel")),
        name="ada_mod",
    )(cond, w_ada, b_ada.reshape(L, 1, D3))


def _ln0_kernel(c_ref, x_ref, o_ref, *, n_ctx_blocks):
    def norm(ref):
        x = ref[...]
        mu = jnp.mean(x, axis=-1, keepdims=True)
        xc = x - mu
        var = jnp.mean(xc * xc, axis=-1, keepdims=True)
        o_ref[...] = xc * lax.rsqrt(var + LN_EPS)

    j = pl.program_id(1)
    pl.when(j < n_ctx_blocks)(lambda: norm(c_ref))
    pl.when(j >= n_ctx_blocks)(lambda: norm(x_ref))


def _ln0_call(ctx, x):
    B, T, D = x.shape
    n_ctx = ctx.shape[1]
    tm = 256
    nc = n_ctx // tm
    return pl.pallas_call(
        functools.partial(_ln0_kernel, n_ctx_blocks=nc),
        out_shape=jax.ShapeDtypeStruct((B, n_ctx + T, D), F32),
        grid=(B, (n_ctx + T) // tm),
        in_specs=[pl.BlockSpec((None, tm, D), lambda b, j: (b, jnp.minimum(j, nc - 1), 0)),
                  pl.BlockSpec((None, tm, D), lambda b, j: (b, jnp.maximum(j - nc, 0), 0))],
        out_specs=pl.BlockSpec((None, tm, D), lambda b, j: (b, j, 0)),
        compiler_params=_cparams(("parallel", "parallel")),
        name="ln_entry",
    )(ctx, x)


def _inproj_kernel(h_ref, mod_ref, modc_ref, *rest, tm, ctx):
    w_refs, (u_ref, xs_ref) = rest[:U_CHUNKS], rest[U_CHUNKS:]
    j = pl.program_id(1)
    n = pl.program_id(2)

    @pl.when(n == 0)
    def _():
        for r0 in range(0, tm, SUB_ROWS):
            rs = slice(r0, r0 + SUB_ROWS)
            rows = j * tm + r0 + lax.broadcasted_iota(jnp.int32, (SUB_ROWS, 1), 0)
            is_ctx = rows < ctx
            sh = jnp.where(is_ctx, modc_ref[0:1, :], mod_ref[0:1, :])
            sc = jnp.where(is_ctx, modc_ref[1:2, :], mod_ref[1:2, :])
            xs = (h_ref[rs, :] * (1.0 + sc) + sh).astype(BF16)
            xs_ref[rs, :] = xs
            u_ref[rs, :] = jnp.dot(xs, w_refs[0][...], preferred_element_type=F32).astype(u_ref.dtype)

    for c in range(1, U_CHUNKS):
        @pl.when(n == c)
        def _():
            u_ref[...] = jnp.dot(xs_ref[...], w_refs[c][...],
                                 preferred_element_type=F32).astype(u_ref.dtype)


def _inproj_call(h, mod_all, w_all, l, ctx, tm):
    B, S, D = h.shape
    return pl.pallas_call(
        functools.partial(_inproj_kernel, tm=tm, ctx=ctx),
        out_shape=jax.ShapeDtypeStruct((B, S, U_W), F32),
        grid=(B, S // tm, U_CHUNKS),
        in_specs=[
            pl.BlockSpec((None, tm, D), lambda b, j, n: (b, j, 0)),
            pl.BlockSpec((None, None, 8, D), lambda b, j, n: (l, b, 0, 0)),
            pl.BlockSpec((None, None, 8, D), lambda b, j, n: (l, B, 0, 0)),
        ] + [
            pl.BlockSpec((None, D, U_TN), functools.partial(lambda b, j, n, c: (l, 0, c), c=c),
                         pipeline_mode=pl.Buffered(1))
            for c in range(U_CHUNKS)
        ],
        out_specs=pl.BlockSpec((None, tm, U_TN), lambda b, j, n: (b, j, n)),
        scratch_shapes=[pltpu.VMEM((tm, D), BF16)],
        compiler_params=_cparams(("parallel", "parallel", "arbitrary")),
        name="in_proj",
    )(h, mod_all, mod_all, *([w_all] * U_CHUNKS))


def _rms(x, g):
    return x * lax.rsqrt(jnp.mean(x * x, axis=-1, keepdims=True) + RMS_EPS) * g


def _mla_prep_kernel(ql_ref, kvl_ref, kr_ref, gq_ref, gkv_ref, wuq_ref, wk_ref, wvt_ref,
                     cos_ref, sin_ref, q_ref, k_ref, vt_ref):
    cos = cos_ref[...]
    sin = sin_ref[...]
    half = MLA_ROPE // 4

    zq = _rms(ql_ref[...].astype(F32), gq_ref[...]).astype(BF16)
    q = jnp.dot(zq, wuq_ref[...], preferred_element_type=F32)
    for hd in range(MLA_HEADS):
        c0 = hd * MLA_HP
        q_ref[:, c0:c0 + MLA_NOPE] = (q[:, c0:c0 + MLA_NOPE] * MLA_SCALE).astype(q_ref.dtype)
        qr = _rope(q[:, c0 + MLA_NOPE:c0 + MLA_HP], cos, sin, half)
        q_ref[:, c0 + MLA_NOPE:c0 + MLA_HP] = (qr * MLA_SCALE).astype(q_ref.dtype)

    zk = _rms(kvl_ref[...].astype(F32), gkv_ref[...]).astype(BF16)
    kn = jnp.dot(zk, wk_ref[...], preferred_element_type=F32)
    kr = _rope(kr_ref[...].astype(F32), cos, sin, half).astype(k_ref.dtype)
    for hd in range(MLA_HEADS):
        c0 = hd * MLA_HP
        k_ref[:, c0:c0 + MLA_NOPE] = kn[:, hd * MLA_NOPE:(hd + 1) * MLA_NOPE].astype(k_ref.dtype)
        k_ref[:, c0 + MLA_NOPE:c0 + MLA_HP] = kr
    vt = lax.dot_general(wvt_ref[...], zk, (((1,), (1,)), ((), ())), preferred_element_type=F32)
    vt_ref[...] = vt.astype(vt_ref.dtype)


def _mla_prep_call(u, gq, gkv, wuq, wk, wvt, l, cos, sin):
    B, S, _ = u.shape
    tp = 256
    HW = MLA_HEADS * MLA_HP
    return pl.pallas_call(
        _mla_prep_kernel,
        out_shape=(
            jax.ShapeDtypeStruct((B, S, HW), BF16),
            jax.ShapeDtypeStruct((B, S, HW), BF16),
            jax.ShapeDtypeStruct((B, MLA_W, S), BF16),
        ),
        grid=(B, S // tp),
        in_specs=[
            pl.BlockSpec((None, tp, MLA_Q_RANK), lambda b, j: (b, j, U_QLAT // MLA_Q_RANK)),
            pl.BlockSpec((None, tp, MLA_KV_RANK), lambda b, j: (b, j, U_KVLAT // MLA_KV_RANK)),
            pl.BlockSpec((None, tp, LANES), lambda b, j: (b, j, U_KR // LANES)),
            pl.BlockSpec((None, 1, MLA_Q_RANK), lambda b, j: (l, 0, 0)),
            pl.BlockSpec((None, 1, MLA_KV_RANK), lambda b, j: (l, 0, 0)),
            pl.BlockSpec((None, MLA_Q_RANK, HW), lambda b, j: (l, 0, 0)),
            pl.BlockSpec((None, MLA_KV_RANK, MLA_W), lambda b, j: (l, 0, 0)),
            pl.BlockSpec((None, MLA_W, MLA_KV_RANK), lambda b, j: (l, 0, 0)),
            pl.BlockSpec((tp, LANES), lambda b, j: (j, 0)),
            pl.BlockSpec((tp, LANES), lambda b, j: (j, 0)),
        ],
        out_specs=(
            pl.BlockSpec((None, tp, HW), lambda b, j: (b, j, 0)),
            pl.BlockSpec((None, tp, HW), lambda b, j: (b, j, 0)),
            pl.BlockSpec((None, MLA_W, tp), lambda b, j: (b, 0, j)),
        ),
        compiler_params=_cparams(("parallel", "parallel")),
        name="mla_prep",
    )(u, u, u, gq, gkv, wuq, wk, wvt, cos, sin)


def _attn_kernel(q_ref, k_ref, vt_ref, g_ref, y_ref, *, ctx, tq):
    S = k_ref.shape[0]
    nq = S // tq

    def scores(blk, nk):
        qb = q_ref[blk * tq:(blk + 1) * tq, :]
        return lax.dot_general(k_ref[0:nk, :], qb, (((1,), (1,)), ((), ())), preferred_element_type=F32)

    def finish(blk, st, nk):
        rows = slice(blk * tq, (blk + 1) * tq)
        p = jnp.exp(st - jnp.max(st, axis=0, keepdims=True))
        l = jnp.sum(p, axis=0, keepdims=True)
        ot = jnp.dot(vt_ref[:, 0:nk], p.astype(BF16), preferred_element_type=F32)
        o = (ot / l).T
        y_ref[rows, :] = (o * _silu(g_ref[rows, :].astype(F32))).astype(y_ref.dtype)

    st = scores(0, ctx)
    for blk in range(nq):
        st_next = scores(blk + 1, S) if blk + 1 < nq else None
        finish(blk, st, ctx if blk == 0 else S)
        st = st_next


def _attn_call(q, k, vt, u, ctx):
    B, S, _ = q.shape
    tq = 256
    assert ctx == tq
    gcol = U_MLA_G // MLA_V
    return pl.pallas_call(
        functools.partial(_attn_kernel, ctx=ctx, tq=tq),
        out_shape=jax.ShapeDtypeStruct((B, S, MLA_W), BF16),
        grid=(B, MLA_HEADS),
        in_specs=[
            pl.BlockSpec((None, S, MLA_HP), lambda b, h: (b, 0, h)),
            pl.BlockSpec((None, S, MLA_HP), lambda b, h: (b, 0, h)),
            pl.BlockSpec((None, MLA_V, S), lambda b, h: (b, h, 0)),
            pl.BlockSpec((None, S, MLA_V), lambda b, h: (b, 0, gcol + h)),
        ],
        out_specs=pl.BlockSpec((None, S, MLA_V), lambda b, h: (b, 0, h)),
        compiler_params=_cparams(("parallel", "parallel")),
        name="mla_attn",
    )(q, k, vt, u)


def _lru_kernel(x_ref, g_ref, cw_ref, cb_ref, wg_ref, bg_ref, lam_ref, y_ref,
                af_ref, bf_ref, ab_ref, bb_ref, hf_ref, hb_ref, *, ctx):
    S, CB = x_ref.shape
    x = x_ref[...].astype(F32)
    rows = lax.broadcasted_iota(jnp.int32, (S, 1), 0)
    in_ctx = rows < ctx

    xc = x * cw_ref[2:3, :] + cb_ref[...]
    for tap, off in ((0, -2), (1, -1), (3, 1)):
        xs = pltpu.roll(x, (-off) % S, axis=0)
        src = rows + off
        valid = (src >= 0) & (src < S) & ((src < ctx) == in_ctx)
        xc = xc + jnp.where(valid, xs, 0.0) * cw_ref[tap:tap + 1, :]

    xb = xc.astype(BF16)
    for d, (a_ref, b_ref) in enumerate(((af_ref, bf_ref), (ab_ref, bb_ref))):
        pre = jnp.dot(xb, wg_ref[d], preferred_element_type=F32) + bg_ref[d]
        r = _sigmoid(pre[:, :CB])
        i = _sigmoid(pre[:, CB:])
        a = jnp.exp(r * (-LRU_C * _softplus(-lam_ref[d:d + 1, :])))
        w = (1.0 - a) * (1.0 + a)
        a_ref[...] = a
        b_ref[...] = jnp.where(w > 0.0, w * lax.rsqrt(w), 0.0) * (i * xc)

    R = 8
    n_t, n_ctx_t = S // R, ctx // R
    row = lax.broadcasted_iota(jnp.int32, (R, CB), 0)

    def tile_prefix(a, b, rev):
        for d in (1, 2, 4):
            shift = R - d if rev else d
            keep = (row < R - d) if rev else (row >= d)
            a_s = pltpu.roll(a, shift, axis=0)
            b_s = pltpu.roll(b, shift, axis=0)
            b = jnp.where(keep, a * b_s, 0.0) + b
            a = jnp.where(keep, a * a_s, a)
        return a, b

    def tile_step(i, carry):
        hf, hb = carry
        sf = pl.ds(pl.multiple_of(i * R, R), R)
        ib = jnp.where(i < n_ctx_t, n_ctx_t - 1 - i, n_t - 1 - (i - n_ctx_t))
        sb = pl.ds(pl.multiple_of(ib * R, R), R)
        pa, pb = tile_prefix(af_ref[sf, :], bf_ref[sf, :], False)
        of = pa * hf + pb
        qa, qb = tile_prefix(ab_ref[sb, :], bb_ref[sb, :], True)
        ob = qa * hb + qb
        hf_ref[sf, :] = of
        hb_ref[sb, :] = ob
        return (jnp.broadcast_to(of[R - 1:R, :], (R, CB)), jnp.broadcast_to(ob[0:1, :], (R, CB)))

    zero = jnp.zeros((R, CB), F32)
    lax.fori_loop(0, n_t, tile_step, (zero, zero), unroll=4)

    y = (hf_ref[...] + hb_ref[...]) * _silu(g_ref[...].astype(F32))
    y_ref[...] = y.astype(y_ref.dtype)


def _lru_call(u, cw, cb, wg, bg, lam, l, ctx):
    B, S, _ = u.shape
    ncb = LRU_W // LRU_CB
    return pl.pallas_call(
        functools.partial(_lru_kernel, ctx=ctx),
        out_shape=jax.ShapeDtypeStruct((B, S, LRU_W), BF16),
        grid=(B, ncb),
        in_specs=[
            pl.BlockSpec((None, S, LRU_CB), lambda b, c: (b, 0, U_LRU_X // LRU_CB + c)),
            pl.BlockSpec((None, S, LRU_CB), lambda b, c: (b, 0, U_LRU_G // LRU_CB + c)),
            pl.BlockSpec((None, LRU_CONV, LRU_CB), lambda b, c: (l, 0, c)),
            pl.BlockSpec((None, 1, LRU_CB), lambda b, c: (l, 0, c)),
            pl.BlockSpec((None, 2, None, LRU_CB, 2 * LRU_CB), lambda b, c: (l, 0, c, 0, 0)),
            pl.BlockSpec((None, 2, None, 1, 2 * LRU_CB), lambda b, c: (l, 0, c, 0, 0)),
            pl.BlockSpec((None, 2, LRU_CB), lambda b, c: (l, 0, c)),
        ],
        out_specs=pl.BlockSpec((None, S, LRU_CB), lambda b, c: (b, 0, c)),
        scratch_shapes=[pltpu.VMEM((S, LRU_CB), F32) for _ in range(6)],
        compiler_params=_cparams(("parallel", "parallel")),
        name="rg_lru",
    )(u, u, cw, cb, wg, bg, lam)


def _ret_kernel(lg_ref, q_ref, k_ref, v_ref, g_ref, cos_ref, sin_ref, y_ref, ks_ref, kv_ref, st_ref,
                *, n_ctx, layer):
    C = RET_CHUNK
    S = q_ref.shape[0]
    n_all = S // C
    unroll = 3 if n_all % 3 == 0 else (2 if n_all % 2 == 0 else 1)
    hd = pl.program_id(1)
    lgf = lg_ref[2 * layer, hd]
    lgb = lg_ref[2 * layer + 1, hd]

    ri = lax.broadcasted_iota(jnp.int32, (C, C), 0).astype(F32)
    ci = lax.broadcasted_iota(jnp.int32, (C, C), 1).astype(F32)
    diff = ri - ci
    decay = jnp.where(diff >= 0, jnp.exp(jnp.maximum(diff, 0.0) * lgf),
                      jnp.exp(jnp.maximum(-diff, 0.0) * lgb))
    zeta_f = jnp.exp((C - 1 - ri) * lgf)
    xi_f = jnp.exp((ri + 1) * lgf)
    gc_f = jnp.exp(jnp.full((C, C), C, F32) * lgf)
    zeta_b = jnp.exp(ri * lgb)
    xi_b = jnp.exp((C - ri) * lgb)
    gc_b = jnp.exp(jnp.full((C, C), C, F32) * lgb)
    half = RET_DH // 4

    def rows_of(c):
        return pl.ds(pl.multiple_of(c * C, C), C)

    def chunk_kv(c, carry):
        sl = rows_of(c)
        k = _rope(k_ref[sl, :].astype(F32), cos_ref[sl, :], sin_ref[sl, :], half) * RET_K_SCALE
        ks_ref[sl, :] = k.astype(BF16)
        kz = jnp.concatenate([k * zeta_f, k * zeta_b], axis=1).astype(BF16)
        kv_ref[c] = lax.dot_general(kz, v_ref[sl, :].astype(BF16), (((0,), (0,)), ((), ())),
                                    preferred_element_type=F32)
        return carry

    lax.fori_loop(0, n_all, chunk_kv, 0, unroll=unroll)

    def fwd_state(c, r):
        st_ref[c, 0:C, :] = r.astype(BF16)
        return gc_f * r + kv_ref[c, 0:C, :]

    def bwd_state(i, r):
        c = jnp.where(i < n_ctx, n_ctx - 1 - i, n_all - 1 - (i - n_ctx))
        st_ref[c, C:2 * C, :] = r.astype(BF16)
        return gc_b * r + kv_ref[c, C:2 * C, :]

    lax.fori_loop(0, n_all, fwd_state, jnp.zeros((C, C), F32))
    lax.fori_loop(0, n_all, bwd_state, jnp.zeros((C, C), F32))

    def chunk_out(c, carry):
        sl = rows_of(c)
        q = _rope(q_ref[sl, :].astype(F32), cos_ref[sl, :], sin_ref[sl, :], half)
        s = lax.dot_general(q.astype(BF16), ks_ref[sl, :], (((1,), (1,)), ((), ())),
                            preferred_element_type=F32)
        o = jnp.dot((s * decay).astype(BF16), v_ref[sl, :].astype(BF16), preferred_element_type=F32)
        qx = jnp.concatenate([q * xi_f, q * xi_b], axis=1).astype(BF16)
        o = o + jnp.dot(qx, st_ref[c], preferred_element_type=F32)
        mu = jnp.mean(o, axis=-1, keepdims=True)
        oc = o - mu
        var = jnp.mean(oc * oc, axis=-1, keepdims=True)
        y = oc * lax.rsqrt(var + LN_EPS) * _silu(g_ref[sl, :].astype(F32))
        y_ref[sl, :] = y.astype(y_ref.dtype)
        return carry

    lax.fori_loop(0, n_all, chunk_out, 0, unroll=unroll)


def _ret_call(u, log_g, l, cos, sin, ctx):
    B, S, _ = u.shape
    n_all = S // RET_CHUNK

    def col(base):
        return lambda b, h: (b, 0, base // RET_DH + h)

    return pl.pallas_call(
        functools.partial(_ret_kernel, n_ctx=ctx // RET_CHUNK, layer=l),
        out_shape=jax.ShapeDtypeStruct((B, S, RET_W), BF16),
        grid=(B, RET_HEADS),
        in_specs=[
            pl.BlockSpec(memory_space=pltpu.SMEM),
            pl.BlockSpec((None, S, RET_DH), col(U_RET_Q)),
            pl.BlockSpec((None, S, RET_DH), col(U_RET_K)),
            pl.BlockSpec((None, S, RET_DH), col(U_RET_V)),
            pl.BlockSpec((None, S, RET_DH), col(U_RET_G)),
            pl.BlockSpec((S, RET_DH), lambda b, h: (0, 0)),
            pl.BlockSpec((S, RET_DH), lambda b, h: (0, 0)),
        ],
        out_specs=pl.BlockSpec((None, S, RET_DH), lambda b, h: (b, 0, h)),
        scratch_shapes=[pltpu.VMEM((S, RET_DH), BF16),
                        pltpu.VMEM((n_all, 2 * RET_CHUNK, RET_DH), F32),
                        pltpu.VMEM((n_all, 2 * RET_CHUNK, RET_DH), BF16)],
        compiler_params=_cparams(("parallel", "parallel")),
        name="retention",
    )(log_g, u, u, u, u, cos, sin)


def _outproj_kernel(ym_ref, yl_ref, yr_ref, h_ref, mod_ref, modc_ref, w_ref, lng_ref, lnb_ref, o_ref,
                    *, tm, ctx, alpha, skip):
    j = pl.program_id(1) + skip
    for r0 in range(0, tm, SUB_ROWS):
        rs = slice(r0, r0 + SUB_ROWS)
        rows = j * tm + r0 + lax.broadcasted_iota(jnp.int32, (SUB_ROWS, 1), 0)
        gt = jnp.where(rows < ctx, modc_ref[2:3, :], mod_ref[2:3, :])
        y = jnp.concatenate([ym_ref[rs, :], yl_ref[rs, :], yr_ref[rs, :]], axis=1)
        acc = jnp.dot(y, w_ref[...], preferred_element_type=F32)
        z = alpha * h_ref[rs, :] + gt * acc
        mu = jnp.mean(z, axis=-1, keepdims=True)
        zc = z - mu
        var = jnp.mean(zc * zc, axis=-1, keepdims=True)
        o_ref[rs, :] = zc * lax.rsqrt(var + LN_EPS) * lng_ref[...] + lnb_ref[...]


def _outproj_call(ym, yl, yr, h, mod, w, lng, lnb, l, ctx, tm, alpha, latent_only):
    B, S, D = h.shape
    skip = ctx // tm if latent_only else 0
    assert skip * tm == (ctx if latent_only else 0)
    n_out = S // tm - skip
    return pl.pallas_call(
        functools.partial(_outproj_kernel, tm=tm, ctx=ctx, alpha=alpha, skip=skip),
        out_shape=jax.ShapeDtypeStruct((B, n_out * tm, D), F32),
        grid=(B, n_out),
        in_specs=[
            pl.BlockSpec((None, tm, MLA_W), lambda b, j: (b, j + skip, 0)),
            pl.BlockSpec((None, tm, LRU_W), lambda b, j: (b, j + skip, 0)),
            pl.BlockSpec((None, tm, RET_W), lambda b, j: (b, j + skip, 0)),
            pl.BlockSpec((None, tm, D), lambda b, j: (b, j + skip, 0)),
            pl.BlockSpec((None, None, 8, D), lambda b, j: (l, b, 0, 0)),
            pl.BlockSpec((None, None, 8, D), lambda b, j: (l, B, 0, 0)),
            pl.BlockSpec((None, D, D), lambda b, j: (l, 0, 0), pipeline_mode=pl.Buffered(1)),
            pl.BlockSpec((None, 1, D), lambda b, j: (l, 0, 0)),
            pl.BlockSpec((None, 1, D), lambda b, j: (l, 0, 0)),
        ],
        out_specs=pl.BlockSpec((None, tm, D), lambda b, j: (b, j, 0)),
        compiler_params=_cparams(("parallel", "parallel")),
        name="out_proj_ln",
    )(ym, yl, yr, h, mod, mod, w, lng, lnb)


def _rope_tables(rows, dim, ctx):
    row = jnp.repeat(jnp.arange(rows, dtype=F32), GRID_W)
    col = jnp.tile(jnp.arange(GRID_W, dtype=F32), rows)
    quarter = dim // 4
    inv = ROPE_BASE ** (-jnp.arange(quarter, dtype=F32) / quarter)
    ang = jnp.stack([row[:, None] * inv, col[:, None] * inv], axis=1)
    cos, sin = jnp.cos(ang), jnp.sin(ang)
    cos_l = jnp.concatenate([cos[:, 0], cos[:, 0], cos[:, 1], cos[:, 1]], axis=-1)
    sin_l = jnp.concatenate([-sin[:, 0], sin[:, 0], -sin[:, 1], sin[:, 1]], axis=-1)
    pad = LANES - dim
    cos_l = jnp.pad(cos_l, ((0, 0), (0, pad)))
    sin_l = jnp.pad(sin_l, ((0, 0), (0, pad)))
    cos_c = jnp.pad(jnp.ones((ctx, dim), F32), ((0, 0), (0, pad)))
    sin_c = jnp.zeros((ctx, LANES), F32)
    return jnp.concatenate([cos_c, cos_l], axis=0), jnp.concatenate([sin_c, sin_l], axis=0)


def _layout_w_in(w_in):
    L, D, _ = w_in.shape
    o = 0
    parts = {}
    for name, width in (("qlat", 512), ("kvlat", 256), ("kr", 64), ("mla_g", 1024), ("lru_x", 512),
                        ("lru_g", 512), ("ret_q", 512), ("ret_k", 512), ("ret_v", 512), ("ret_g", 512)):
        parts[name] = w_in[:, :, o:o + width]
        o += width
    return jnp.concatenate([parts["qlat"], parts["lru_x"], parts["lru_g"], parts["ret_q"], parts["ret_k"],
                            parts["ret_v"], parts["ret_g"], parts["mla_g"], parts["kvlat"], parts["kr"],
                            jnp.zeros((L, D, LANES - MLA_ROPE), w_in.dtype)], axis=2).astype(BF16)


def _layout_w_uq(w_uq):
    L = w_uq.shape[0]
    w = w_uq.reshape(L, MLA_Q_RANK, MLA_HEADS, MLA_NOPE + MLA_ROPE)
    w = jnp.pad(w, ((0, 0), (0, 0), (0, 0), (0, MLA_HP - MLA_NOPE - MLA_ROPE)))
    return w.reshape(L, MLA_Q_RANK, MLA_HEADS * MLA_HP).astype(BF16)


def _layout_w_ukv(w_ukv):
    L = w_ukv.shape[0]
    w = w_ukv.reshape(L, MLA_KV_RANK, MLA_HEADS, MLA_NOPE + MLA_V)
    wk = w[..., :MLA_NOPE].reshape(L, MLA_KV_RANK, MLA_W).astype(BF16)
    wvt = w[..., MLA_NOPE:].reshape(L, MLA_KV_RANK, MLA_W).transpose(0, 2, 1).astype(BF16)
    return wk, wvt


def _layout_lru_gates(w_r, b_r, w_i, b_i):
    L = w_r.shape[0]
    eye = jnp.eye(LRU_BLOCKS, dtype=w_r.dtype)
    ncb = LRU_W // LRU_CB

    def dense(w):
        return jnp.einsum("ldgij,gh->ldgihj", w, eye).reshape(L, 2, LRU_W, LRU_W)

    def diag_blocks(w):
        w = w.reshape(L, 2, ncb, LRU_CB, ncb, LRU_CB)
        return jnp.stack([w[:, :, c, :, c, :] for c in range(ncb)], axis=2)

    wg = jnp.concatenate([diag_blocks(dense(w_r)), diag_blocks(dense(w_i))], axis=-1).astype(BF16)
    bg = jnp.concatenate([b_r.reshape(L, 2, ncb, 1, LRU_CB), b_i.reshape(L, 2, ncb, 1, LRU_CB)], axis=-1)
    return wg, bg


def kernel(x, c, ctx, c_ctx, w_ada, b_ada, w_in, mla_q_norm_g, mla_kv_norm_g, mla_w_uq, mla_w_ukv,
           lru_conv_w, lru_conv_b, lru_w_r, lru_b_r, lru_w_i, lru_b_i, lru_lambda, ret_decay,
           w_out, ln_g, ln_b):
    B, T, D = x.shape
    L = w_in.shape[0]
    n_ctx = ctx.shape[1]
    S = n_ctx + T
    assert D == 2 * MLA_W and w_in.shape[2] == 4928 and n_ctx % 256 == 0 and T % 256 == 0
    alpha = (2 * L) ** 0.25
    tm = 768 if S % 768 == 0 else 256

    cos_m, sin_m = _rope_tables(T // GRID_W, MLA_ROPE, n_ctx)
    cos_r, sin_r = _rope_tables(T // GRID_W, RET_DH, n_ctx)

    cond = jnp.zeros((8, D), F32).at[:B].set(c).at[B].set(c_ctx)
    mod_all = _ada_call(cond, w_ada, b_ada)
    mod_all = mod_all[:, :B + 1].reshape(L, B + 1, 3, D)
    mod_all = jnp.pad(mod_all, ((0, 0), (0, 0), (0, 5), (0, 0)))

    h = _ln0_call(ctx, x)
    log_g = jax.nn.log_sigmoid(ret_decay.astype(F32)).reshape(2 * L, RET_HEADS)

    w_in_l = _layout_w_in(w_in)
    wuq = _layout_w_uq(mla_w_uq)
    wk, wvt = _layout_w_ukv(mla_w_ukv)
    wg, bg = _layout_lru_gates(lru_w_r, lru_b_r, lru_w_i, lru_b_i)
    w_out_b = w_out.astype(BF16)
    gq, gkv = mla_q_norm_g[:, None, :], mla_kv_norm_g[:, None, :]
    conv_b, lng, lnb = lru_conv_b[:, None, :], ln_g[:, None, :], ln_b[:, None, :]

    for l in range(L):
        last = l == L - 1
        u = _inproj_call(h, mod_all, w_in_l, l, n_ctx, tm)
        q, k, vt = _mla_prep_call(u, gq, gkv, wuq, wk, wvt, l, cos_m, sin_m)
        y_mla = _attn_call(q, k, vt, u, n_ctx)
        y_lru = _lru_call(u, lru_conv_w, conv_b, wg, bg, lru_lambda, l, n_ctx)
        y_ret = _ret_call(u, log_g, l, cos_r, sin_r, n_ctx)
        h = _outproj_call(y_mla, y_lru, y_ret, h, mod_all, w_out_b, lng, lnb, l, n_ctx,
                          256 if last else tm, alpha, last)
    return h
```

```python
import functools

import jax
import jax.numpy as jnp
from jax import lax
from jax.experimental import pallas as pl
from jax.experimental.pallas import tpu as pltpu

F32 = jnp.float32
BF16 = jnp.bfloat16

GRID_W = 64
MLA_V = 128
MLA_NOPE = 128
MLA_ROPE = 64
MLA_HEADS = 8
MLA_W = MLA_HEADS * MLA_V
MLA_Q_RANK = 512
MLA_KV_RANK = 256
MLA_SCALE = (MLA_NOPE + MLA_ROPE) ** -0.5
LOG2_E = 1.4426950408889634
MLA_Q_SCALE = MLA_SCALE * LOG2_E
MLA_HP = 256
LRU_W = 512
LRU_BLOCKS = 8
LRU_BW = LRU_W // LRU_BLOCKS
LRU_CONV = 4
LRU_C = 8.0
LRU_CB = 256
RET_HEADS = 4
RET_DH = 128
RET_W = RET_HEADS * RET_DH
RET_CHUNK = 128
RET_K_SCALE = RET_DH ** -0.5
ROPE_BASE = 10000.0
LN_EPS = 1e-5
RMS_EPS = 1e-6
LANES = 128
BF16_ROWS = 16
KEY_CHUNK = 768
ATTN_BUFS = 3
SUB_ROWS = 256

U_QLAT = 0
U_LRU_X = 512
U_LRU_G = 1024
U_RET_Q = 1536
U_RET_K = 2048
U_RET_V = 2560
U_RET_G = 3072
U_MLA_G = 3584
U_KVLAT = 4608
U_KR = 4864
U_W = 4992
U_CHUNKS = 3
U_TN = U_W // U_CHUNKS

VMEM_LIMIT = 56 * 1024 * 1024


def _cparams(sem):
    return pltpu.CompilerParams(dimension_semantics=sem, vmem_limit_bytes=VMEM_LIMIT)


def _sigmoid(x):
    return 0.5 * jnp.tanh(0.5 * x) + 0.5


def _silu(x):
    return x * _sigmoid(x)


def _softplus(x):
    return jnp.maximum(x, 0.0) + jnp.log1p(jnp.exp(-jnp.abs(x)))


def _pair_swap(x, half):
    n = x.shape[-1]
    lane = lax.broadcasted_iota(jnp.int32, x.shape, x.ndim - 1)
    up = pltpu.roll(x, n - half, axis=x.ndim - 1)
    down = pltpu.roll(x, half, axis=x.ndim - 1)
    return jnp.where(lane % (2 * half) < half, up, down)


def _rope(x, cos, sin, half):
    return x * cos + _pair_swap(x, half) * sin


def _ada_kernel(cond_ref, w_ref, b_ref, o_ref):
    c = cond_ref[...]
    a = _silu(c).astype(BF16)
    o_ref[...] = jnp.dot(a, w_ref[...].astype(BF16), preferred_element_type=F32) + b_ref[...]


def _ada_call(cond, w_ada, b_ada):
    L, D, D3 = w_ada.shape
    tn = 1024
    return pl.pallas_call(
        _ada_kernel,
        out_shape=jax.ShapeDtypeStruct((L, 8, D3), F32),
        grid=(L, D3 // tn),
        in_specs=[
            pl.BlockSpec((8, D), lambda l, n: (0, 0)),
            pl.BlockSpec((None, D, tn), lambda l, n: (l, 0, n)),
            pl.BlockSpec((None, 1, tn), lambda l, n: (l, 0, n)),
        ],
        out_specs=pl.BlockSpec((None, 8, tn), lambda l, n: (l, 0, n)),
        compiler_params=_cparams(("parallel", "parallel")),
        name="ada_mod",
    )(cond, w_ada, b_ada.reshape(L, 1, D3))


def _ln0_kernel(c_ref, x_ref, o_ref, *, n_ctx_blocks):
    def norm(ref):
        x = ref[...]
        mu = jnp.mean(x, axis=-1, keepdims=True)
        xc = x - mu
        var = jnp.mean(xc * xc, axis=-1, keepdims=True)
        o_ref[...] = xc * lax.rsqrt(var + LN_EPS)

    j = pl.program_id(1)
    pl.when(j < n_ctx_blocks)(lambda: norm(c_ref))
    pl.when(j >= n_ctx_blocks)(lambda: norm(x_ref))


def _ln0_call(ctx, x):
    B, T, D = x.shape
    n_ctx = ctx.shape[1]
    tm = 256
    nc = n_ctx // tm
    return pl.pallas_call(
        functools.partial(_ln0_kernel, n_ctx_blocks=nc),
        out_shape=jax.ShapeDtypeStruct((B, n_ctx + T, D), F32),
        grid=(B, (n_ctx + T) // tm),
        in_specs=[pl.BlockSpec((None, tm, D), lambda b, j: (b, jnp.minimum(j, nc - 1), 0)),
                  pl.BlockSpec((None, tm, D), lambda b, j: (b, jnp.maximum(j - nc, 0), 0))],
        out_specs=pl.BlockSpec((None, tm, D), lambda b, j: (b, j, 0)),
        compiler_params=_cparams(("parallel", "parallel")),
        name="ln_entry",
    )(ctx, x)


def _inproj_kernel(h_ref, mod_ref, modc_ref, *rest, tm, ctx):
    w_refs, (u_ref, xs_ref) = rest[:U_CHUNKS], rest[U_CHUNKS:]
    j = pl.program_id(1)
    n = pl.program_id(2)

    @pl.when(n == 0)
    def _():
        for r0 in range(0, tm, SUB_ROWS):
            rs = slice(r0, r0 + SUB_ROWS)
            rows = j * tm + r0 + lax.broadcasted_iota(jnp.int32, (SUB_ROWS, 1), 0)
            is_ctx = rows < ctx
            sh = jnp.where(is_ctx, modc_ref[0:1, :], mod_ref[0:1, :])
            sc = jnp.where(is_ctx, modc_ref[1:2, :], mod_ref[1:2, :])
            xs = (h_ref[rs, :] * (1.0 + sc) + sh).astype(BF16)
            xs_ref[rs, :] = xs
            u_ref[rs, :] = jnp.dot(xs, w_refs[0][...], preferred_element_type=F32).astype(u_ref.dtype)

    for c in range(1, U_CHUNKS):
        @pl.when(n == c)
        def _():
            u_ref[...] = jnp.dot(xs_ref[...], w_refs[c][...],
                                 preferred_element_type=F32).astype(u_ref.dtype)


def _inproj_call(h, mod_all, w_all, l, ctx, tm):
    B, S, D = h.shape
    return pl.pallas_call(
        functools.partial(_inproj_kernel, tm=tm, ctx=ctx),
        out_shape=jax.ShapeDtypeStruct((B, S, U_W), F32),
        grid=(B, S // tm, U_CHUNKS),
        in_specs=[
            pl.BlockSpec((None, tm, D), lambda b, j, n: (b, j, 0)),
            pl.BlockSpec((None, None, 8, D), lambda b, j, n: (l, b, 0, 0)),
            pl.BlockSpec((None, None, 8, D), lambda b, j, n: (l, B, 0, 0)),
        ] + [
            pl.BlockSpec((None, D, U_TN), functools.partial(lambda b, j, n, c: (l, 0, c), c=c),
                         pipeline_mode=pl.Buffered(1))
            for c in range(U_CHUNKS)
        ],
        out_specs=pl.BlockSpec((None, tm, U_TN), lambda b, j, n: (b, j, n)),
        scratch_shapes=[pltpu.VMEM((tm, D), BF16)],
        compiler_params=_cparams(("parallel", "parallel", "arbitrary")),
        name="in_proj",
    )(h, mod_all, mod_all, *([w_all] * U_CHUNKS))


def _rms(x, g):
    return x * lax.rsqrt(jnp.mean(x * x, axis=-1, keepdims=True) + RMS_EPS) * g


def _mla_prep_kernel(ql_ref, kvl_ref, kr_ref, gq_ref, gkv_ref, wuq_ref, wk_ref, wvt_ref,
                     cos_ref, sin_ref, q_ref, k_ref, vt_ref):
    cos = cos_ref[...]
    sin = sin_ref[...]
    half = MLA_ROPE // 4

    zq = _rms(ql_ref[...].astype(F32), gq_ref[...]).astype(BF16)
    q = jnp.dot(zq, wuq_ref[...], preferred_element_type=F32)
    for hd in range(MLA_HEADS):
        c0 = hd * MLA_HP
        q_ref[:, c0:c0 + MLA_NOPE] = (q[:, c0:c0 + MLA_NOPE] * MLA_Q_SCALE).astype(q_ref.dtype)
        qr = _rope(q[:, c0 + MLA_NOPE:c0 + MLA_HP], cos, sin, half)
        q_ref[:, c0 + MLA_NOPE:c0 + MLA_HP] = (qr * MLA_Q_SCALE).astype(q_ref.dtype)

    zk = _rms(kvl_ref[...].astype(F32), gkv_ref[...]).astype(BF16)
    kn = jnp.dot(zk, wk_ref[...], preferred_element_type=F32)
    kr = _rope(kr_ref[...].astype(F32), cos, sin, half).astype(k_ref.dtype)
    for hd in range(MLA_HEADS):
        c0 = hd * MLA_HP
        k_ref[:, c0:c0 + MLA_NOPE] = kn[:, hd * MLA_NOPE:(hd + 1) * MLA_NOPE].astype(k_ref.dtype)
        k_ref[:, c0 + MLA_NOPE:c0 + MLA_HP] = kr
    vt = lax.dot_general(wvt_ref[...], zk, (((1,), (1,)), ((), ())), preferred_element_type=F32)
    vt_ref[...] = vt.astype(vt_ref.dtype)


def _mla_prep_call(u, gq, gkv, wuq, wk, wvt, l, cos, sin):
    B, S, _ = u.shape
    tp = 256
    HW = MLA_HEADS * MLA_HP
    return pl.pallas_call(
        _mla_prep_kernel,
        out_shape=(
            jax.ShapeDtypeStruct((B, S, HW), BF16),
            jax.ShapeDtypeStruct((B, S, HW), BF16),
            jax.ShapeDtypeStruct((B, MLA_W, S), BF16),
        ),
        grid=(B, S // tp),
        in_specs=[
            pl.BlockSpec((None, tp, MLA_Q_RANK), lambda b, j: (b, j, U_QLAT // MLA_Q_RANK)),
            pl.BlockSpec((None, tp, MLA_KV_RANK), lambda b, j: (b, j, U_KVLAT // MLA_KV_RANK)),
            pl.BlockSpec((None, tp, LANES), lambda b, j: (b, j, U_KR // LANES)),
            pl.BlockSpec((None, 1, MLA_Q_RANK), lambda b, j: (l, 0, 0)),
            pl.BlockSpec((None, 1, MLA_KV_RANK), lambda b, j: (l, 0, 0)),
            pl.BlockSpec((None, MLA_Q_RANK, HW), lambda b, j: (l, 0, 0)),
            pl.BlockSpec((None, MLA_KV_RANK, MLA_W), lambda b, j: (l, 0, 0)),
            pl.BlockSpec((None, MLA_W, MLA_KV_RANK), lambda b, j: (l, 0, 0)),
            pl.BlockSpec((tp, LANES), lambda b, j: (j, 0)),
            pl.BlockSpec((tp, LANES), lambda b, j: (j, 0)),
        ],
        out_specs=(
            pl.BlockSpec((None, tp, HW), lambda b, j: (b, j, 0)),
            pl.BlockSpec((None, tp, HW), lambda b, j: (b, j, 0)),
            pl.BlockSpec((None, MLA_W, tp), lambda b, j: (b, 0, j)),
        ),
        compiler_params=_cparams(("parallel", "parallel")),
        name="mla_prep",
    )(u, u, u, gq, gkv, wuq, wk, wvt, cos, sin)


def _attn_kernel(q_ref, k_ref, vt_ref, g_ref, y_ref, st_ref, m_ref, p_ref, vt1_ref, *, ctx, tq):
    S = k_ref.shape[0]
    n_lat = S // tq - 1
    assert n_lat >= 2 and S % KEY_CHUNK == 0

    vt1_ref[0:MLA_V, :] = vt_ref[...]
    vt1_ref[MLA_V:, :] = jnp.ones((BF16_ROWS, S), BF16)

    def rows_of(blk):
        return slice(blk * tq, (blk + 1) * tq)

    def scores(blk, nk):
        return lax.dot_general(k_ref[0:nk, :], q_ref[rows_of(blk), :], (((1,), (1,)), ((), ())),
                               preferred_element_type=F32)

    def numerators(st, m):
        return jnp.exp2(st - m).astype(BF16)

    def attend(blk, p, nk):
        rows = rows_of(blk)
        ot = jnp.dot(vt1_ref[:, 0:nk], p, preferred_element_type=F32)
        o = (ot[0:MLA_V, :] / ot[MLA_V:MLA_V + 1, :]).T
        y_ref[rows, :] = (o * _silu(g_ref[rows, :].astype(F32))).astype(y_ref.dtype)

    def step(score=None, num=None, att=None):
        sb, nb, ab = (None if b is None else b % ATTN_BUFS for b in (score, num, att))
        qb = None if score is None else q_ref[rows_of(score), :]
        m_num = None if num is None else m_ref[nb]
        m_run, acc = None, None
        for c0 in range(0, S, KEY_CHUNK):
            ks = slice(c0, c0 + KEY_CHUNK)
            if score is not None:
                st = lax.dot_general(k_ref[ks, :], qb, (((1,), (1,)), ((), ())),
                                     preferred_element_type=F32)
                st_ref[sb, ks, :] = st
                m_c = jnp.max(st, axis=0, keepdims=True)
                m_run = m_c if m_run is None else jnp.maximum(m_run, m_c)
            if num is not None:
                p_ref[nb, ks, :] = numerators(st_ref[nb, ks, :], m_num)
            if att is not None:
                part = jnp.dot(vt1_ref[:, ks], p_ref[ab, ks, :], preferred_element_type=F32)
                acc = part if acc is None else acc + part
        if score is not None:
            m_ref[sb] = m_run
        if att is not None:
            rows = rows_of(att)
            o = (acc[0:MLA_V, :] / acc[MLA_V:MLA_V + 1, :]).T
            y_ref[rows, :] = (o * _silu(g_ref[rows, :].astype(F32))).astype(y_ref.dtype)

    def latent(blk):
        return blk if blk <= n_lat else None

    step(score=1)
    step(score=latent(2), num=1)
    st_ctx = scores(0, ctx)
    attend(0, numerators(st_ctx, jnp.max(st_ctx, axis=0, keepdims=True)), ctx)
    for t in range(1, n_lat + 1):
        step(score=latent(t + 2), num=latent(t + 1), att=t)


def _attn_call(q, k, vt, u, ctx):
    B, S, _ = q.shape
    tq = 256
    assert ctx == tq
    gcol = U_MLA_G // MLA_V
    return pl.pallas_call(
        functools.partial(_attn_kernel, ctx=ctx, tq=tq),
        out_shape=jax.ShapeDtypeStruct((B, S, MLA_W), BF16),
        grid=(B, MLA_HEADS),
        in_specs=[
            pl.BlockSpec((None, S, MLA_HP), lambda b, h: (b, 0, h)),
            pl.BlockSpec((None, S, MLA_HP), lambda b, h: (b, 0, h)),
            pl.BlockSpec((None, MLA_V, S), lambda b, h: (b, h, 0)),
            pl.BlockSpec((None, S, MLA_V), lambda b, h: (b, 0, gcol + h)),
        ],
        out_specs=pl.BlockSpec((None, S, MLA_V), lambda b, h: (b, 0, h)),
        scratch_shapes=[pltpu.VMEM((ATTN_BUFS, S, tq), F32), pltpu.VMEM((ATTN_BUFS, 1, tq), F32),
                        pltpu.VMEM((ATTN_BUFS, S, tq), BF16),
                        pltpu.VMEM((MLA_V + BF16_ROWS, S), BF16)],
        compiler_params=_cparams(("parallel", "parallel")),
        name="mla_attn",
    )(q, k, vt, u)


def _lru_kernel(x_ref, g_ref, cw_ref, cb_ref, wg_ref, bg_ref, lam_ref, y_ref,
                af_ref, bf_ref, ab_ref, bb_ref, hf_ref, hb_ref, *, ctx):
    S, CB = x_ref.shape
    x = x_ref[...].astype(F32)
    rows = lax.broadcasted_iota(jnp.int32, (S, 1), 0)
    in_ctx = rows < ctx

    xc = x * cw_ref[2:3, :] + cb_ref[...]
    for tap, off in ((0, -2), (1, -1), (3, 1)):
        xs = pltpu.roll(x, (-off) % S, axis=0)
        src = rows + off
        valid = (src >= 0) & (src < S) & ((src < ctx) == in_ctx)
        xc = xc + jnp.where(valid, xs, 0.0) * cw_ref[tap:tap + 1, :]

    xb = xc.astype(BF16)
    for d, (a_ref, b_ref) in enumerate(((af_ref, bf_ref), (ab_ref, bb_ref))):
        pre = jnp.dot(xb, wg_ref[d], preferred_element_type=F32) + bg_ref[d]
        r = _sigmoid(pre[:, :CB])
        i = _sigmoid(pre[:, CB:])
        a = jnp.exp(r * (-LRU_C * _softplus(-lam_ref[d:d + 1, :])))
        w = (1.0 - a) * (1.0 + a)
        a_ref[...] = a
        b_ref[...] = jnp.where(w > 0.0, w * lax.rsqrt(w), 0.0) * (i * xc)

    R = 8
    n_t, n_ctx_t = S // R, ctx // R
    row = lax.broadcasted_iota(jnp.int32, (R, CB), 0)

    def tile_prefix(a, b, rev):
        for d in (1, 2, 4):
            shift = R - d if rev else d
            keep = (row < R - d) if rev else (row >= d)
            a_s = pltpu.roll(a, shift, axis=0)
            b_s = pltpu.roll(b, shift, axis=0)
            b = jnp.where(keep, a * b_s, 0.0) + b
            a = jnp.where(keep, a * a_s, a)
        return a, b

    def tile_step(i, carry):
        hf, hb = carry
        sf = pl.ds(pl.multiple_of(i * R, R), R)
        ib = jnp.where(i < n_ctx_t, n_ctx_t - 1 - i, n_t - 1 - (i - n_ctx_t))
        sb = pl.ds(pl.multiple_of(ib * R, R), R)
        pa, pb = tile_prefix(af_ref[sf, :], bf_ref[sf, :], False)
        of = pa * hf + pb
        qa, qb = tile_prefix(ab_ref[sb, :], bb_ref[sb, :], True)
        ob = qa * hb + qb
        hf_ref[sf, :] = of
        hb_ref[sb, :] = ob
        return (jnp.broadcast_to(of[R - 1:R, :], (R, CB)), jnp.broadcast_to(ob[0:1, :], (R, CB)))

    zero = jnp.zeros((R, CB), F32)
    lax.fori_loop(0, n_t, tile_step, (zero, zero), unroll=4)

    y = (hf_ref[...] + hb_ref[...]) * _silu(g_ref[...].astype(F32))
    y_ref[...] = y.astype(y_ref.dtype)


def _lru_call(u, cw, cb, wg, bg, lam, l, ctx):
    B, S, _ = u.shape
    ncb = LRU_W // LRU_CB
    return pl.pallas_call(
        functools.partial(_lru_kernel, ctx=ctx),
        out_shape=jax.ShapeDtypeStruct((B, S, LRU_W), BF16),
        grid=(B, ncb),
        in_specs=[
            pl.BlockSpec((None, S, LRU_CB), lambda b, c: (b, 0, U_LRU_X // LRU_CB + c)),
            pl.BlockSpec((None, S, LRU_CB), lambda b, c: (b, 0, U_LRU_G // LRU_CB + c)),
            pl.BlockSpec((None, LRU_CONV, LRU_CB), lambda b, c: (l, 0, c)),
            pl.BlockSpec((None, 1, LRU_CB), lambda b, c: (l, 0, c)),
            pl.BlockSpec((None, 2, None, LRU_CB, 2 * LRU_CB), lambda b, c: (l, 0, c, 0, 0)),
            pl.BlockSpec((None, 2, None, 1, 2 * LRU_CB), lambda b, c: (l, 0, c, 0, 0)),
            pl.BlockSpec((None, 2, LRU_CB), lambda b, c: (l, 0, c)),
        ],
        out_specs=pl.BlockSpec((None, S, LRU_CB), lambda b, c: (b, 0, c)),
        scratch_shapes=[pltpu.VMEM((S, LRU_CB), F32) for _ in range(6)],
        compiler_params=_cparams(("parallel", "parallel")),
        name="rg_lru",
    )(u, u, cw, cb, wg, bg, lam)


def _ret_kernel(lg_ref, q_ref, k_ref, v_ref, g_ref, cos_ref, sin_ref, y_ref, ks_ref, kv_ref, st_ref,
                *, n_ctx, layer):
    C = RET_CHUNK
    S = q_ref.shape[0]
    n_all = S // C
    unroll = next(f for f in (6, 3, 2, 1) if n_all % f == 0)
    hd = pl.program_id(1)
    lgf = lg_ref[2 * layer, hd]
    lgb = lg_ref[2 * layer + 1, hd]

    ri = lax.broadcasted_iota(jnp.int32, (C, C), 0).astype(F32)
    ci = lax.broadcasted_iota(jnp.int32, (C, C), 1).astype(F32)
    diff = ri - ci
    decay = jnp.where(diff >= 0, jnp.exp(jnp.maximum(diff, 0.0) * lgf),
                      jnp.exp(jnp.maximum(-diff, 0.0) * lgb))
    zeta_f = jnp.exp((C - 1 - ri) * lgf)
    xi_f = jnp.exp((ri + 1) * lgf)
    gc_f = jnp.exp(jnp.full((C, C), C, F32) * lgf)
    zeta_b = jnp.exp(ri * lgb)
    xi_b = jnp.exp((C - ri) * lgb)
    gc_b = jnp.exp(jnp.full((C, C), C, F32) * lgb)
    half = RET_DH // 4

    def rows_of(c):
        return pl.ds(pl.multiple_of(c * C, C), C)

    def chunk_kv(c, carry):
        sl = rows_of(c)
        k = _rope(k_ref[sl, :].astype(F32), cos_ref[sl, :], sin_ref[sl, :], half) * RET_K_SCALE
        ks_ref[sl, :] = k.astype(BF16)
        kz = jnp.concatenate([k * zeta_f, k * zeta_b], axis=1).astype(BF16)
        kv_ref[c] = lax.dot_general(kz, v_ref[sl, :].astype(BF16), (((0,), (0,)), ((), ())),
                                    preferred_element_type=F32)
        return carry

    lax.fori_loop(0, n_all, chunk_kv, 0, unroll=unroll)

    def fwd_state(c, r):
        st_ref[c, 0:C, :] = r.astype(BF16)
        return gc_f * r + kv_ref[c, 0:C, :]

    def bwd_state(i, r):
        c = jnp.where(i < n_ctx, n_ctx - 1 - i, n_all - 1 - (i - n_ctx))
        st_ref[c, C:2 * C, :] = r.astype(BF16)
        return gc_b * r + kv_ref[c, C:2 * C, :]

    lax.fori_loop(0, n_all, fwd_state, jnp.zeros((C, C), F32))
    lax.fori_loop(0, n_all, bwd_state, jnp.zeros((C, C), F32))

    def chunk_out(c, carry):
        sl = rows_of(c)
        q = _rope(q_ref[sl, :].astype(F32), cos_ref[sl, :], sin_ref[sl, :], half)
        s = lax.dot_general(q.astype(BF16), ks_ref[sl, :], (((1,), (1,)), ((), ())),
                            preferred_element_type=F32)
        o = jnp.dot((s * decay).astype(BF16), v_ref[sl, :].astype(BF16), preferred_element_type=F32)
        qx = jnp.concatenate([q * xi_f, q * xi_b], axis=1).astype(BF16)
        o = o + jnp.dot(qx, st_ref[c], preferred_element_type=F32)
        mu = jnp.mean(o, axis=-1, keepdims=True)
        oc = o - mu
        var = jnp.mean(oc * oc, axis=-1, keepdims=True)
        y = oc * lax.rsqrt(var + LN_EPS) * _silu(g_ref[sl, :].astype(F32))
        y_ref[sl, :] = y.astype(y_ref.dtype)
        return carry

    lax.fori_loop(0, n_all, chunk_out, 0, unroll=unroll)


def _ret_call(u, log_g, l, cos, sin, ctx):
    B, S, _ = u.shape
    n_all = S // RET_CHUNK

    def col(base):
        return lambda b, h: (b, 0, base // RET_DH + h)

    return pl.pallas_call(
        functools.partial(_ret_kernel, n_ctx=ctx // RET_CHUNK, layer=l),
        out_shape=jax.ShapeDtypeStruct((B, S, RET_W), BF16),
        grid=(B, RET_HEADS),
        in_specs=[
            pl.BlockSpec(memory_space=pltpu.SMEM),
            pl.BlockSpec((None, S, RET_DH), col(U_RET_Q)),
            pl.BlockSpec((None, S, RET_DH), col(U_RET_K)),
            pl.BlockSpec((None, S, RET_DH), col(U_RET_V)),
            pl.BlockSpec((None, S, RET_DH), col(U_RET_G)),
            pl.BlockSpec((S, RET_DH), lambda b, h: (0, 0)),
            pl.BlockSpec((S, RET_DH), lambda b, h: (0, 0)),
        ],
        out_specs=pl.BlockSpec((None, S, RET_DH), lambda b, h: (b, 0, h)),
        scratch_shapes=[pltpu.VMEM((S, RET_DH), BF16),
                        pltpu.VMEM((n_all, 2 * RET_CHUNK, RET_DH), F32),
                        pltpu.VMEM((n_all, 2 * RET_CHUNK, RET_DH), BF16)],
        compiler_params=_cparams(("parallel", "parallel")),
        name="retention",
    )(log_g, u, u, u, u, cos, sin)


def _outproj_kernel(ym_ref, yl_ref, yr_ref, h_ref, mod_ref, modc_ref, w_ref, lng_ref, lnb_ref, o_ref,
                    *, tm, ctx, alpha, skip):
    j = pl.program_id(1) + skip
    for r0 in range(0, tm, SUB_ROWS):
        rs = slice(r0, r0 + SUB_ROWS)
        rows = j * tm + r0 + lax.broadcasted_iota(jnp.int32, (SUB_ROWS, 1), 0)
        gt = jnp.where(rows < ctx, modc_ref[2:3, :], mod_ref[2:3, :])
        y = jnp.concatenate([ym_ref[rs, :], yl_ref[rs, :], yr_ref[rs, :]], axis=1)
        acc = jnp.dot(y, w_ref[...], preferred_element_type=F32)
        z = alpha * h_ref[rs, :] + gt * acc
        mu = jnp.mean(z, axis=-1, keepdims=True)
        zc = z - mu
        var = jnp.mean(zc * zc, axis=-1, keepdims=True)
        o_ref[rs, :] = zc * lax.rsqrt(var + LN_EPS) * lng_ref[...] + lnb_ref[...]


def _outproj_call(ym, yl, yr, h, mod, w, lng, lnb, l, ctx, tm, alpha, latent_only):
    B, S, D = h.shape
    skip = ctx // tm if latent_only else 0
    assert skip * tm == (ctx if latent_only else 0)
    n_out = S // tm - skip
    return pl.pallas_call(
        functools.partial(_outproj_kernel, tm=tm, ctx=ctx, alpha=alpha, skip=skip),
        out_shape=jax.ShapeDtypeStruct((B, n_out * tm, D), F32),
        grid=(B, n_out),
        in_specs=[
            pl.BlockSpec((None, tm, MLA_W), lambda b, j: (b, j + skip, 0)),
            pl.BlockSpec((None, tm, LRU_W), lambda b, j: (b, j + skip, 0)),
            pl.BlockSpec((None, tm, RET_W), lambda b, j: (b, j + skip, 0)),
            pl.BlockSpec((None, tm, D), lambda b, j: (b, j + skip, 0)),
            pl.BlockSpec((None, None, 8, D), lambda b, j: (l, b, 0, 0)),
            pl.BlockSpec((None, None, 8, D), lambda b, j: (l, B, 0, 0)),
            pl.BlockSpec((None, D, D), lambda b, j: (l, 0, 0), pipeline_mode=pl.Buffered(1)),
            pl.BlockSpec((None, 1, D), lambda b, j: (l, 0, 0)),
            pl.BlockSpec((None, 1, D), lambda b, j: (l, 0, 0)),
        ],
        out_specs=pl.BlockSpec((None, tm, D), lambda b, j: (b, j, 0)),
        compiler_params=_cparams(("parallel", "parallel")),
        name="out_proj_ln",
    )(ym, yl, yr, h, mod, mod, w, lng, lnb)


def _rope_tables(rows, dim, ctx):
    row = jnp.repeat(jnp.arange(rows, dtype=F32), GRID_W)
    col = jnp.tile(jnp.arange(GRID_W, dtype=F32), rows)
    quarter = dim // 4
    inv = ROPE_BASE ** (-jnp.arange(quarter, dtype=F32) / quarter)
    ang = jnp.stack([row[:, None] * inv, col[:, None] * inv], axis=1)
    cos, sin = jnp.cos(ang), jnp.sin(ang)
    cos_l = jnp.concatenate([cos[:, 0], cos[:, 0], cos[:, 1], cos[:, 1]], axis=-1)
    sin_l = jnp.concatenate([-sin[:, 0], sin[:, 0], -sin[:, 1], sin[:, 1]], axis=-1)
    pad = LANES - dim
    cos_l = jnp.pad(cos_l, ((0, 0), (0, pad)))
    sin_l = jnp.pad(sin_l, ((0, 0), (0, pad)))
    cos_c = jnp.pad(jnp.ones((ctx, dim), F32), ((0, 0), (0, pad)))
    sin_c = jnp.zeros((ctx, LANES), F32)
    return jnp.concatenate([cos_c, cos_l], axis=0), jnp.concatenate([sin_c, sin_l], axis=0)


W_IN_MOVES = ((0, U_QLAT, 512), (512, U_KVLAT, 256), (768, U_KR, MLA_ROPE), (832, U_MLA_G, 1024),
              (1856, U_LRU_X, 512), (2368, U_LRU_G, 512), (2880, U_RET_Q, 512), (3392, U_RET_K, 512),
              (3904, U_RET_V, 512), (4416, U_RET_G, 512))


def _w_in_layout_kernel(w_ref, o_ref):
    for src, dst, width in W_IN_MOVES:
        piece = w_ref[:, src:src + width].astype(BF16)
        if width % LANES:
            piece = jnp.concatenate([piece, jnp.zeros((piece.shape[0], LANES - width % LANES), BF16)],
                                    axis=1)
        o_ref[:, dst:dst + piece.shape[1]] = piece


def _layout_w_in(w_in):
    L, D, W = w_in.shape
    tr = 256
    return pl.pallas_call(
        _w_in_layout_kernel,
        out_shape=jax.ShapeDtypeStruct((L, D, U_W), BF16),
        grid=(L, D // tr),
        in_specs=[pl.BlockSpec((None, tr, W), lambda l, r: (l, r, 0))],
        out_specs=pl.BlockSpec((None, tr, U_W), lambda l, r: (l, r, 0)),
        compiler_params=_cparams(("parallel", "parallel")),
        name="w_in_layout",
    )(w_in)


def _layout_w_uq(w_uq):
    L = w_uq.shape[0]
    w = w_uq.reshape(L, MLA_Q_RANK, MLA_HEADS, MLA_NOPE + MLA_ROPE)
    w = jnp.pad(w, ((0, 0), (0, 0), (0, 0), (0, MLA_HP - MLA_NOPE - MLA_ROPE)))
    return w.reshape(L, MLA_Q_RANK, MLA_HEADS * MLA_HP).astype(BF16)


def _layout_w_ukv(w_ukv):
    L = w_ukv.shape[0]
    w = w_ukv.reshape(L, MLA_KV_RANK, MLA_HEADS, MLA_NOPE + MLA_V)
    wk = w[..., :MLA_NOPE].reshape(L, MLA_KV_RANK, MLA_W).astype(BF16)
    wvt = w[..., MLA_NOPE:].reshape(L, MLA_KV_RANK, MLA_W).transpose(0, 2, 1).astype(BF16)
    return wk, wvt


def _layout_lru_gates(w_r, b_r, w_i, b_i):
    L = w_r.shape[0]
    eye = jnp.eye(LRU_BLOCKS, dtype=w_r.dtype)
    ncb = LRU_W // LRU_CB

    def dense(w):
        return jnp.einsum("ldgij,gh->ldgihj", w, eye).reshape(L, 2, LRU_W, LRU_W)

    def diag_blocks(w):
        w = w.reshape(L, 2, ncb, LRU_CB, ncb, LRU_CB)
        return jnp.stack([w[:, :, c, :, c, :] for c in range(ncb)], axis=2)

    wg = jnp.concatenate([diag_blocks(dense(w_r)), diag_blocks(dense(w_i))], axis=-1).astype(BF16)
    bg = jnp.concatenate([b_r.reshape(L, 2, ncb, 1, LRU_CB), b_i.reshape(L, 2, ncb, 1, LRU_CB)], axis=-1)
    return wg, bg


def kernel(x, c, ctx, c_ctx, w_ada, b_ada, w_in, mla_q_norm_g, mla_kv_norm_g, mla_w_uq, mla_w_ukv,
           lru_conv_w, lru_conv_b, lru_w_r, lru_b_r, lru_w_i, lru_b_i, lru_lambda, ret_decay,
           w_out, ln_g, ln_b):
    B, T, D = x.shape
    L = w_in.shape[0]
    n_ctx = ctx.shape[1]
    S = n_ctx + T
    assert D == 2 * MLA_W and w_in.shape[2] == 4928 and n_ctx % 256 == 0 and T % 256 == 0
    alpha = (2 * L) ** 0.25
    tm = 768 if S % 768 == 0 else 256

    cos_m, sin_m = _rope_tables(T // GRID_W, MLA_ROPE, n_ctx)
    cos_r, sin_r = _rope_tables(T // GRID_W, RET_DH, n_ctx)

    cond = jnp.zeros((8, D), F32).at[:B].set(c).at[B].set(c_ctx)
    mod_all = _ada_call(cond, w_ada, b_ada)
    mod_all = mod_all[:, :B + 1].reshape(L, B + 1, 3, D)
    mod_all = jnp.pad(mod_all, ((0, 0), (0, 0), (0, 5), (0, 0)))

    h = _ln0_call(ctx, x)
    log_g = jax.nn.log_sigmoid(ret_decay.astype(F32)).reshape(2 * L, RET_HEADS)

    w_in_l = _layout_w_in(w_in)
    wuq = _layout_w_uq(mla_w_uq)
    wk, wvt = _layout_w_ukv(mla_w_ukv)
    wg, bg = _layout_lru_gates(lru_w_r, lru_b_r, lru_w_i, lru_b_i)
    w_out_b = w_out.astype(BF16)
    gq, gkv = mla_q_norm_g[:, None, :], mla_kv_norm_g[:, None, :]
    conv_b, lng, lnb = lru_conv_b[:, None, :], ln_g[:, None, :], ln_b[:, None, :]

    for l in range(L):
        last = l == L - 1
        u = _inproj_call(h, mod_all, w_in_l, l, n_ctx, tm)
        q, k, vt = _mla_prep_call(u, gq, gkv, wuq, wk, wvt, l, cos_m, sin_m)
        y_mla = _attn_call(q, k, vt, u, n_ctx)
        y_lru = _lru_call(u, lru_conv_w, conv_b, wg, bg, lru_lambda, l, n_ctx)
        y_ret = _ret_call(u, log_g, l, cos_r, sin_r, n_ctx)
        h = _outproj_call(y_mla, y_lru, y_ret, h, mod_all, w_out_b, lng, lnb, l, n_ctx,
                          256 if last else tm, alpha, last)
    return h
```

```python
import functools

import jax
import jax.numpy as jnp
from jax import lax
from jax.experimental import pallas as pl
from jax.experimental.pallas import tpu as pltpu

F32 = jnp.float32
BF16 = jnp.bfloat16

GRID_W = 64
MLA_V = 128
MLA_NOPE = 128
MLA_ROPE = 64
MLA_HEADS = 8
MLA_W = MLA_HEADS * MLA_V
MLA_Q_RANK = 512
MLA_KV_RANK = 256
MLA_SCALE = (MLA_NOPE + MLA_ROPE) ** -0.5
LOG2_E = 1.4426950408889634
MLA_Q_SCALE = MLA_SCALE * LOG2_E
MLA_HP = 256
LRU_W = 512
LRU_BLOCKS = 8
LRU_BW = LRU_W // LRU_BLOCKS
LRU_CONV = 4
LRU_C = 8.0
LRU_CB = 256
RET_HEADS = 4
RET_DH = 128
RET_W = RET_HEADS * RET_DH
RET_CHUNK = 128
RET_K_SCALE = RET_DH ** -0.5
ROPE_BASE = 10000.0
LN_EPS = 1e-5
RMS_EPS = 1e-6
LANES = 128
BF16_ROWS = 16
KEY_CHUNK = 768
ATTN_BUFS = 3
SUB_ROWS = 256

U_QLAT = 0
U_LRU_X = 512
U_LRU_G = 1024
U_RET_Q = 1536
U_RET_K = 2048
U_RET_V = 2560
U_RET_G = 3072
U_MLA_G = 3584
U_KVLAT = 4608
U_KR = 4864
U_W = 4992
U_CHUNKS = 3
U_TN = U_W // U_CHUNKS

VMEM_LIMIT = 56 * 1024 * 1024


def _cparams(sem):
    return pltpu.CompilerParams(dimension_semantics=sem, vmem_limit_bytes=VMEM_LIMIT)


def _dot_nt(a, b):
    return lax.dot_general(a, b, (((1,), (1,)), ((), ())), preferred_element_type=F32)


def _sigmoid(x):
    return 0.5 * jnp.tanh(0.5 * x) + 0.5


def _silu(x):
    return x * _sigmoid(x)


def _softplus(x):
    return jnp.maximum(x, 0.0) + jnp.log1p(jnp.exp(-jnp.abs(x)))


def _pair_swap(x, half):
    n = x.shape[-1]
    lane = lax.broadcasted_iota(jnp.int32, x.shape, x.ndim - 1)
    up = pltpu.roll(x, n - half, axis=x.ndim - 1)
    down = pltpu.roll(x, half, axis=x.ndim - 1)
    return jnp.where(lane % (2 * half) < half, up, down)


def _rope(x, cos, sin, half):
    return x * cos + _pair_swap(x, half) * sin


def _ada_kernel(cond_ref, w_ref, b_ref, o_ref):
    c = cond_ref[...]
    a = _silu(c).astype(BF16)
    o_ref[...] = jnp.dot(a, w_ref[...].astype(BF16), preferred_element_type=F32) + b_ref[...]


def _ada_call(cond, w_ada, b_ada):
    L, D, D3 = w_ada.shape
    tn = 1024
    return pl.pallas_call(
        _ada_kernel,
        out_shape=jax.ShapeDtypeStruct((L, 8, D3), F32),
        grid=(L, D3 // tn),
        in_specs=[
            pl.BlockSpec((8, D), lambda l, n: (0, 0)),
            pl.BlockSpec((None, D, tn), lambda l, n: (l, 0, n)),
            pl.BlockSpec((None, 1, tn), lambda l, n: (l, 0, n)),
        ],
        out_specs=pl.BlockSpec((None, 8, tn), lambda l, n: (l, 0, n)),
        compiler_params=_cparams(("parallel", "parallel")),
        name="ada_mod",
    )(cond, w_ada, b_ada.reshape(L, 1, D3))


def _ln0_kernel(c_ref, x_ref, o_ref, *, n_ctx_blocks):
    def norm(ref):
        x = ref[...]
        mu = jnp.mean(x, axis=-1, keepdims=True)
        xc = x - mu
        var = jnp.mean(xc * xc, axis=-1, keepdims=True)
        o_ref[...] = xc * lax.rsqrt(var + LN_EPS)

    j = pl.program_id(1)
    pl.when(j < n_ctx_blocks)(lambda: norm(c_ref))
    pl.when(j >= n_ctx_blocks)(lambda: norm(x_ref))


def _ln0_call(ctx, x):
    B, T, D = x.shape
    n_ctx = ctx.shape[1]
    tm = 256
    nc = n_ctx // tm
    return pl.pallas_call(
        functools.partial(_ln0_kernel, n_ctx_blocks=nc),
        out_shape=jax.ShapeDtypeStruct((B, n_ctx + T, D), F32),
        grid=(B, (n_ctx + T) // tm),
        in_specs=[pl.BlockSpec((None, tm, D), lambda b, j: (b, jnp.minimum(j, nc - 1), 0)),
                  pl.BlockSpec((None, tm, D), lambda b, j: (b, jnp.maximum(j - nc, 0), 0))],
        out_specs=pl.BlockSpec((None, tm, D), lambda b, j: (b, j, 0)),
        compiler_params=_cparams(("parallel", "parallel")),
        name="ln_entry",
    )(ctx, x)


def _inproj_kernel(h_ref, mod_ref, modc_ref, *rest, tm, ctx):
    w_refs, (u_ref, xs_ref) = rest[:U_CHUNKS], rest[U_CHUNKS:]
    j = pl.program_id(1)
    n = pl.program_id(2)

    @pl.when(n == 0)
    def _():
        for r0 in range(0, tm, SUB_ROWS):
            rs = slice(r0, r0 + SUB_ROWS)
            rows = j * tm + r0 + lax.broadcasted_iota(jnp.int32, (SUB_ROWS, 1), 0)
            is_ctx = rows < ctx
            sh = jnp.where(is_ctx, modc_ref[0:1, :], mod_ref[0:1, :])
            sc = jnp.where(is_ctx, modc_ref[1:2, :], mod_ref[1:2, :])
            xs = (h_ref[rs, :] * (1.0 + sc) + sh).astype(BF16)
            xs_ref[rs, :] = xs
            u_ref[rs, :] = _dot_nt(xs, w_refs[0][...]).astype(u_ref.dtype)

    for c in range(1, U_CHUNKS):
        @pl.when(n == c)
        def _():
            u_ref[...] = _dot_nt(xs_ref[...], w_refs[c][...]).astype(u_ref.dtype)


def _inproj_call(h, mod_all, w_all, l, ctx, tm):
    B, S, D = h.shape
    return pl.pallas_call(
        functools.partial(_inproj_kernel, tm=tm, ctx=ctx),
        out_shape=jax.ShapeDtypeStruct((B, S, U_W), F32),
        grid=(B, S // tm, U_CHUNKS),
        in_specs=[
            pl.BlockSpec((None, tm, D), lambda b, j, n: (b, j, 0)),
            pl.BlockSpec((None, None, 8, D), lambda b, j, n: (l, b, 0, 0)),
            pl.BlockSpec((None, None, 8, D), lambda b, j, n: (l, B, 0, 0)),
        ] + [
            pl.BlockSpec((None, U_TN, D), functools.partial(lambda b, j, n, c: (l, c, 0), c=c),
                         pipeline_mode=pl.Buffered(1))
            for c in range(U_CHUNKS)
        ],
        out_specs=pl.BlockSpec((None, tm, U_TN), lambda b, j, n: (b, j, n)),
        scratch_shapes=[pltpu.VMEM((tm, D), BF16)],
        compiler_params=_cparams(("parallel", "parallel", "arbitrary")),
        name="in_proj",
    )(h, mod_all, mod_all, *([w_all] * U_CHUNKS))


def _rms(x, g):
    return x * lax.rsqrt(jnp.mean(x * x, axis=-1, keepdims=True) + RMS_EPS) * g


def _mla_prep_kernel(ql_ref, kvl_ref, kr_ref, gq_ref, gkv_ref, wuq_ref, wk_ref, wvt_ref,
                     cos_ref, sin_ref, q_ref, k_ref, vt_ref):
    cos = cos_ref[...]
    sin = sin_ref[...]
    half = MLA_ROPE // 4

    zq = _rms(ql_ref[...].astype(F32), gq_ref[...]).astype(BF16)
    q = jnp.dot(zq, wuq_ref[...], preferred_element_type=F32)
    for hd in range(MLA_HEADS):
        c0 = hd * MLA_HP
        q_ref[:, c0:c0 + MLA_NOPE] = (q[:, c0:c0 + MLA_NOPE] * MLA_Q_SCALE).astype(q_ref.dtype)
        qr = _rope(q[:, c0 + MLA_NOPE:c0 + MLA_HP], cos, sin, half)
        q_ref[:, c0 + MLA_NOPE:c0 + MLA_HP] = (qr * MLA_Q_SCALE).astype(q_ref.dtype)

    zk = _rms(kvl_ref[...].astype(F32), gkv_ref[...]).astype(BF16)
    kn = jnp.dot(zk, wk_ref[...], preferred_element_type=F32)
    kr = _rope(kr_ref[...].astype(F32), cos, sin, half).astype(k_ref.dtype)
    for hd in range(MLA_HEADS):
        c0 = hd * MLA_HP
        k_ref[:, c0:c0 + MLA_NOPE] = kn[:, hd * MLA_NOPE:(hd + 1) * MLA_NOPE].astype(k_ref.dtype)
        k_ref[:, c0 + MLA_NOPE:c0 + MLA_HP] = kr
    vt = lax.dot_general(wvt_ref[...], zk, (((1,), (1,)), ((), ())), preferred_element_type=F32)
    vt_ref[...] = vt.astype(vt_ref.dtype)


def _mla_prep_call(u, gq, gkv, wuq, wk, wvt, l, cos, sin):
    B, S, _ = u.shape
    tp = 768 if S % 768 == 0 else 256
    HW = MLA_HEADS * MLA_HP
    return pl.pallas_call(
        _mla_prep_kernel,
        out_shape=(
            jax.ShapeDtypeStruct((B, S, HW), BF16),
            jax.ShapeDtypeStruct((B, S, HW), BF16),
            jax.ShapeDtypeStruct((B, MLA_W, S), BF16),
        ),
        grid=(B, S // tp),
        in_specs=[
            pl.BlockSpec((None, tp, MLA_Q_RANK), lambda b, j: (b, j, U_QLAT // MLA_Q_RANK)),
            pl.BlockSpec((None, tp, MLA_KV_RANK), lambda b, j: (b, j, U_KVLAT // MLA_KV_RANK)),
            pl.BlockSpec((None, tp, LANES), lambda b, j: (b, j, U_KR // LANES)),
            pl.BlockSpec((None, 1, MLA_Q_RANK), lambda b, j: (l, 0, 0)),
            pl.BlockSpec((None, 1, MLA_KV_RANK), lambda b, j: (l, 0, 0)),
            pl.BlockSpec((None, MLA_Q_RANK, HW), lambda b, j: (l, 0, 0)),
            pl.BlockSpec((None, MLA_KV_RANK, MLA_W), lambda b, j: (l, 0, 0)),
            pl.BlockSpec((None, MLA_W, MLA_KV_RANK), lambda b, j: (l, 0, 0)),
            pl.BlockSpec((tp, LANES), lambda b, j: (j, 0)),
            pl.BlockSpec((tp, LANES), lambda b, j: (j, 0)),
        ],
        out_specs=(
            pl.BlockSpec((None, tp, HW), lambda b, j: (b, j, 0)),
            pl.BlockSpec((None, tp, HW), lambda b, j: (b, j, 0)),
            pl.BlockSpec((None, MLA_W, tp), lambda b, j: (b, 0, j)),
        ),
        compiler_params=_cparams(("parallel", "parallel")),
        name="mla_prep",
    )(u, u, u, gq, gkv, wuq, wk, wvt, cos, sin)


def _attn_kernel(q_ref, k_ref, vt_ref, g_ref, y_ref, st_ref, m_ref, p_ref, vt1_ref, *, ctx, tq):
    S = k_ref.shape[0]
    n_lat = S // tq - 1
    assert n_lat >= 2 and S % KEY_CHUNK == 0

    vt1_ref[0:MLA_V, :] = vt_ref[...]
    vt1_ref[MLA_V:, :] = jnp.ones((BF16_ROWS, S), BF16)

    def rows_of(blk):
        return slice(blk * tq, (blk + 1) * tq)

    def scores(blk, nk):
        return lax.dot_general(k_ref[0:nk, :], q_ref[rows_of(blk), :], (((1,), (1,)), ((), ())),
                               preferred_element_type=F32)

    def numerators(st, m):
        return jnp.exp2(st - m).astype(BF16)

    def attend(blk, p, nk):
        rows = rows_of(blk)
        ot = jnp.dot(vt1_ref[:, 0:nk], p, preferred_element_type=F32)
        o = (ot[0:MLA_V, :] / ot[MLA_V:MLA_V + 1, :]).T
        y_ref[rows, :] = (o * _silu(g_ref[rows, :].astype(F32))).astype(y_ref.dtype)

    def step(score=None, num=None, att=None):
        sb, nb, ab = (None if b is None else b % ATTN_BUFS for b in (score, num, att))
        qb = None if score is None else q_ref[rows_of(score), :]
        m_num = None if num is None else m_ref[nb]
        m_run, acc = None, None
        for c0 in range(0, S, KEY_CHUNK):
            ks = slice(c0, c0 + KEY_CHUNK)
            if score is not None:
                st = lax.dot_general(k_ref[ks, :], qb, (((1,), (1,)), ((), ())),
                                     preferred_element_type=F32)
                st_ref[sb, ks, :] = st
                m_c = jnp.max(st, axis=0, keepdims=True)
                m_run = m_c if m_run is None else jnp.maximum(m_run, m_c)
            if num is not None:
                p_ref[nb, ks, :] = numerators(st_ref[nb, ks, :], m_num)
            if att is not None:
                part = jnp.dot(vt1_ref[:, ks], p_ref[ab, ks, :], preferred_element_type=F32)
                acc = part if acc is None else acc + part
        if score is not None:
            m_ref[sb] = m_run
        if att is not None:
            rows = rows_of(att)
            o = (acc[0:MLA_V, :] / acc[MLA_V:MLA_V + 1, :]).T
            y_ref[rows, :] = (o * _silu(g_ref[rows, :].astype(F32))).astype(y_ref.dtype)

    def latent(blk):
        return blk if blk <= n_lat else None

    step(score=1)
    step(score=latent(2), num=1)
    st_ctx = scores(0, ctx)
    attend(0, numerators(st_ctx, jnp.max(st_ctx, axis=0, keepdims=True)), ctx)
    for t in range(1, n_lat + 1):
        step(score=latent(t + 2), num=latent(t + 1), att=t)


def _attn_call(q, k, vt, u, ctx):
    B, S, _ = q.shape
    tq = 256
    assert ctx == tq
    gcol = U_MLA_G // MLA_V
    return pl.pallas_call(
        functools.partial(_attn_kernel, ctx=ctx, tq=tq),
        out_shape=jax.ShapeDtypeStruct((B, S, MLA_W), BF16),
        grid=(B, MLA_HEADS),
        in_specs=[
            pl.BlockSpec((None, S, MLA_HP), lambda b, h: (b, 0, h)),
            pl.BlockSpec((None, S, MLA_HP), lambda b, h: (b, 0, h)),
            pl.BlockSpec((None, MLA_V, S), lambda b, h: (b, h, 0)),
            pl.BlockSpec((None, S, MLA_V), lambda b, h: (b, 0, gcol + h)),
        ],
        out_specs=pl.BlockSpec((None, S, MLA_V), lambda b, h: (b, 0, h)),
        scratch_shapes=[pltpu.VMEM((ATTN_BUFS, S, tq), F32), pltpu.VMEM((ATTN_BUFS, 1, tq), F32),
                        pltpu.VMEM((ATTN_BUFS, S, tq), BF16),
                        pltpu.VMEM((MLA_V + BF16_ROWS, S), BF16)],
        compiler_params=_cparams(("parallel", "parallel")),
        name="mla_attn",
    )(q, k, vt, u)


def _lru_kernel(x_ref, g_ref, cw_ref, cb_ref, wg_ref, bg_ref, lam_ref, y_ref,
                af_ref, bf_ref, ab_ref, bb_ref, hf_ref, hb_ref, *, ctx):
    S, CB = x_ref.shape
    x = x_ref[...].astype(F32)
    rows = lax.broadcasted_iota(jnp.int32, (S, 1), 0)
    in_ctx = rows < ctx

    xc = x * cw_ref[2:3, :] + cb_ref[...]
    for tap, off in ((0, -2), (1, -1), (3, 1)):
        xs = pltpu.roll(x, (-off) % S, axis=0)
        src = rows + off
        valid = (src >= 0) & (src < S) & ((src < ctx) == in_ctx)
        xc = xc + jnp.where(valid, xs, 0.0) * cw_ref[tap:tap + 1, :]

    xb = xc.astype(BF16)
    xh = 0.5 * xc
    for d, (a_ref, b_ref) in enumerate(((af_ref, bf_ref), (ab_ref, bb_ref))):
        t = jnp.tanh(jnp.dot(xb, wg_ref[d], preferred_element_type=F32) + bg_ref[d])
        t_r, t_i = t[:, :CB], t[:, CB:]
        c2 = (-0.5 * LRU_C * LOG2_E) * _softplus(-lam_ref[d:d + 1, :])
        a = jnp.exp2(t_r * c2 + c2)
        w = (1.0 - a) * (1.0 + a)
        a_ref[...] = a
        b_ref[...] = jnp.where(w > 0.0, w * lax.rsqrt(w), 0.0) * (t_i * xh + xh)

    R = 8
    n_t, n_ctx_t = S // R, ctx // R
    row = lax.broadcasted_iota(jnp.int32, (R, CB), 0)

    def tile_prefix(a, b, rev):
        for d in (1, 2, 4):
            shift = R - d if rev else d
            keep = (row < R - d) if rev else (row >= d)
            a_s = pltpu.roll(a, shift, axis=0)
            b_s = pltpu.roll(b, shift, axis=0)
            b = jnp.where(keep, a * b_s, 0.0) + b
            a = jnp.where(keep, a * a_s, a)
        return a, b

    def tile_step(i, carry):
        hf, hb = carry
        sf = pl.ds(pl.multiple_of(i * R, R), R)
        ib = jnp.where(i < n_ctx_t, n_ctx_t - 1 - i, n_t - 1 - (i - n_ctx_t))
        sb = pl.ds(pl.multiple_of(ib * R, R), R)
        pa, pb = tile_prefix(af_ref[sf, :], bf_ref[sf, :], False)
        of = pa * hf + pb
        qa, qb = tile_prefix(ab_ref[sb, :], bb_ref[sb, :], True)
        ob = qa * hb + qb
        hf_ref[sf, :] = of
        hb_ref[sb, :] = ob
        return (jnp.broadcast_to(of[R - 1:R, :], (R, CB)), jnp.broadcast_to(ob[0:1, :], (R, CB)))

    zero = jnp.zeros((R, CB), F32)
    lax.fori_loop(0, n_t, tile_step, (zero, zero), unroll=4)

    y = (hf_ref[...] + hb_ref[...]) * _silu(g_ref[...].astype(F32))
    y_ref[...] = y.astype(y_ref.dtype)


def _lru_call(u, cw, cb, wg, bg, lam, l, ctx):
    B, S, _ = u.shape
    ncb = LRU_W // LRU_CB
    return pl.pallas_call(
        functools.partial(_lru_kernel, ctx=ctx),
        out_shape=jax.ShapeDtypeStruct((B, S, LRU_W), BF16),
        grid=(B, ncb),
        in_specs=[
            pl.BlockSpec((None, S, LRU_CB), lambda b, c: (b, 0, U_LRU_X // LRU_CB + c)),
            pl.BlockSpec((None, S, LRU_CB), lambda b, c: (b, 0, U_LRU_G // LRU_CB + c)),
            pl.BlockSpec((None, LRU_CONV, LRU_CB), lambda b, c: (l, 0, c)),
            pl.BlockSpec((None, 1, LRU_CB), lambda b, c: (l, 0, c)),
            pl.BlockSpec((None, 2, None, LRU_CB, 2 * LRU_CB), lambda b, c: (l, 0, c, 0, 0)),
            pl.BlockSpec((None, 2, None, 1, 2 * LRU_CB), lambda b, c: (l, 0, c, 0, 0)),
            pl.BlockSpec((None, 2, LRU_CB), lambda b, c: (l, 0, c)),
        ],
        out_specs=pl.BlockSpec((None, S, LRU_CB), lambda b, c: (b, 0, c)),
        scratch_shapes=[pltpu.VMEM((S, LRU_CB), F32) for _ in range(6)],
        compiler_params=_cparams(("parallel", "parallel")),
        name="rg_lru",
    )(u, u, cw, cb, wg, bg, lam)


def _ret_kernel(lg_ref, q_ref, k_ref, v_ref, g_ref, cos_ref, sin_ref, y_ref, ks_ref, kv_ref, st_ref,
                *, n_ctx, layer):
    C = RET_CHUNK
    S = q_ref.shape[0]
    n_all = S // C
    unroll = next(f for f in (6, 3, 2, 1) if n_all % f == 0)
    hd = pl.program_id(1)
    lgf = lg_ref[2 * layer, hd]
    lgb = lg_ref[2 * layer + 1, hd]

    ri = lax.broadcasted_iota(jnp.int32, (C, C), 0).astype(F32)
    ci = lax.broadcasted_iota(jnp.int32, (C, C), 1).astype(F32)
    diff = ri - ci
    decay = jnp.where(diff >= 0, jnp.exp(jnp.maximum(diff, 0.0) * lgf),
                      jnp.exp(jnp.maximum(-diff, 0.0) * lgb))
    zeta_f = jnp.exp((C - 1 - ri) * lgf)
    xi_f = jnp.exp((ri + 1) * lgf)
    gc_f = jnp.exp(jnp.full((C, C), C, F32) * lgf)
    zeta_b = jnp.exp(ri * lgb)
    xi_b = jnp.exp((C - ri) * lgb)
    gc_b = jnp.exp(jnp.full((C, C), C, F32) * lgb)
    half = RET_DH // 4

    def rows_of(c):
        return pl.ds(pl.multiple_of(c * C, C), C)

    def chunk_kv(c, carry):
        sl = rows_of(c)
        k = _rope(k_ref[sl, :].astype(F32), cos_ref[sl, :], sin_ref[sl, :], half) * RET_K_SCALE
        ks_ref[sl, :] = k.astype(BF16)
        kz = jnp.concatenate([k * zeta_f, k * zeta_b], axis=1).astype(BF16)
        kv_ref[c] = lax.dot_general(kz, v_ref[sl, :].astype(BF16), (((0,), (0,)), ((), ())),
                                    preferred_element_type=F32)
        return carry

    lax.fori_loop(0, n_all, chunk_kv, 0, unroll=unroll)

    def fwd_state(c, r):
        st_ref[c, 0:C, :] = r.astype(BF16)
        return gc_f * r + kv_ref[c, 0:C, :]

    def bwd_state(i, r):
        c = jnp.where(i < n_ctx, n_ctx - 1 - i, n_all - 1 - (i - n_ctx))
        st_ref[c, C:2 * C, :] = r.astype(BF16)
        return gc_b * r + kv_ref[c, C:2 * C, :]

    lax.fori_loop(0, n_all, fwd_state, jnp.zeros((C, C), F32))
    lax.fori_loop(0, n_all, bwd_state, jnp.zeros((C, C), F32))

    def chunk_out(c, carry):
        sl = rows_of(c)
        q = _rope(q_ref[sl, :].astype(F32), cos_ref[sl, :], sin_ref[sl, :], half)
        s = lax.dot_general(q.astype(BF16), ks_ref[sl, :], (((1,), (1,)), ((), ())),
                            preferred_element_type=F32)
        o = jnp.dot((s * decay).astype(BF16), v_ref[sl, :].astype(BF16), preferred_element_type=F32)
        qx = jnp.concatenate([q * xi_f, q * xi_b], axis=1).astype(BF16)
        o = o + jnp.dot(qx, st_ref[c], preferred_element_type=F32)
        mu = jnp.mean(o, axis=-1, keepdims=True)
        oc = o - mu
        var = jnp.mean(oc * oc, axis=-1, keepdims=True)
        y = oc * lax.rsqrt(var + LN_EPS) * _silu(g_ref[sl, :].astype(F32))
        y_ref[sl, :] = y.astype(y_ref.dtype)
        return carry

    lax.fori_loop(0, n_all, chunk_out, 0, unroll=unroll)


def _ret_call(u, log_g, l, cos, sin, ctx):
    B, S, _ = u.shape
    n_all = S // RET_CHUNK

    def col(base):
        return lambda b, h: (b, 0, base // RET_DH + h)

    return pl.pallas_call(
        functools.partial(_ret_kernel, n_ctx=ctx // RET_CHUNK, layer=l),
        out_shape=jax.ShapeDtypeStruct((B, S, RET_W), BF16),
        grid=(B, RET_HEADS),
        in_specs=[
            pl.BlockSpec(memory_space=pltpu.SMEM),
            pl.BlockSpec((None, S, RET_DH), col(U_RET_Q)),
            pl.BlockSpec((None, S, RET_DH), col(U_RET_K)),
            pl.BlockSpec((None, S, RET_DH), col(U_RET_V)),
            pl.BlockSpec((None, S, RET_DH), col(U_RET_G)),
            pl.BlockSpec((S, RET_DH), lambda b, h: (0, 0)),
            pl.BlockSpec((S, RET_DH), lambda b, h: (0, 0)),
        ],
        out_specs=pl.BlockSpec((None, S, RET_DH), lambda b, h: (b, 0, h)),
        scratch_shapes=[pltpu.VMEM((S, RET_DH), BF16),
                        pltpu.VMEM((n_all, 2 * RET_CHUNK, RET_DH), F32),
                        pltpu.VMEM((n_all, 2 * RET_CHUNK, RET_DH), BF16)],
        compiler_params=_cparams(("parallel", "parallel")),
        name="retention",
    )(log_g, u, u, u, u, cos, sin)


def _outproj_kernel(ym_ref, yl_ref, yr_ref, h_ref, mod_ref, modc_ref, w_ref, lng_ref, lnb_ref, o_ref,
                    *, tm, ctx, alpha, skip):
    j = pl.program_id(1) + skip
    for r0 in range(0, tm, SUB_ROWS):
        rs = slice(r0, r0 + SUB_ROWS)
        rows = j * tm + r0 + lax.broadcasted_iota(jnp.int32, (SUB_ROWS, 1), 0)
        gt = jnp.where(rows < ctx, modc_ref[2:3, :], mod_ref[2:3, :])
        y = jnp.concatenate([ym_ref[rs, :], yl_ref[rs, :], yr_ref[rs, :]], axis=1)
        acc = jnp.dot(y, w_ref[...], preferred_element_type=F32)
        z = alpha * h_ref[rs, :] + gt * acc
        mu = jnp.mean(z, axis=-1, keepdims=True)
        zc = z - mu
        var = jnp.mean(zc * zc, axis=-1, keepdims=True)
        o_ref[rs, :] = zc * lax.rsqrt(var + LN_EPS) * lng_ref[...] + lnb_ref[...]


def _outproj_call(ym, yl, yr, h, mod, w, lng, lnb, l, ctx, tm, alpha, latent_only):
    B, S, D = h.shape
    skip = ctx // tm if latent_only else 0
    assert skip * tm == (ctx if latent_only else 0)
    n_out = S // tm - skip
    return pl.pallas_call(
        functools.partial(_outproj_kernel, tm=tm, ctx=ctx, alpha=alpha, skip=skip),
        out_shape=jax.ShapeDtypeStruct((B, n_out * tm, D), F32),
        grid=(B, n_out),
        in_specs=[
            pl.BlockSpec((None, tm, MLA_W), lambda b, j: (b, j + skip, 0)),
            pl.BlockSpec((None, tm, LRU_W), lambda b, j: (b, j + skip, 0)),
            pl.BlockSpec((None, tm, RET_W), lambda b, j: (b, j + skip, 0)),
            pl.BlockSpec((None, tm, D), lambda b, j: (b, j + skip, 0)),
            pl.BlockSpec((None, None, 8, D), lambda b, j: (l, b, 0, 0)),
            pl.BlockSpec((None, None, 8, D), lambda b, j: (l, B, 0, 0)),
            pl.BlockSpec((None, D, D), lambda b, j: (l, 0, 0), pipeline_mode=pl.Buffered(1)),
            pl.BlockSpec((None, 1, D), lambda b, j: (l, 0, 0)),
            pl.BlockSpec((None, 1, D), lambda b, j: (l, 0, 0)),
        ],
        out_specs=pl.BlockSpec((None, tm, D), lambda b, j: (b, j, 0)),
        compiler_params=_cparams(("parallel", "parallel")),
        name="out_proj_ln",
    )(ym, yl, yr, h, mod, mod, w, lng, lnb)


def _rope_tables(rows, dim, ctx):
    row = jnp.repeat(jnp.arange(rows, dtype=F32), GRID_W)
    col = jnp.tile(jnp.arange(GRID_W, dtype=F32), rows)
    quarter = dim // 4
    inv = ROPE_BASE ** (-jnp.arange(quarter, dtype=F32) / quarter)
    ang = jnp.stack([row[:, None] * inv, col[:, None] * inv], axis=1)
    cos, sin = jnp.cos(ang), jnp.sin(ang)
    cos_l = jnp.concatenate([cos[:, 0], cos[:, 0], cos[:, 1], cos[:, 1]], axis=-1)
    sin_l = jnp.concatenate([-sin[:, 0], sin[:, 0], -sin[:, 1], sin[:, 1]], axis=-1)
    pad = LANES - dim
    cos_l = jnp.pad(cos_l, ((0, 0), (0, pad)))
    sin_l = jnp.pad(sin_l, ((0, 0), (0, pad)))
    cos_c = jnp.pad(jnp.ones((ctx, dim), F32), ((0, 0), (0, pad)))
    sin_c = jnp.zeros((ctx, LANES), F32)
    return jnp.concatenate([cos_c, cos_l], axis=0), jnp.concatenate([sin_c, sin_l], axis=0)


W_IN_MOVES = ((0, U_QLAT, 512), (512, U_KVLAT, 256), (768, U_KR, MLA_ROPE), (832, U_MLA_G, 1024),
              (1856, U_LRU_X, 512), (2368, U_LRU_G, 512), (2880, U_RET_Q, 512), (3392, U_RET_K, 512),
              (3904, U_RET_V, 512), (4416, U_RET_G, 512))


def _w_in_layout_kernel(w_ref, o_ref):
    for src, dst, width in W_IN_MOVES:
        o_ref[dst:dst + width, :] = w_ref[src:src + width, :].astype(BF16)
    pad0 = U_KR + MLA_ROPE
    o_ref[pad0:U_KR + LANES, :] = jnp.zeros((U_KR + LANES - pad0, o_ref.shape[1]), BF16)


def _layout_w_in(w_in):
    L, D, W = w_in.shape
    tc = 512
    return pl.pallas_call(
        _w_in_layout_kernel,
        out_shape=jax.ShapeDtypeStruct((L, U_W, D), BF16),
        grid=(L, D // tc),
        in_specs=[pl.BlockSpec((None, W, tc), lambda l, r: (l, 0, r))],
        out_specs=pl.BlockSpec((None, U_W, tc), lambda l, r: (l, 0, r)),
        compiler_params=_cparams(("parallel", "parallel")),
        name="w_in_layout",
    )(jnp.swapaxes(w_in, 1, 2))


def _layout_w_uq(w_uq):
    L = w_uq.shape[0]
    w = w_uq.reshape(L, MLA_Q_RANK, MLA_HEADS, MLA_NOPE + MLA_ROPE)
    w = jnp.pad(w, ((0, 0), (0, 0), (0, 0), (0, MLA_HP - MLA_NOPE - MLA_ROPE)))
    return w.reshape(L, MLA_Q_RANK, MLA_HEADS * MLA_HP).astype(BF16)


def _layout_w_ukv(w_ukv):
    L = w_ukv.shape[0]
    w = w_ukv.reshape(L, MLA_KV_RANK, MLA_HEADS, MLA_NOPE + MLA_V)
    wk = w[..., :MLA_NOPE].reshape(L, MLA_KV_RANK, MLA_W).astype(BF16)
    wvt = w[..., MLA_NOPE:].reshape(L, MLA_KV_RANK, MLA_W).transpose(0, 2, 1).astype(BF16)
    return wk, wvt


def _layout_lru_gates(w_r, b_r, w_i, b_i):
    L = w_r.shape[0]
    eye = jnp.eye(LRU_BLOCKS, dtype=w_r.dtype)
    ncb = LRU_W // LRU_CB

    def dense(w):
        return jnp.einsum("ldgij,gh->ldgihj", w, eye).reshape(L, 2, LRU_W, LRU_W)

    def diag_blocks(w):
        w = w.reshape(L, 2, ncb, LRU_CB, ncb, LRU_CB)
        return jnp.stack([w[:, :, c, :, c, :] for c in range(ncb)], axis=2)

    wg = jnp.concatenate([diag_blocks(dense(w_r)), diag_blocks(dense(w_i))], axis=-1).astype(BF16)
    bg = jnp.concatenate([b_r.reshape(L, 2, ncb, 1, LRU_CB), b_i.reshape(L, 2, ncb, 1, LRU_CB)], axis=-1)
    return 0.5 * wg, 0.5 * bg


def kernel(x, c, ctx, c_ctx, w_ada, b_ada, w_in, mla_q_norm_g, mla_kv_norm_g, mla_w_uq, mla_w_ukv,
           lru_conv_w, lru_conv_b, lru_w_r, lru_b_r, lru_w_i, lru_b_i, lru_lambda, ret_decay,
           w_out, ln_g, ln_b):
    B, T, D = x.shape
    L = w_in.shape[0]
    n_ctx = ctx.shape[1]
    S = n_ctx + T
    assert D == 2 * MLA_W and w_in.shape[2] == 4928 and n_ctx % 256 == 0 and T % 256 == 0
    alpha = (2 * L) ** 0.25
    tm = 768 if S % 768 == 0 else 256

    cos_m, sin_m = _rope_tables(T // GRID_W, MLA_ROPE, n_ctx)
    cos_r, sin_r = _rope_tables(T // GRID_W, RET_DH, n_ctx)

    cond = jnp.zeros((8, D), F32).at[:B].set(c).at[B].set(c_ctx)
    mod_all = _ada_call(cond, w_ada, b_ada)
    mod_all = mod_all[:, :B + 1].reshape(L, B + 1, 3, D)
    mod_all = jnp.pad(mod_all, ((0, 0), (0, 0), (0, 5), (0, 0)))

    h = _ln0_call(ctx, x)
    log_g = jax.nn.log_sigmoid(ret_decay.astype(F32)).reshape(2 * L, RET_HEADS)

    w_in_l = _layout_w_in(w_in)
    wuq = _layout_w_uq(mla_w_uq)
    wk, wvt = _layout_w_ukv(mla_w_ukv)
    wg, bg = _layout_lru_gates(lru_w_r, lru_b_r, lru_w_i, lru_b_i)
    w_out_b = w_out.astype(BF16)
    gq, gkv = mla_q_norm_g[:, None, :], mla_kv_norm_g[:, None, :]
    conv_b, lng, lnb = lru_conv_b[:, None, :], ln_g[:, None, :], ln_b[:, None, :]

    for l in range(L):
        last = l == L - 1
        u = _inproj_call(h, mod_all, w_in_l, l, n_ctx, tm)
        q, k, vt = _mla_prep_call(u, gq, gkv, wuq, wk, wvt, l, cos_m, sin_m)
        y_mla = _attn_call(q, k, vt, u, n_ctx)
        y_lru = _lru_call(u, lru_conv_w, conv_b, wg, bg, lru_lambda, l, n_ctx)
        y_ret = _ret_call(u, log_g, l, cos_r, sin_r, n_ctx)
        h = _outproj_call(y_mla, y_lru, y_ret, h, mod_all, w_out_b, lng, lnb, l, n_ctx,
                          256 if last else tm, alpha, last)
    return h
```

```python
import functools

import jax
import jax.numpy as jnp
from jax import lax
from jax.experimental import pallas as pl
from jax.experimental.pallas import tpu as pltpu

F32 = jnp.float32
BF16 = jnp.bfloat16

GRID_W = 64
MLA_V = 128
MLA_NOPE = 128
MLA_ROPE = 64
MLA_HEADS = 8
MLA_W = MLA_HEADS * MLA_V
MLA_Q_RANK = 512
MLA_KV_RANK = 256
MLA_SCALE = (MLA_NOPE + MLA_ROPE) ** -0.5
LOG2_E = 1.4426950408889634
MLA_Q_SCALE = MLA_SCALE * LOG2_E
MLA_HP = 256
LRU_W = 512
LRU_BLOCKS = 8
LRU_BW = LRU_W // LRU_BLOCKS
LRU_CONV = 4
LRU_C = 8.0
LRU_CB = 256
RET_HEADS = 4
RET_DH = 128
RET_W = RET_HEADS * RET_DH
RET_CHUNK = 128
RET_K_SCALE = RET_DH ** -0.5
ROPE_BASE = 10000.0
LN_EPS = 1e-5
RMS_EPS = 1e-6
LANES = 128
BF16_ROWS = 16
KEY_CHUNK = 768
ATTN_BUFS = 3
SUB_ROWS = 256

U_QLAT = 0
U_LRU_X = 512
U_LRU_G = 1024
U_RET_Q = 1536
U_RET_K = 2048
U_RET_V = 2560
U_RET_G = 3072
U_MLA_G = 3584
U_KVLAT = 4608
U_KR = 4864
U_W = 4992
U_CHUNKS = 3
U_TN = U_W // U_CHUNKS

VMEM_LIMIT = 56 * 1024 * 1024


def _cparams(sem):
    return pltpu.CompilerParams(dimension_semantics=sem, vmem_limit_bytes=VMEM_LIMIT)


def _dot_nt(a, b):
    return lax.dot_general(a, b, (((1,), (1,)), ((), ())), preferred_element_type=F32)


def _sigmoid(x):
    return 0.5 * jnp.tanh(0.5 * x) + 0.5


def _silu(x):
    return x * _sigmoid(x)


def _softplus(x):
    return jnp.maximum(x, 0.0) + jnp.log1p(jnp.exp(-jnp.abs(x)))


def _rope(x, cos, sin):
    return x * cos + pltpu.roll(x, LANES // 2, axis=x.ndim - 1) * sin


def _ada_kernel(cond_ref, w_ref, b_ref, o_ref):
    c = cond_ref[...]
    a = _silu(c).astype(BF16)
    o_ref[...] = jnp.dot(a, w_ref[...].astype(BF16), preferred_element_type=F32) + b_ref[...]


def _ada_call(cond, w_ada, b_ada):
    L, D, D3 = w_ada.shape
    tn = 1024
    return pl.pallas_call(
        _ada_kernel,
        out_shape=jax.ShapeDtypeStruct((L, 8, D3), F32),
        grid=(L, D3 // tn),
        in_specs=[
            pl.BlockSpec((8, D), lambda l, n: (0, 0)),
            pl.BlockSpec((None, D, tn), lambda l, n: (l, 0, n)),
            pl.BlockSpec((None, 1, tn), lambda l, n: (l, 0, n)),
        ],
        out_specs=pl.BlockSpec((None, 8, tn), lambda l, n: (l, 0, n)),
        compiler_params=_cparams(("parallel", "parallel")),
        name="ada_mod",
    )(cond, w_ada, b_ada.reshape(L, 1, D3))


def _ln0_kernel(c_ref, x_ref, o_ref, *, n_ctx_blocks):
    def norm(ref):
        x = ref[...]
        mu = jnp.mean(x, axis=-1, keepdims=True)
        xc = x - mu
        var = jnp.mean(xc * xc, axis=-1, keepdims=True)
        o_ref[...] = xc * lax.rsqrt(var + LN_EPS)

    j = pl.program_id(1)
    pl.when(j < n_ctx_blocks)(lambda: norm(c_ref))
    pl.when(j >= n_ctx_blocks)(lambda: norm(x_ref))


def _ln0_call(ctx, x):
    B, T, D = x.shape
    n_ctx = ctx.shape[1]
    tm = 256
    nc = n_ctx // tm
    return pl.pallas_call(
        functools.partial(_ln0_kernel, n_ctx_blocks=nc),
        out_shape=jax.ShapeDtypeStruct((B, n_ctx + T, D), F32),
        grid=(B, (n_ctx + T) // tm),
        in_specs=[pl.BlockSpec((None, tm, D), lambda b, j: (b, jnp.minimum(j, nc - 1), 0)),
                  pl.BlockSpec((None, tm, D), lambda b, j: (b, jnp.maximum(j - nc, 0), 0))],
        out_specs=pl.BlockSpec((None, tm, D), lambda b, j: (b, j, 0)),
        compiler_params=_cparams(("parallel", "parallel")),
        name="ln_entry",
    )(ctx, x)


def _inproj_kernel(h_ref, mod_ref, modc_ref, *rest, tm, ctx):
    w_refs, (u_ref, xs_ref) = rest[:U_CHUNKS], rest[U_CHUNKS:]
    j = pl.program_id(1)
    n = pl.program_id(2)

    @pl.when(n == 0)
    def _():
        for r0 in range(0, tm, SUB_ROWS):
            rs = slice(r0, r0 + SUB_ROWS)
            rows = j * tm + r0 + lax.broadcasted_iota(jnp.int32, (SUB_ROWS, 1), 0)
            is_ctx = rows < ctx
            sh = jnp.where(is_ctx, modc_ref[0:1, :], mod_ref[0:1, :])
            sc = jnp.where(is_ctx, modc_ref[1:2, :], mod_ref[1:2, :])
            xs = (h_ref[rs, :] * (1.0 + sc) + sh).astype(BF16)
            xs_ref[rs, :] = xs
            u_ref[rs, :] = _dot_nt(xs, w_refs[0][...]).astype(u_ref.dtype)

    for c in range(1, U_CHUNKS):
        @pl.when(n == c)
        def _():
            u_ref[...] = _dot_nt(xs_ref[...], w_refs[c][...]).astype(u_ref.dtype)


def _inproj_call(h, mod_all, w_all, l, ctx, tm):
    B, S, D = h.shape
    return pl.pallas_call(
        functools.partial(_inproj_kernel, tm=tm, ctx=ctx),
        out_shape=jax.ShapeDtypeStruct((B, S, U_W), F32),
        grid=(B, S // tm, U_CHUNKS),
        in_specs=[
            pl.BlockSpec((None, tm, D), lambda b, j, n: (b, j, 0)),
            pl.BlockSpec((None, None, 8, D), lambda b, j, n: (l, b, 0, 0)),
            pl.BlockSpec((None, None, 8, D), lambda b, j, n: (l, B, 0, 0)),
        ] + [
            pl.BlockSpec((None, U_TN, D), functools.partial(lambda b, j, n, c: (l, c, 0), c=c),
                         pipeline_mode=pl.Buffered(1))
            for c in range(U_CHUNKS)
        ],
        out_specs=pl.BlockSpec((None, tm, U_TN), lambda b, j, n: (b, j, n)),
        scratch_shapes=[pltpu.VMEM((tm, D), BF16)],
        compiler_params=_cparams(("parallel", "parallel", "arbitrary")),
        name="in_proj",
    )(h, mod_all, mod_all, *([w_all] * U_CHUNKS))


def _rms(x, g):
    return x * lax.rsqrt(jnp.mean(x * x, axis=-1, keepdims=True) + RMS_EPS) * g


def _mla_prep_kernel(ql_ref, kvl_ref, kr_ref, gq_ref, gkv_ref, wuq_ref, wk_ref, wvt_ref,
                     cos_ref, sin_ref, q_ref, k_ref, vt_ref):
    cos = cos_ref[...]
    sin = sin_ref[...]

    zq = _rms(ql_ref[...].astype(F32), gq_ref[...]).astype(BF16)
    q = jnp.dot(zq, wuq_ref[...], preferred_element_type=F32)
    for hd in range(MLA_HEADS):
        c0 = hd * MLA_HP
        q_ref[:, c0:c0 + MLA_NOPE] = (q[:, c0:c0 + MLA_NOPE] * MLA_Q_SCALE).astype(q_ref.dtype)
        qr = _rope(q[:, c0 + MLA_NOPE:c0 + MLA_HP], cos, sin)
        q_ref[:, c0 + MLA_NOPE:c0 + MLA_HP] = (qr * MLA_Q_SCALE).astype(q_ref.dtype)

    zk = _rms(kvl_ref[...].astype(F32), gkv_ref[...]).astype(BF16)
    kn = jnp.dot(zk, wk_ref[...], preferred_element_type=F32)
    kr = _rope(kr_ref[...].astype(F32), cos, sin).astype(k_ref.dtype)
    for hd in range(MLA_HEADS):
        c0 = hd * MLA_HP
        k_ref[:, c0:c0 + MLA_NOPE] = kn[:, hd * MLA_NOPE:(hd + 1) * MLA_NOPE].astype(k_ref.dtype)
        k_ref[:, c0 + MLA_NOPE:c0 + MLA_HP] = kr
    vt = lax.dot_general(wvt_ref[...], zk, (((1,), (1,)), ((), ())), preferred_element_type=F32)
    vt_ref[...] = vt.astype(vt_ref.dtype)


def _mla_prep_call(u, gq, gkv, wuq, wk, wvt, l, cos, sin):
    B, S, _ = u.shape
    tp = 768 if S % 768 == 0 else 256
    HW = MLA_HEADS * MLA_HP
    return pl.pallas_call(
        _mla_prep_kernel,
        out_shape=(
            jax.ShapeDtypeStruct((B, S, HW), BF16),
            jax.ShapeDtypeStruct((B, S, HW), BF16),
            jax.ShapeDtypeStruct((B, MLA_W, S), BF16),
        ),
        grid=(B, S // tp),
        in_specs=[
            pl.BlockSpec((None, tp, MLA_Q_RANK), lambda b, j: (b, j, U_QLAT // MLA_Q_RANK)),
            pl.BlockSpec((None, tp, MLA_KV_RANK), lambda b, j: (b, j, U_KVLAT // MLA_KV_RANK)),
            pl.BlockSpec((None, tp, LANES), lambda b, j: (b, j, U_KR // LANES)),
            pl.BlockSpec((None, 1, MLA_Q_RANK), lambda b, j: (l, 0, 0)),
            pl.BlockSpec((None, 1, MLA_KV_RANK), lambda b, j: (l, 0, 0)),
            pl.BlockSpec((None, MLA_Q_RANK, HW), lambda b, j: (l, 0, 0)),
            pl.BlockSpec((None, MLA_KV_RANK, MLA_W), lambda b, j: (l, 0, 0)),
            pl.BlockSpec((None, MLA_W, MLA_KV_RANK), lambda b, j: (l, 0, 0)),
            pl.BlockSpec((tp, LANES), lambda b, j: (j, 0)),
            pl.BlockSpec((tp, LANES), lambda b, j: (j, 0)),
        ],
        out_specs=(
            pl.BlockSpec((None, tp, HW), lambda b, j: (b, j, 0)),
            pl.BlockSpec((None, tp, HW), lambda b, j: (b, j, 0)),
            pl.BlockSpec((None, MLA_W, tp), lambda b, j: (b, 0, j)),
        ),
        compiler_params=_cparams(("parallel", "parallel")),
        name="mla_prep",
    )(u, u, u, gq, gkv, wuq, wk, wvt, cos, sin)


def _attn_kernel(q_ref, k_ref, vt_ref, g_ref, y_ref, st_ref, m_ref, p_ref, vt1_ref, *, ctx, tq):
    S = k_ref.shape[0]
    n_lat = S // tq - 1
    assert n_lat >= 2 and S % KEY_CHUNK == 0

    vt1_ref[0:MLA_V, :] = vt_ref[...]
    vt1_ref[MLA_V:, :] = jnp.ones((BF16_ROWS, S), BF16)

    def rows_of(blk):
        return slice(blk * tq, (blk + 1) * tq)

    def scores(blk, nk):
        return lax.dot_general(k_ref[0:nk, :], q_ref[rows_of(blk), :], (((1,), (1,)), ((), ())),
                               preferred_element_type=F32)

    def numerators(st, m):
        return jnp.exp2(st - m).astype(BF16)

    def attend(blk, p, nk):
        rows = rows_of(blk)
        ot = jnp.dot(vt1_ref[:, 0:nk], p, preferred_element_type=F32)
        o = (ot[0:MLA_V, :] / ot[MLA_V:MLA_V + 1, :]).T
        y_ref[rows, :] = (o * _silu(g_ref[rows, :].astype(F32))).astype(y_ref.dtype)

    def step(score=None, num=None, att=None):
        sb, nb, ab = (None if b is None else b % ATTN_BUFS for b in (score, num, att))
        qb = None if score is None else q_ref[rows_of(score), :]
        m_num = None if num is None else m_ref[nb]
        m_run, acc = None, None
        for c0 in range(0, S, KEY_CHUNK):
            ks = slice(c0, c0 + KEY_CHUNK)
            if score is not None:
                st = lax.dot_general(k_ref[ks, :], qb, (((1,), (1,)), ((), ())),
                                     preferred_element_type=F32)
                st_ref[sb, ks, :] = st
                m_c = jnp.max(st, axis=0, keepdims=True)
                m_run = m_c if m_run is None else jnp.maximum(m_run, m_c)
            if num is not None:
                p_ref[nb, ks, :] = numerators(st_ref[nb, ks, :], m_num)
            if att is not None:
                part = jnp.dot(vt1_ref[:, ks], p_ref[ab, ks, :], preferred_element_type=F32)
                acc = part if acc is None else acc + part
        if score is not None:
            m_ref[sb] = m_run
        if att is not None:
            rows = rows_of(att)
            o = (acc[0:MLA_V, :] / acc[MLA_V:MLA_V + 1, :]).T
            y_ref[rows, :] = (o * _silu(g_ref[rows, :].astype(F32))).astype(y_ref.dtype)

    def latent(blk):
        return blk if blk <= n_lat else None

    step(score=1)
    step(score=latent(2), num=1)
    st_ctx = scores(0, ctx)
    attend(0, numerators(st_ctx, jnp.max(st_ctx, axis=0, keepdims=True)), ctx)
    for t in range(1, n_lat + 1):
        step(score=latent(t + 2), num=latent(t + 1), att=t)


def _attn_call(q, k, vt, u, ctx):
    B, S, _ = q.shape
    tq = 256
    assert ctx == tq
    gcol = U_MLA_G // MLA_V
    return pl.pallas_call(
        functools.partial(_attn_kernel, ctx=ctx, tq=tq),
        out_shape=jax.ShapeDtypeStruct((B, S, MLA_W), BF16),
        grid=(B, MLA_HEADS),
        in_specs=[
            pl.BlockSpec((None, S, MLA_HP), lambda b, h: (b, 0, h)),
            pl.BlockSpec((None, S, MLA_HP), lambda b, h: (b, 0, h)),
            pl.BlockSpec((None, MLA_V, S), lambda b, h: (b, h, 0)),
            pl.BlockSpec((None, S, MLA_V), lambda b, h: (b, 0, gcol + h)),
        ],
        out_specs=pl.BlockSpec((None, S, MLA_V), lambda b, h: (b, 0, h)),
        scratch_shapes=[pltpu.VMEM((ATTN_BUFS, S, tq), F32), pltpu.VMEM((ATTN_BUFS, 1, tq), F32),
                        pltpu.VMEM((ATTN_BUFS, S, tq), BF16),
                        pltpu.VMEM((MLA_V + BF16_ROWS, S), BF16)],
        compiler_params=_cparams(("parallel", "parallel")),
        name="mla_attn",
    )(q, k, vt, u)


def _lru_kernel(x_ref, g_ref, cw_ref, cb_ref, wg_ref, bg_ref, lam_ref, y_ref,
                af_ref, bf_ref, ab_ref, bb_ref, hf_ref, hb_ref, *, ctx):
    S, CB = x_ref.shape
    x = x_ref[...].astype(F32)
    rows = lax.broadcasted_iota(jnp.int32, (S, 1), 0)
    in_ctx = rows < ctx

    xc = x * cw_ref[2:3, :] + cb_ref[...]
    for tap, off in ((0, -2), (1, -1), (3, 1)):
        xs = pltpu.roll(x, (-off) % S, axis=0)
        src = rows + off
        valid = (src >= 0) & (src < S) & ((src < ctx) == in_ctx)
        xc = xc + jnp.where(valid, xs, 0.0) * cw_ref[tap:tap + 1, :]

    xb = xc.astype(BF16)
    xh = 0.5 * xc
    for d, (a_ref, b_ref) in enumerate(((af_ref, bf_ref), (ab_ref, bb_ref))):
        t = jnp.tanh(jnp.dot(xb, wg_ref[d], preferred_element_type=F32) + bg_ref[d])
        t_r, t_i = t[:, :CB], t[:, CB:]
        c2 = (-0.5 * LRU_C * LOG2_E) * _softplus(-lam_ref[d:d + 1, :])
        a = jnp.exp2(t_r * c2 + c2)
        w = (1.0 - a) * (1.0 + a)
        a_ref[...] = a
        b_ref[...] = jnp.where(w > 0.0, w * lax.rsqrt(w), 0.0) * (t_i * xh + xh)

    R = 8
    n_t, n_ctx_t = S // R, ctx // R
    row = lax.broadcasted_iota(jnp.int32, (R, CB), 0)

    def tile_prefix(a, b, rev):
        for d in (1, 2, 4):
            shift = R - d if rev else d
            keep = (row < R - d) if rev else (row >= d)
            a_s = pltpu.roll(a, shift, axis=0)
            b_s = pltpu.roll(b, shift, axis=0)
            b = jnp.where(keep, a * b_s, 0.0) + b
            a = jnp.where(keep, a * a_s, a)
        return a, b

    def tile_step(i, carry):
        hf, hb = carry
        sf = pl.ds(pl.multiple_of(i * R, R), R)
        ib = jnp.where(i < n_ctx_t, n_ctx_t - 1 - i, n_t - 1 - (i - n_ctx_t))
        sb = pl.ds(pl.multiple_of(ib * R, R), R)
        pa, pb = tile_prefix(af_ref[sf, :], bf_ref[sf, :], False)
        of = pa * hf + pb
        qa, qb = tile_prefix(ab_ref[sb, :], bb_ref[sb, :], True)
        ob = qa * hb + qb
        hf_ref[sf, :] = of
        hb_ref[sb, :] = ob
        return (jnp.broadcast_to(of[R - 1:R, :], (R, CB)), jnp.broadcast_to(ob[0:1, :], (R, CB)))

    zero = jnp.zeros((R, CB), F32)
    lax.fori_loop(0, n_t, tile_step, (zero, zero), unroll=4)

    y = (hf_ref[...] + hb_ref[...]) * _silu(g_ref[...].astype(F32))
    y_ref[...] = y.astype(y_ref.dtype)


def _lru_call(u, cw, cb, wg, bg, lam, l, ctx):
    B, S, _ = u.shape
    ncb = LRU_W // LRU_CB
    return pl.pallas_call(
        functools.partial(_lru_kernel, ctx=ctx),
        out_shape=jax.ShapeDtypeStruct((B, S, LRU_W), BF16),
        grid=(B, ncb),
        in_specs=[
            pl.BlockSpec((None, S, LRU_CB), lambda b, c: (b, 0, U_LRU_X // LRU_CB + c)),
            pl.BlockSpec((None, S, LRU_CB), lambda b, c: (b, 0, U_LRU_G // LRU_CB + c)),
            pl.BlockSpec((None, LRU_CONV, LRU_CB), lambda b, c: (l, 0, c)),
            pl.BlockSpec((None, 1, LRU_CB), lambda b, c: (l, 0, c)),
            pl.BlockSpec((None, 2, None, LRU_CB, 2 * LRU_CB), lambda b, c: (l, 0, c, 0, 0)),
            pl.BlockSpec((None, 2, None, 1, 2 * LRU_CB), lambda b, c: (l, 0, c, 0, 0)),
            pl.BlockSpec((None, 2, LRU_CB), lambda b, c: (l, 0, c)),
        ],
        out_specs=pl.BlockSpec((None, S, LRU_CB), lambda b, c: (b, 0, c)),
        scratch_shapes=[pltpu.VMEM((S, LRU_CB), F32) for _ in range(6)],
        compiler_params=_cparams(("parallel", "parallel")),
        name="rg_lru",
    )(u, u, cw, cb, wg, bg, lam)


def _ret_kernel(lg_ref, q_ref, k_ref, v_ref, g_ref, cos_ref, sin_ref, y_ref, ks_ref, kv_ref, st_ref,
                *, n_ctx, layer):
    C = RET_CHUNK
    S = q_ref.shape[0]
    n_all = S // C
    unroll = next(f for f in (9, 6, 3, 2, 1) if n_all % f == 0)
    hd = pl.program_id(1)
    lgf = lg_ref[2 * layer, hd]
    lgb = lg_ref[2 * layer + 1, hd]

    ri = lax.broadcasted_iota(jnp.int32, (C, C), 0).astype(F32)
    ci = lax.broadcasted_iota(jnp.int32, (C, C), 1).astype(F32)
    diff = ri - ci
    decay = jnp.where(diff >= 0, jnp.exp(jnp.maximum(diff, 0.0) * lgf),
                      jnp.exp(jnp.maximum(-diff, 0.0) * lgb))
    zeta_f = jnp.exp((C - 1 - ri) * lgf)
    xi_f = jnp.exp((ri + 1) * lgf)
    gc_f = jnp.exp(jnp.full((C, C), C, F32) * lgf)
    zeta_b = jnp.exp(ri * lgb)
    xi_b = jnp.exp((C - ri) * lgb)
    gc_b = jnp.exp(jnp.full((C, C), C, F32) * lgb)
    def rows_of(c):
        return pl.ds(pl.multiple_of(c * C, C), C)

    def chunk_kv(c, carry):
        sl = rows_of(c)
        k = _rope(k_ref[sl, :].astype(F32), cos_ref[sl, :], sin_ref[sl, :]) * RET_K_SCALE
        ks_ref[sl, :] = k.astype(BF16)
        kz = jnp.concatenate([k * zeta_f, k * zeta_b], axis=1).astype(BF16)
        kv_ref[c] = lax.dot_general(v_ref[sl, :].astype(BF16), kz, (((0,), (0,)), ((), ())),
                                    preferred_element_type=F32)
        return carry

    lax.fori_loop(0, n_all, chunk_kv, 0, unroll=unroll)

    def fwd_state(c, r):
        st_ref[c, :, 0:C] = r.astype(BF16)
        return gc_f * r + kv_ref[c, :, 0:C]

    def bwd_state(i, r):
        c = jnp.where(i < n_ctx, n_ctx - 1 - i, n_all - 1 - (i - n_ctx))
        st_ref[c, :, C:2 * C] = r.astype(BF16)
        return gc_b * r + kv_ref[c, :, C:2 * C]

    lax.fori_loop(0, n_all, fwd_state, jnp.zeros((C, C), F32))
    lax.fori_loop(0, n_all, bwd_state, jnp.zeros((C, C), F32))

    group = next(f for f in (6, 3, 2, 1) if n_all % f == 0)

    def chunks_out(i, carry):
        cs = [i * group + g for g in range(group)]
        sls = [rows_of(c) for c in cs]
        qs = [_rope(q_ref[sl, :].astype(F32), cos_ref[sl, :], sin_ref[sl, :]) for sl in sls]
        ss = [_dot_nt(q.astype(BF16), ks_ref[sl, :]) for q, sl in zip(qs, sls)]
        qxs = [jnp.concatenate([q * xi_f, q * xi_b], axis=1).astype(BF16) for q in qs]
        os = [jnp.dot((s * decay).astype(BF16), v_ref[sl, :].astype(BF16), preferred_element_type=F32)
              + _dot_nt(qx, st_ref[c]) for s, sl, qx, c in zip(ss, sls, qxs, cs)]
        for o, sl in zip(os, sls):
            mu = jnp.mean(o, axis=-1, keepdims=True)
            oc = o - mu
            var = jnp.mean(oc * oc, axis=-1, keepdims=True)
            y = oc * lax.rsqrt(var + LN_EPS) * _silu(g_ref[sl, :].astype(F32))
            y_ref[sl, :] = y.astype(y_ref.dtype)
        return carry

    lax.fori_loop(0, n_all // group, chunks_out, 0)


def _ret_call(u, log_g, l, cos, sin, ctx):
    B, S, _ = u.shape
    n_all = S // RET_CHUNK

    def col(base):
        return lambda b, h: (b, 0, base // RET_DH + h)

    return pl.pallas_call(
        functools.partial(_ret_kernel, n_ctx=ctx // RET_CHUNK, layer=l),
        out_shape=jax.ShapeDtypeStruct((B, S, RET_W), BF16),
        grid=(B, RET_HEADS),
        in_specs=[
            pl.BlockSpec(memory_space=pltpu.SMEM),
            pl.BlockSpec((None, S, RET_DH), col(U_RET_Q)),
            pl.BlockSpec((None, S, RET_DH), col(U_RET_K)),
            pl.BlockSpec((None, S, RET_DH), col(U_RET_V)),
            pl.BlockSpec((None, S, RET_DH), col(U_RET_G)),
            pl.BlockSpec((S, RET_DH), lambda b, h: (0, 0)),
            pl.BlockSpec((S, RET_DH), lambda b, h: (0, 0)),
        ],
        out_specs=pl.BlockSpec((None, S, RET_DH), lambda b, h: (b, 0, h)),
        scratch_shapes=[pltpu.VMEM((S, RET_DH), BF16),
                        pltpu.VMEM((n_all, RET_DH, 2 * RET_DH), F32),
                        pltpu.VMEM((n_all, RET_DH, 2 * RET_DH), BF16)],
        compiler_params=_cparams(("parallel", "parallel")),
        name="retention",
    )(log_g, u, u, u, u, cos, sin)


def _outproj_kernel(ym_ref, yl_ref, yr_ref, h_ref, mod_ref, modc_ref, w_ref, lng_ref, lnb_ref, o_ref,
                    *, tm, ctx, alpha, skip):
    j = pl.program_id(1) + skip

    def project(r0):
        rs = slice(r0, r0 + SUB_ROWS)
        y = jnp.concatenate([ym_ref[rs, :], yl_ref[rs, :], yr_ref[rs, :]], axis=1)
        return jnp.dot(y, w_ref[...], preferred_element_type=F32)

    starts = list(range(0, tm, SUB_ROWS))
    acc_next = project(starts[0])
    for i, r0 in enumerate(starts):
        acc = acc_next
        if i + 1 < len(starts):
            acc_next = project(starts[i + 1])
        rs = slice(r0, r0 + SUB_ROWS)
        rows = j * tm + r0 + lax.broadcasted_iota(jnp.int32, (SUB_ROWS, 1), 0)
        gt = jnp.where(rows < ctx, modc_ref[2:3, :], mod_ref[2:3, :])
        z = alpha * h_ref[rs, :] + gt * acc
        mu = jnp.mean(z, axis=-1, keepdims=True)
        zc = z - mu
        var = jnp.mean(zc * zc, axis=-1, keepdims=True)
        o_ref[rs, :] = zc * lax.rsqrt(var + LN_EPS) * lng_ref[...] + lnb_ref[...]


def _outproj_call(ym, yl, yr, h, mod, w, lng, lnb, l, ctx, tm, alpha, latent_only):
    B, S, D = h.shape
    skip = ctx // tm if latent_only else 0
    assert skip * tm == (ctx if latent_only else 0)
    n_out = S // tm - skip
    return pl.pallas_call(
        functools.partial(_outproj_kernel, tm=tm, ctx=ctx, alpha=alpha, skip=skip),
        out_shape=jax.ShapeDtypeStruct((B, n_out * tm, D), F32),
        grid=(B, n_out),
        in_specs=[
            pl.BlockSpec((None, tm, MLA_W), lambda b, j: (b, j + skip, 0)),
            pl.BlockSpec((None, tm, LRU_W), lambda b, j: (b, j + skip, 0)),
            pl.BlockSpec((None, tm, RET_W), lambda b, j: (b, j + skip, 0)),
            pl.BlockSpec((None, tm, D), lambda b, j: (b, j + skip, 0)),
            pl.BlockSpec((None, None, 8, D), lambda b, j: (l, b, 0, 0)),
            pl.BlockSpec((None, None, 8, D), lambda b, j: (l, B, 0, 0)),
            pl.BlockSpec((None, D, D), lambda b, j: (l, 0, 0), pipeline_mode=pl.Buffered(1)),
            pl.BlockSpec((None, 1, D), lambda b, j: (l, 0, 0)),
            pl.BlockSpec((None, 1, D), lambda b, j: (l, 0, 0)),
        ],
        out_specs=pl.BlockSpec((None, tm, D), lambda b, j: (b, j, 0)),
        compiler_params=_cparams(("parallel", "parallel")),
        name="out_proj_ln",
    )(ym, yl, yr, h, mod, mod, w, lng, lnb)


def _rope_tables(rows, dim, ctx):
    row = jnp.repeat(jnp.arange(rows, dtype=F32), GRID_W)
    col = jnp.tile(jnp.arange(GRID_W, dtype=F32), rows)
    quarter = dim // 4
    inv = ROPE_BASE ** (-jnp.arange(quarter, dtype=F32) / quarter)
    ang = jnp.stack([row[:, None] * inv, col[:, None] * inv], axis=1)
    cos, sin = jnp.cos(ang), jnp.sin(ang)
    pad = ((0, 0), (0, LANES // 2 - 2 * quarter))
    cos_h = jnp.pad(jnp.concatenate([cos[:, 0], cos[:, 1]], axis=-1), pad)
    sin_h = jnp.pad(jnp.concatenate([sin[:, 0], sin[:, 1]], axis=-1), pad)
    cos_l = jnp.concatenate([cos_h, cos_h], axis=-1)
    sin_l = jnp.concatenate([-sin_h, sin_h], axis=-1)
    cos_c = jnp.ones((ctx, LANES), F32)
    sin_c = jnp.zeros((ctx, LANES), F32)
    return jnp.concatenate([cos_c, cos_l], axis=0), jnp.concatenate([sin_c, sin_l], axis=0)


def _rotary_lane_order(dim):
    quarter = dim // 4
    return ((0, 0), (quarter, LANES // 2), (2 * quarter, quarter), (3 * quarter, LANES // 2 + quarter))


def _rotary_moves(src, dst, dim, groups):
    return tuple((src + g * dim + s, dst + g * LANES + d, dim // 4)
                 for g in range(groups) for s, d in _rotary_lane_order(dim))


W_IN_MOVES = ((0, U_QLAT, 512), (512, U_KVLAT, 256), (832, U_MLA_G, 1024),
              (1856, U_LRU_X, 512), (2368, U_LRU_G, 512), (3904, U_RET_V, 512), (4416, U_RET_G, 512)
              ) + _rotary_moves(768, U_KR, MLA_ROPE, 1) + _rotary_moves(2880, U_RET_Q, RET_DH, RET_HEADS
              ) + _rotary_moves(3392, U_RET_K, RET_DH, RET_HEADS)
W_IN_ZERO = tuple((U_KR + o, LANES // 2 - MLA_ROPE // 2) for o in (MLA_ROPE // 2, LANES // 2 + MLA_ROPE // 2))


def _w_in_layout_kernel(w_ref, o_ref):
    for src, dst, width in W_IN_MOVES:
        o_ref[dst:dst + width, :] = w_ref[src:src + width, :].astype(BF16)
    for dst, width in W_IN_ZERO:
        o_ref[dst:dst + width, :] = jnp.zeros((width, o_ref.shape[1]), BF16)


def _layout_w_in(w_in):
    L, D, W = w_in.shape
    tc = 512
    return pl.pallas_call(
        _w_in_layout_kernel,
        out_shape=jax.ShapeDtypeStruct((L, U_W, D), BF16),
        grid=(L, D // tc),
        in_specs=[pl.BlockSpec((None, W, tc), lambda l, r: (l, 0, r))],
        out_specs=pl.BlockSpec((None, U_W, tc), lambda l, r: (l, 0, r)),
        compiler_params=_cparams(("parallel", "parallel")),
        name="w_in_layout",
    )(jnp.swapaxes(w_in, 1, 2))


def _layout_w_uq(w_uq):
    L = w_uq.shape[0]
    w = w_uq.reshape(L, MLA_Q_RANK, MLA_HEADS, MLA_NOPE + MLA_ROPE)
    quarter = MLA_ROPE // 4
    rope = jnp.zeros((L, MLA_Q_RANK, MLA_HEADS, LANES), w.dtype)
    for s, d in _rotary_lane_order(MLA_ROPE):
        rope = rope.at[..., d:d + quarter].set(w[..., MLA_NOPE + s:MLA_NOPE + s + quarter])
    w = jnp.concatenate([w[..., :MLA_NOPE], rope], axis=-1)
    return w.reshape(L, MLA_Q_RANK, MLA_HEADS * MLA_HP).astype(BF16)


def _layout_w_ukv(w_ukv):
    L = w_ukv.shape[0]
    w = w_ukv.reshape(L, MLA_KV_RANK, MLA_HEADS, MLA_NOPE + MLA_V)
    wk = w[..., :MLA_NOPE].reshape(L, MLA_KV_RANK, MLA_W).astype(BF16)
    wvt = w[..., MLA_NOPE:].reshape(L, MLA_KV_RANK, MLA_W).transpose(0, 2, 1).astype(BF16)
    return wk, wvt


def _layout_lru_gates(w_r, b_r, w_i, b_i):
    L = w_r.shape[0]
    eye = jnp.eye(LRU_BLOCKS, dtype=w_r.dtype)
    ncb = LRU_W // LRU_CB

    def dense(w):
        return jnp.einsum("ldgij,gh->ldgihj", w, eye).reshape(L, 2, LRU_W, LRU_W)

    def diag_blocks(w):
        w = w.reshape(L, 2, ncb, LRU_CB, ncb, LRU_CB)
        return jnp.stack([w[:, :, c, :, c, :] for c in range(ncb)], axis=2)

    wg = jnp.concatenate([diag_blocks(dense(w_r)), diag_blocks(dense(w_i))], axis=-1).astype(BF16)
    bg = jnp.concatenate([b_r.reshape(L, 2, ncb, 1, LRU_CB), b_i.reshape(L, 2, ncb, 1, LRU_CB)], axis=-1)
    return 0.5 * wg, 0.5 * bg


def kernel(x, c, ctx, c_ctx, w_ada, b_ada, w_in, mla_q_norm_g, mla_kv_norm_g, mla_w_uq, mla_w_ukv,
           lru_conv_w, lru_conv_b, lru_w_r, lru_b_r, lru_w_i, lru_b_i, lru_lambda, ret_decay,
           w_out, ln_g, ln_b):
    B, T, D = x.shape
    L = w_in.shape[0]
    n_ctx = ctx.shape[1]
    S = n_ctx + T
    assert D == 2 * MLA_W and w_in.shape[2] == 4928 and n_ctx % 256 == 0 and T % 256 == 0
    alpha = (2 * L) ** 0.25
    tm = 768 if S % 768 == 0 else 256

    cos_m, sin_m = _rope_tables(T // GRID_W, MLA_ROPE, n_ctx)
    cos_r, sin_r = _rope_tables(T // GRID_W, RET_DH, n_ctx)

    cond = jnp.zeros((8, D), F32).at[:B].set(c).at[B].set(c_ctx)
    mod_all = _ada_call(cond, w_ada, b_ada)
    mod_all = mod_all[:, :B + 1].reshape(L, B + 1, 3, D)
    mod_all = jnp.pad(mod_all, ((0, 0), (0, 0), (0, 5), (0, 0)))

    h = _ln0_call(ctx, x)
    log_g = jax.nn.log_sigmoid(ret_decay.astype(F32)).reshape(2 * L, RET_HEADS)

    w_in_l = _layout_w_in(w_in)
    wuq = _layout_w_uq(mla_w_uq)
    wk, wvt = _layout_w_ukv(mla_w_ukv)
    wg, bg = _layout_lru_gates(lru_w_r, lru_b_r, lru_w_i, lru_b_i)
    w_out_b = w_out.astype(BF16)
    gq, gkv = mla_q_norm_g[:, None, :], mla_kv_norm_g[:, None, :]
    conv_b, lng, lnb = lru_conv_b[:, None, :], ln_g[:, None, :], ln_b[:, None, :]

    for l in range(L):
        last = l == L - 1
        u = _inproj_call(h, mod_all, w_in_l, l, n_ctx, tm)
        q, k, vt = _mla_prep_call(u, gq, gkv, wuq, wk, wvt, l, cos_m, sin_m)
        y_mla = _attn_call(q, k, vt, u, n_ctx)
        y_lru = _lru_call(u, lru_conv_w, conv_b, wg, bg, lru_lambda, l, n_ctx)
        y_ret = _ret_call(u, log_g, l, cos_r, sin_r, n_ctx)
        h = _outproj_call(y_mla, y_lru, y_ret, h, mod_all, w_out_b, lng, lnb, l, n_ctx,
                          256 if last else tm, alpha, last)
    return h
```

```python
import functools

import jax
import jax.numpy as jnp
from jax import lax
from jax.experimental import pallas as pl
from jax.experimental.pallas import tpu as pltpu

F32 = jnp.float32
BF16 = jnp.bfloat16

GRID_W = 64
MLA_V = 128
MLA_NOPE = 128
MLA_ROPE = 64
MLA_HEADS = 8
MLA_W = MLA_HEADS * MLA_V
MLA_Q_RANK = 512
MLA_KV_RANK = 256
MLA_SCALE = (MLA_NOPE + MLA_ROPE) ** -0.5
LOG2_E = 1.4426950408889634
MLA_Q_SCALE = MLA_SCALE * LOG2_E
MLA_HP = 256
LRU_W = 512
LRU_BLOCKS = 8
LRU_BW = LRU_W // LRU_BLOCKS
LRU_CONV = 4
LRU_C = 8.0
LRU_CB = 256
RET_HEADS = 4
RET_DH = 128
RET_W = RET_HEADS * RET_DH
RET_CHUNK = 128
RET_K_SCALE = RET_DH ** -0.5
ROPE_BASE = 10000.0
LN_EPS = 1e-5
RMS_EPS = 1e-6
LANES = 128
BF16_ROWS = 16
KEY_CHUNK = 768
ATTN_BUFS = 3
SUB_ROWS = 256

U_QLAT = 0
U_LRU_X = 512
U_LRU_G = 1024
U_RET_Q = 1536
U_RET_K = 2048
U_RET_V = 2560
U_RET_G = 3072
U_MLA_G = 3584
U_KVLAT = 4608
U_KR = 4864
U_W = 4992
U_CHUNKS = 3
U_TN = U_W // U_CHUNKS

VMEM_LIMIT = 56 * 1024 * 1024


def _cparams(sem):
    return pltpu.CompilerParams(dimension_semantics=sem, vmem_limit_bytes=VMEM_LIMIT)


def _dot_nt(a, b):
    return lax.dot_general(a, b, (((1,), (1,)), ((), ())), preferred_element_type=F32)


def _sigmoid(x):
    return 0.5 * jnp.tanh(0.5 * x) + 0.5


def _silu(x):
    return x * _sigmoid(x)


def _softplus(x):
    return jnp.maximum(x, 0.0) + jnp.log1p(jnp.exp(-jnp.abs(x)))


def _rope(x, cos, sin):
    return x * cos + pltpu.roll(x, LANES // 2, axis=x.ndim - 1) * sin


def _ada_kernel(cond_ref, w_ref, b_ref, o_ref):
    c = cond_ref[...]
    a = _silu(c).astype(BF16)
    o_ref[...] = jnp.dot(a, w_ref[...].astype(BF16), preferred_element_type=F32) + b_ref[...]


def _ada_call(cond, w_ada, b_ada):
    L, D, D3 = w_ada.shape
    tn = 1024
    return pl.pallas_call(
        _ada_kernel,
        out_shape=jax.ShapeDtypeStruct((L, 8, D3), F32),
        grid=(L, D3 // tn),
        in_specs=[
            pl.BlockSpec((8, D), lambda l, n: (0, 0)),
            pl.BlockSpec((None, D, tn), lambda l, n: (l, 0, n)),
            pl.BlockSpec((None, 1, tn), lambda l, n: (l, 0, n)),
        ],
        out_specs=pl.BlockSpec((None, 8, tn), lambda l, n: (l, 0, n)),
        compiler_params=_cparams(("parallel", "parallel")),
        name="ada_mod",
    )(cond, w_ada, b_ada.reshape(L, 1, D3))


def _ln0_kernel(c_ref, x_ref, o_ref, *, n_ctx_blocks):
    def norm(ref):
        x = ref[...]
        mu = jnp.mean(x, axis=-1, keepdims=True)
        xc = x - mu
        var = jnp.mean(xc * xc, axis=-1, keepdims=True)
        o_ref[...] = xc * lax.rsqrt(var + LN_EPS)

    j = pl.program_id(1)
    pl.when(j < n_ctx_blocks)(lambda: norm(c_ref))
    pl.when(j >= n_ctx_blocks)(lambda: norm(x_ref))


def _ln0_call(ctx, x):
    B, T, D = x.shape
    n_ctx = ctx.shape[1]
    tm = 256
    nc = n_ctx // tm
    return pl.pallas_call(
        functools.partial(_ln0_kernel, n_ctx_blocks=nc),
        out_shape=jax.ShapeDtypeStruct((B, n_ctx + T, D), F32),
        grid=(B, (n_ctx + T) // tm),
        in_specs=[pl.BlockSpec((None, tm, D), lambda b, j: (b, jnp.minimum(j, nc - 1), 0)),
                  pl.BlockSpec((None, tm, D), lambda b, j: (b, jnp.maximum(j - nc, 0), 0))],
        out_specs=pl.BlockSpec((None, tm, D), lambda b, j: (b, j, 0)),
        compiler_params=_cparams(("parallel", "parallel")),
        name="ln_entry",
    )(ctx, x)


def _inproj_kernel(h_ref, sh_ref, sc_ref, *rest, tm, ctx, ctx_row):
    w_refs, (u_ref, xs_ref) = rest[:U_CHUNKS], rest[U_CHUNKS:]
    b = pl.program_id(0)
    j = pl.program_id(1)
    n = pl.program_id(2)

    @pl.when(n == 0)
    def _():
        for r0 in range(0, tm, SUB_ROWS):
            rs = slice(r0, r0 + SUB_ROWS)
            rows = j * tm + r0 + lax.broadcasted_iota(jnp.int32, (SUB_ROWS, 1), 0)
            is_ctx = rows < ctx
            sh = jnp.where(is_ctx, sh_ref[ctx_row:ctx_row + 1, :], sh_ref[pl.ds(b, 1), :])
            sc = jnp.where(is_ctx, sc_ref[ctx_row:ctx_row + 1, :], sc_ref[pl.ds(b, 1), :])
            xs = (h_ref[rs, :] * (1.0 + sc) + sh).astype(BF16)
            xs_ref[rs, :] = xs
            u_ref[rs, :] = _dot_nt(xs, w_refs[0][...]).astype(u_ref.dtype)

    for c in range(1, U_CHUNKS):
        @pl.when(n == c)
        def _():
            u_ref[...] = _dot_nt(xs_ref[...], w_refs[c][...]).astype(u_ref.dtype)


def _inproj_call(h, mod_all, w_all, l, ctx, tm):
    B, S, D = h.shape
    return pl.pallas_call(
        functools.partial(_inproj_kernel, tm=tm, ctx=ctx, ctx_row=B),
        out_shape=jax.ShapeDtypeStruct((B, S, U_W), F32),
        grid=(B, S // tm, U_CHUNKS),
        in_specs=[
            pl.BlockSpec((None, tm, D), lambda b, j, n: (b, j, 0)),
            pl.BlockSpec((None, 8, D), lambda b, j, n: (l, 0, 0)),
            pl.BlockSpec((None, 8, D), lambda b, j, n: (l, 0, 1)),
        ] + [
            pl.BlockSpec((None, U_TN, D), functools.partial(lambda b, j, n, c: (l, c, 0), c=c),
                         pipeline_mode=pl.Buffered(1))
            for c in range(U_CHUNKS)
        ],
        out_specs=pl.BlockSpec((None, tm, U_TN), lambda b, j, n: (b, j, n)),
        scratch_shapes=[pltpu.VMEM((tm, D), BF16)],
        compiler_params=_cparams(("parallel", "parallel", "arbitrary")),
        name="in_proj",
    )(h, mod_all, mod_all, *([w_all] * U_CHUNKS))


def _rms(x, g):
    return x * lax.rsqrt(jnp.mean(x * x, axis=-1, keepdims=True) + RMS_EPS) * g


def _mla_prep_kernel(ql_ref, kvl_ref, kr_ref, gq_ref, gkv_ref, wuq_ref, wk_ref, wvt_ref,
                     cos_ref, sin_ref, q_ref, k_ref, vt_ref):
    cos = cos_ref[...]
    sin = sin_ref[...]

    zq = _rms(ql_ref[...].astype(F32), gq_ref[...]).astype(BF16)
    q = jnp.dot(zq, wuq_ref[...], preferred_element_type=F32)
    for hd in range(MLA_HEADS):
        c0 = hd * MLA_HP
        q_ref[:, c0:c0 + MLA_NOPE] = (q[:, c0:c0 + MLA_NOPE] * MLA_Q_SCALE).astype(q_ref.dtype)
        qr = _rope(q[:, c0 + MLA_NOPE:c0 + MLA_HP], cos, sin)
        q_ref[:, c0 + MLA_NOPE:c0 + MLA_HP] = (qr * MLA_Q_SCALE).astype(q_ref.dtype)

    zk = _rms(kvl_ref[...].astype(F32), gkv_ref[...]).astype(BF16)
    kn = jnp.dot(zk, wk_ref[...], preferred_element_type=F32)
    kr = _rope(kr_ref[...].astype(F32), cos, sin).astype(k_ref.dtype)
    for hd in range(MLA_HEADS):
        c0 = hd * MLA_HP
        k_ref[:, c0:c0 + MLA_NOPE] = kn[:, hd * MLA_NOPE:(hd + 1) * MLA_NOPE].astype(k_ref.dtype)
        k_ref[:, c0 + MLA_NOPE:c0 + MLA_HP] = kr
    vt = lax.dot_general(wvt_ref[...], zk, (((1,), (1,)), ((), ())), preferred_element_type=F32)
    vt_ref[...] = vt.astype(vt_ref.dtype)


def _mla_prep_call(u, gq, gkv, wuq, wk, wvt, l, cos, sin):
    B, S, _ = u.shape
    tp = 768 if S % 768 == 0 else 256
    HW = MLA_HEADS * MLA_HP
    return pl.pallas_call(
        _mla_prep_kernel,
        out_shape=(
            jax.ShapeDtypeStruct((B, S, HW), BF16),
            jax.ShapeDtypeStruct((B, S, HW), BF16),
            jax.ShapeDtypeStruct((B, MLA_W, S), BF16),
        ),
        grid=(B, S // tp),
        in_specs=[
            pl.BlockSpec((None, tp, MLA_Q_RANK), lambda b, j: (b, j, U_QLAT // MLA_Q_RANK)),
            pl.BlockSpec((None, tp, MLA_KV_RANK), lambda b, j: (b, j, U_KVLAT // MLA_KV_RANK)),
            pl.BlockSpec((None, tp, LANES), lambda b, j: (b, j, U_KR // LANES)),
            pl.BlockSpec((None, 1, MLA_Q_RANK), lambda b, j: (l, 0, 0)),
            pl.BlockSpec((None, 1, MLA_KV_RANK), lambda b, j: (l, 0, 0)),
            pl.BlockSpec((None, MLA_Q_RANK, HW), lambda b, j: (l, 0, 0)),
            pl.BlockSpec((None, MLA_KV_RANK, MLA_W), lambda b, j: (l, 0, 0)),
            pl.BlockSpec((None, MLA_W, MLA_KV_RANK), lambda b, j: (l, 0, 0)),
            pl.BlockSpec((tp, LANES), lambda b, j: (j, 0)),
            pl.BlockSpec((tp, LANES), lambda b, j: (j, 0)),
        ],
        out_specs=(
            pl.BlockSpec((None, tp, HW), lambda b, j: (b, j, 0)),
            pl.BlockSpec((None, tp, HW), lambda b, j: (b, j, 0)),
            pl.BlockSpec((None, MLA_W, tp), lambda b, j: (b, 0, j)),
        ),
        compiler_params=_cparams(("parallel", "parallel")),
        name="mla_prep",
    )(u, u, u, gq, gkv, wuq, wk, wvt, cos, sin)


def _attn_kernel(q_ref, k_ref, vt_ref, g_ref, y_ref, st_ref, m_ref, p_ref, vt1_ref, *, ctx, tq):
    S = k_ref.shape[0]
    n_lat = S // tq - 1
    assert n_lat >= 2 and S % KEY_CHUNK == 0

    vt1_ref[0:MLA_V, :] = vt_ref[...]
    vt1_ref[MLA_V:, :] = jnp.ones((BF16_ROWS, S), BF16)

    def rows_of(blk):
        return slice(blk * tq, (blk + 1) * tq)

    def scores(blk, nk):
        return lax.dot_general(k_ref[0:nk, :], q_ref[rows_of(blk), :], (((1,), (1,)), ((), ())),
                               preferred_element_type=F32)

    def numerators(st, m):
        return jnp.exp2(st - m).astype(BF16)

    def attend(blk, p, nk):
        rows = rows_of(blk)
        ot = jnp.dot(vt1_ref[:, 0:nk], p, preferred_element_type=F32)
        o = (ot[0:MLA_V, :] / ot[MLA_V:MLA_V + 1, :]).T
        y_ref[rows, :] = (o * _silu(g_ref[rows, :].astype(F32))).astype(y_ref.dtype)

    def step(score=None, num=None, att=None):
        sb, nb, ab = (None if b is None else b % ATTN_BUFS for b in (score, num, att))
        qb = None if score is None else q_ref[rows_of(score), :]
        m_num = None if num is None else m_ref[nb]
        m_run, acc = None, None
        for c0 in range(0, S, KEY_CHUNK):
            ks = slice(c0, c0 + KEY_CHUNK)
            if score is not None:
                st = lax.dot_general(k_ref[ks, :], qb, (((1,), (1,)), ((), ())),
                                     preferred_element_type=F32)
                st_ref[sb, ks, :] = st
                m_c = jnp.max(st, axis=0, keepdims=True)
                m_run = m_c if m_run is None else jnp.maximum(m_run, m_c)
            if num is not None:
                p_ref[nb, ks, :] = numerators(st_ref[nb, ks, :], m_num)
            if att is not None:
                part = jnp.dot(vt1_ref[:, ks], p_ref[ab, ks, :], preferred_element_type=F32)
                acc = part if acc is None else acc + part
        if score is not None:
            m_ref[sb] = m_run
        if att is not None:
            rows = rows_of(att)
            o = (acc[0:MLA_V, :] / acc[MLA_V:MLA_V + 1, :]).T
            y_ref[rows, :] = (o * _silu(g_ref[rows, :].astype(F32))).astype(y_ref.dtype)

    def latent(blk):
        return blk if blk <= n_lat else None

    step(score=1)
    step(score=latent(2), num=1)
    st_ctx = scores(0, ctx)
    attend(0, numerators(st_ctx, jnp.max(st_ctx, axis=0, keepdims=True)), ctx)
    for t in range(1, n_lat + 1):
        step(score=latent(t + 2), num=latent(t + 1), att=t)


def _attn_call(q, k, vt, u, ctx):
    B, S, _ = q.shape
    tq = 256
    assert ctx == tq
    gcol = U_MLA_G // MLA_V
    return pl.pallas_call(
        functools.partial(_attn_kernel, ctx=ctx, tq=tq),
        out_shape=jax.ShapeDtypeStruct((B, S, MLA_W), BF16),
        grid=(B, MLA_HEADS),
        in_specs=[
            pl.BlockSpec((None, S, MLA_HP), lambda b, h: (b, 0, h)),
            pl.BlockSpec((None, S, MLA_HP), lambda b, h: (b, 0, h)),
            pl.BlockSpec((None, MLA_V, S), lambda b, h: (b, h, 0)),
            pl.BlockSpec((None, S, MLA_V), lambda b, h: (b, 0, gcol + h)),
        ],
        out_specs=pl.BlockSpec((None, S, MLA_V), lambda b, h: (b, 0, h)),
        scratch_shapes=[pltpu.VMEM((ATTN_BUFS, S, tq), F32), pltpu.VMEM((ATTN_BUFS, 1, tq), F32),
                        pltpu.VMEM((ATTN_BUFS, S, tq), BF16),
                        pltpu.VMEM((MLA_V + BF16_ROWS, S), BF16)],
        compiler_params=_cparams(("parallel", "parallel")),
        name="mla_attn",
    )(q, k, vt, u)


def _lru_kernel(x_ref, g_ref, cw_ref, cb_ref, wg_ref, bg_ref, lam_ref, y_ref,
                af_ref, bf_ref, ab_ref, bb_ref, hlf_ref, plf_ref, hlb_ref, plb_ref, *, ctx):
    S, CB = x_ref.shape
    NG = CB // LANES
    x = x_ref[...].astype(F32)
    rows = lax.broadcasted_iota(jnp.int32, (S, 1), 0)
    in_ctx = rows < ctx

    xc = x * cw_ref[2:3, :] + cb_ref[...]
    for tap, off in ((0, -2), (1, -1), (3, 1)):
        xs = pltpu.roll(x, (-off) % S, axis=0)
        src = rows + off
        valid = (src >= 0) & (src < S) & ((src < ctx) == in_ctx)
        xc = xc + jnp.where(valid, xs, 0.0) * cw_ref[tap:tap + 1, :]

    xb = xc.astype(BF16)
    xh = 0.5 * xc
    for d, (a_ref, b_ref) in enumerate(((af_ref, bf_ref), (ab_ref, bb_ref))):
        t = jnp.tanh(jnp.dot(xb, wg_ref[d], preferred_element_type=F32) + bg_ref[d])
        t_r, t_i = t[:, :CB], t[:, CB:]
        c2 = (-0.5 * LRU_C * LOG2_E) * _softplus(-lam_ref[d:d + 1, :])
        a = jnp.exp2(t_r * c2 + c2)
        w = (1.0 - a) * (1.0 + a)
        b = jnp.where(w > 0.0, w * lax.rsqrt(w), 0.0) * (t_i * xh + xh)
        for g in range(NG):
            gs = slice(g * LANES, (g + 1) * LANES)
            a_ref[g] = a[:, gs]
            b_ref[g] = b[:, gs]

    R = 8
    zeros, ones = jnp.zeros((R, LANES), F32), jnp.ones((R, LANES), F32)
    row = lax.broadcasted_iota(jnp.int32, (R, LANES), 0)

    def tile_scan(a, b, enter, rev):
        for d in (1, 2, 4):
            shift = R - d if rev else d
            keep = (row < R - d) if rev else (row >= d)
            a_s = pltpu.roll(a, shift, axis=0)
            b_s = pltpu.roll(b, shift, axis=0)
            b = jnp.where(keep, a * b_s, 0.0) + b
            a = jnp.where(keep, a * a_s, a)
        return a * enter + b

    def scan_part(base, n_rows, enter):
        seg = n_rows // R - 1
        assert seg % 2 == 1
        last = pl.ds(base + R * seg, R)

        def strided(i):
            return pl.ds(base + i, R, stride=seg)

        def packed(i):
            return pl.ds(pl.multiple_of(base + i * R, R), R)

        def local(i, carry):
            ib = seg - 1 - i
            out = []
            for g, (hf, pf, hb, pb) in enumerate(carry):
                a = af_ref[g, strided(i), :]
                hf = a * hf + bf_ref[g, strided(i), :]
                pf = a * pf
                hlf_ref[g, packed(i), :] = hf
                plf_ref[g, packed(i), :] = pf
                a = ab_ref[g, strided(ib), :]
                hb = a * hb + bb_ref[g, strided(ib), :]
                pb = a * pb
                hlb_ref[g, packed(ib), :] = hb
                plb_ref[g, packed(ib), :] = pb
                out.append((hf, pf, hb, pb))
            return tuple(out)

        ends = lax.fori_loop(0, seg, local, ((zeros, ones, zeros, ones),) * NG, unroll=4)

        leave, fixes = [], []
        for g, ((hf, pf, hb, pb), (c_f, c_b)) in enumerate(zip(ends, enter)):
            rows_f, rows_b = [], [None] * R
            for s in range(R):
                rows_f.append(c_f)
                c_f = hf[s:s + 1, :] + pf[s:s + 1, :] * c_f
            tail_f = tile_scan(af_ref[g, last, :], bf_ref[g, last, :], c_f, False)
            tail_b = tile_scan(ab_ref[g, last, :], bb_ref[g, last, :], c_b, True)
            c_f, c_b = tail_f[R - 1:R, :], tail_b[0:1, :]
            for s in reversed(range(R)):
                rows_b[s] = c_b
                c_b = hb[s:s + 1, :] + pb[s:s + 1, :] * c_b
            af_ref[g, last, :] = tail_f
            ab_ref[g, last, :] = tail_b
            leave.append((c_f, c_b))
            fixes.append((jnp.concatenate(rows_f, axis=0), jnp.concatenate(rows_b, axis=0)))

        def fix(i, carry):
            for g, (cf, cb) in enumerate(fixes):
                af_ref[g, strided(i), :] = hlf_ref[g, packed(i), :] + plf_ref[g, packed(i), :] * cf
                ab_ref[g, strided(i), :] = hlb_ref[g, packed(i), :] + plb_ref[g, packed(i), :] * cb
            return carry

        lax.fori_loop(0, seg, fix, 0, unroll=4)
        return leave

    row0 = jnp.zeros((1, LANES), F32)
    scan_part(ctx, S - ctx, scan_part(0, ctx, [(row0, row0)] * NG))

    g_all = g_ref[...].astype(F32)
    for g in range(NG):
        gs = slice(g * LANES, (g + 1) * LANES)
        y_ref[:, gs] = ((af_ref[g] + ab_ref[g]) * _silu(g_all[:, gs])).astype(y_ref.dtype)


def _lru_call(u, cw, cb, wg, bg, lam, l, ctx):
    B, S, _ = u.shape
    ncb = LRU_W // LRU_CB
    return pl.pallas_call(
        functools.partial(_lru_kernel, ctx=ctx),
        out_shape=jax.ShapeDtypeStruct((B, S, LRU_W), BF16),
        grid=(B, ncb),
        in_specs=[
            pl.BlockSpec((None, S, LRU_CB), lambda b, c: (b, 0, U_LRU_X // LRU_CB + c)),
            pl.BlockSpec((None, S, LRU_CB), lambda b, c: (b, 0, U_LRU_G // LRU_CB + c)),
            pl.BlockSpec((None, LRU_CONV, LRU_CB), lambda b, c: (l, 0, c)),
            pl.BlockSpec((None, 1, LRU_CB), lambda b, c: (l, 0, c)),
            pl.BlockSpec((None, 2, None, LRU_CB, 2 * LRU_CB), lambda b, c: (l, 0, c, 0, 0)),
            pl.BlockSpec((None, 2, None, 1, 2 * LRU_CB), lambda b, c: (l, 0, c, 0, 0)),
            pl.BlockSpec((None, 2, LRU_CB), lambda b, c: (l, 0, c)),
        ],
        out_specs=pl.BlockSpec((None, S, LRU_CB), lambda b, c: (b, 0, c)),
        scratch_shapes=[pltpu.VMEM((LRU_CB // LANES, S, LANES), F32) for _ in range(8)],
        compiler_params=_cparams(("parallel", "parallel")),
        name="rg_lru",
    )(u, u, cw, cb, wg, bg, lam)


def _ret_kernel(lg_ref, q_ref, k_ref, v_ref, g_ref, cos_ref, sin_ref, y_ref, ks_ref, kv_ref, st_ref,
                *, n_ctx, layer):
    C = RET_CHUNK
    S = q_ref.shape[0]
    n_all = S // C
    unroll = next(f for f in (9, 6, 3, 2, 1) if n_all % f == 0)
    hd = pl.program_id(1)
    lgf = lg_ref[2 * layer, hd]
    lgb = lg_ref[2 * layer + 1, hd]

    ri = lax.broadcasted_iota(jnp.int32, (C, C), 0).astype(F32)
    ci = lax.broadcasted_iota(jnp.int32, (C, C), 1).astype(F32)
    diff = ri - ci
    decay = jnp.where(diff >= 0, jnp.exp(jnp.maximum(diff, 0.0) * lgf),
                      jnp.exp(jnp.maximum(-diff, 0.0) * lgb))
    zeta_f = jnp.exp((C - 1 - ri) * lgf)
    xi_f = jnp.exp((ri + 1) * lgf)
    gc_f = jnp.exp(jnp.full((C, C), C, F32) * lgf)
    zeta_b = jnp.exp(ri * lgb)
    xi_b = jnp.exp((C - ri) * lgb)
    gc_b = jnp.exp(jnp.full((C, C), C, F32) * lgb)
    def rows_of(c):
        return pl.ds(pl.multiple_of(c * C, C), C)

    def chunk_kv(c, carry):
        sl = rows_of(c)
        k = _rope(k_ref[sl, :].astype(F32), cos_ref[sl, :], sin_ref[sl, :]) * RET_K_SCALE
        ks_ref[sl, :] = k.astype(BF16)
        kz = jnp.concatenate([k * zeta_f, k * zeta_b], axis=1).astype(BF16)
        kv_ref[c] = lax.dot_general(v_ref[sl, :].astype(BF16), kz, (((0,), (0,)), ((), ())),
                                    preferred_element_type=F32)
        return carry

    lax.fori_loop(0, n_all, chunk_kv, 0, unroll=unroll)

    def fwd_state(c, r):
        st_ref[c, :, 0:C] = r.astype(BF16)
        return gc_f * r + kv_ref[c, :, 0:C]

    def bwd_state(i, r):
        c = jnp.where(i < n_ctx, n_ctx - 1 - i, n_all - 1 - (i - n_ctx))
        st_ref[c, :, C:2 * C] = r.astype(BF16)
        return gc_b * r + kv_ref[c, :, C:2 * C]

    lax.fori_loop(0, n_all, fwd_state, jnp.zeros((C, C), F32))
    lax.fori_loop(0, n_all, bwd_state, jnp.zeros((C, C), F32))

    group = next(f for f in (6, 3, 2, 1) if n_all % f == 0)

    def chunks_out(i, carry):
        cs = [i * group + g for g in range(group)]
        sls = [rows_of(c) for c in cs]
        qs = [_rope(q_ref[sl, :].astype(F32), cos_ref[sl, :], sin_ref[sl, :]) for sl in sls]
        ss = [_dot_nt(q.astype(BF16), ks_ref[sl, :]) for q, sl in zip(qs, sls)]
        qxs = [jnp.concatenate([q * xi_f, q * xi_b], axis=1).astype(BF16) for q in qs]
        os = [jnp.dot((s * decay).astype(BF16), v_ref[sl, :].astype(BF16), preferred_element_type=F32)
              + _dot_nt(qx, st_ref[c]) for s, sl, qx, c in zip(ss, sls, qxs, cs)]
        for o, sl in zip(os, sls):
            mu = jnp.mean(o, axis=-1, keepdims=True)
            oc = o - mu
            var = jnp.mean(oc * oc, axis=-1, keepdims=True)
            y = oc * lax.rsqrt(var + LN_EPS) * _silu(g_ref[sl, :].astype(F32))
            y_ref[sl, :] = y.astype(y_ref.dtype)
        return carry

    lax.fori_loop(0, n_all // group, chunks_out, 0)


def _ret_call(u, log_g, l, cos, sin, ctx):
    B, S, _ = u.shape
    n_all = S // RET_CHUNK

    def col(base):
        return lambda b, h: (b, 0, base // RET_DH + h)

    return pl.pallas_call(
        functools.partial(_ret_kernel, n_ctx=ctx // RET_CHUNK, layer=l),
        out_shape=jax.ShapeDtypeStruct((B, S, RET_W), BF16),
        grid=(B, RET_HEADS),
        in_specs=[
            pl.BlockSpec(memory_space=pltpu.SMEM),
            pl.BlockSpec((None, S, RET_DH), col(U_RET_Q)),
            pl.BlockSpec((None, S, RET_DH), col(U_RET_K)),
            pl.BlockSpec((None, S, RET_DH), col(U_RET_V)),
            pl.BlockSpec((None, S, RET_DH), col(U_RET_G)),
            pl.BlockSpec((S, RET_DH), lambda b, h: (0, 0)),
            pl.BlockSpec((S, RET_DH), lambda b, h: (0, 0)),
        ],
        out_specs=pl.BlockSpec((None, S, RET_DH), lambda b, h: (b, 0, h)),
        scratch_shapes=[pltpu.VMEM((S, RET_DH), BF16),
                        pltpu.VMEM((n_all, RET_DH, 2 * RET_DH), F32),
                        pltpu.VMEM((n_all, RET_DH, 2 * RET_DH), BF16)],
        compiler_params=_cparams(("parallel", "parallel")),
        name="retention",
    )(log_g, u, u, u, u, cos, sin)


def _outproj_kernel(ym_ref, yl_ref, yr_ref, h_ref, gt_ref, w_ref, lng_ref, lnb_ref, o_ref,
                    *, tm, ctx, alpha, skip, ctx_row):
    b = pl.program_id(0)
    j = pl.program_id(1) + skip

    def project(r0):
        rs = slice(r0, r0 + SUB_ROWS)
        y = jnp.concatenate([ym_ref[rs, :], yl_ref[rs, :], yr_ref[rs, :]], axis=1)
        return jnp.dot(y, w_ref[...], preferred_element_type=F32)

    starts = list(range(0, tm, SUB_ROWS))
    acc_next = project(starts[0])
    for i, r0 in enumerate(starts):
        acc = acc_next
        if i + 1 < len(starts):
            acc_next = project(starts[i + 1])
        rs = slice(r0, r0 + SUB_ROWS)
        rows = j * tm + r0 + lax.broadcasted_iota(jnp.int32, (SUB_ROWS, 1), 0)
        gt = jnp.where(rows < ctx, gt_ref[ctx_row:ctx_row + 1, :], gt_ref[pl.ds(b, 1), :])
        z = alpha * h_ref[rs, :] + gt * acc
        mu = jnp.mean(z, axis=-1, keepdims=True)
        zc = z - mu
        var = jnp.mean(zc * zc, axis=-1, keepdims=True)
        o_ref[rs, :] = zc * lax.rsqrt(var + LN_EPS) * lng_ref[...] + lnb_ref[...]


def _outproj_call(ym, yl, yr, h, mod, w, lng, lnb, l, ctx, tm, alpha, latent_only):
    B, S, D = h.shape
    skip = ctx // tm if latent_only else 0
    assert skip * tm == (ctx if latent_only else 0)
    n_out = S // tm - skip
    return pl.pallas_call(
        functools.partial(_outproj_kernel, tm=tm, ctx=ctx, alpha=alpha, skip=skip, ctx_row=B),
        out_shape=jax.ShapeDtypeStruct((B, n_out * tm, D), F32),
        grid=(B, n_out),
        in_specs=[
            pl.BlockSpec((None, tm, MLA_W), lambda b, j: (b, j + skip, 0)),
            pl.BlockSpec((None, tm, LRU_W), lambda b, j: (b, j + skip, 0)),
            pl.BlockSpec((None, tm, RET_W), lambda b, j: (b, j + skip, 0)),
            pl.BlockSpec((None, tm, D), lambda b, j: (b, j + skip, 0)),
            pl.BlockSpec((None, 8, D), lambda b, j: (l, 0, 2)),
            pl.BlockSpec((None, D, D), lambda b, j: (l, 0, 0), pipeline_mode=pl.Buffered(1)),
            pl.BlockSpec((None, 1, D), lambda b, j: (l, 0, 0)),
            pl.BlockSpec((None, 1, D), lambda b, j: (l, 0, 0)),
        ],
        out_specs=pl.BlockSpec((None, tm, D), lambda b, j: (b, j, 0)),
        compiler_params=_cparams(("parallel", "parallel")),
        name="out_proj_ln",
    )(ym, yl, yr, h, mod, w, lng, lnb)


def _rope_tables(rows, dim, ctx):
    row = jnp.repeat(jnp.arange(rows, dtype=F32), GRID_W)
    col = jnp.tile(jnp.arange(GRID_W, dtype=F32), rows)
    quarter = dim // 4
    inv = ROPE_BASE ** (-jnp.arange(quarter, dtype=F32) / quarter)
    ang = jnp.stack([row[:, None] * inv, col[:, None] * inv], axis=1)
    cos, sin = jnp.cos(ang), jnp.sin(ang)
    pad = ((0, 0), (0, LANES // 2 - 2 * quarter))
    cos_h = jnp.pad(jnp.concatenate([cos[:, 0], cos[:, 1]], axis=-1), pad)
    sin_h = jnp.pad(jnp.concatenate([sin[:, 0], sin[:, 1]], axis=-1), pad)
    cos_l = jnp.concatenate([cos_h, cos_h], axis=-1)
    sin_l = jnp.concatenate([-sin_h, sin_h], axis=-1)
    cos_c = jnp.ones((ctx, LANES), F32)
    sin_c = jnp.zeros((ctx, LANES), F32)
    return jnp.concatenate([cos_c, cos_l], axis=0), jnp.concatenate([sin_c, sin_l], axis=0)


def _rotary_lane_order(dim):
    quarter = dim // 4
    return ((0, 0), (quarter, LANES // 2), (2 * quarter, quarter), (3 * quarter, LANES // 2 + quarter))


def _rotary_moves(src, dst, dim, groups):
    return tuple((src + g * dim + s, dst + g * LANES + d, dim // 4)
                 for g in range(groups) for s, d in _rotary_lane_order(dim))


W_IN_MOVES = ((0, U_QLAT, 512), (512, U_KVLAT, 256), (832, U_MLA_G, 1024),
              (1856, U_LRU_X, 512), (2368, U_LRU_G, 512), (3904, U_RET_V, 512), (4416, U_RET_G, 512)
              ) + _rotary_moves(768, U_KR, MLA_ROPE, 1) + _rotary_moves(2880, U_RET_Q, RET_DH, RET_HEADS
              ) + _rotary_moves(3392, U_RET_K, RET_DH, RET_HEADS)
W_IN_ZERO = tuple((U_KR + o, LANES // 2 - MLA_ROPE // 2) for o in (MLA_ROPE // 2, LANES // 2 + MLA_ROPE // 2))


def _w_in_layout_kernel(w_ref, o_ref):
    for src, dst, width in W_IN_MOVES:
        o_ref[dst:dst + width, :] = w_ref[src:src + width, :].astype(BF16)
    for dst, width in W_IN_ZERO:
        o_ref[dst:dst + width, :] = jnp.zeros((width, o_ref.shape[1]), BF16)


def _layout_w_in(w_in):
    L, D, W = w_in.shape
    tc = 512
    return pl.pallas_call(
        _w_in_layout_kernel,
        out_shape=jax.ShapeDtypeStruct((L, U_W, D), BF16),
        grid=(L, D // tc),
        in_specs=[pl.BlockSpec((None, W, tc), lambda l, r: (l, 0, r))],
        out_specs=pl.BlockSpec((None, U_W, tc), lambda l, r: (l, 0, r)),
        compiler_params=_cparams(("parallel", "parallel")),
        name="w_in_layout",
    )(jnp.swapaxes(w_in, 1, 2))


def _layout_w_uq(w_uq):
    L = w_uq.shape[0]
    w = w_uq.reshape(L, MLA_Q_RANK, MLA_HEADS, MLA_NOPE + MLA_ROPE)
    quarter = MLA_ROPE // 4
    rope = jnp.zeros((L, MLA_Q_RANK, MLA_HEADS, LANES), w.dtype)
    for s, d in _rotary_lane_order(MLA_ROPE):
        rope = rope.at[..., d:d + quarter].set(w[..., MLA_NOPE + s:MLA_NOPE + s + quarter])
    w = jnp.concatenate([w[..., :MLA_NOPE], rope], axis=-1)
    return w.reshape(L, MLA_Q_RANK, MLA_HEADS * MLA_HP).astype(BF16)


def _layout_w_ukv(w_ukv):
    L = w_ukv.shape[0]
    w = w_ukv.reshape(L, MLA_KV_RANK, MLA_HEADS, MLA_NOPE + MLA_V)
    wk = w[..., :MLA_NOPE].reshape(L, MLA_KV_RANK, MLA_W).astype(BF16)
    wvt = w[..., MLA_NOPE:].reshape(L, MLA_KV_RANK, MLA_W).transpose(0, 2, 1).astype(BF16)
    return wk, wvt


def _layout_lru_gates(w_r, b_r, w_i, b_i):
    L = w_r.shape[0]
    eye = jnp.eye(LRU_BLOCKS, dtype=w_r.dtype)
    ncb = LRU_W // LRU_CB

    def dense(w):
        return jnp.einsum("ldgij,gh->ldgihj", w, eye).reshape(L, 2, LRU_W, LRU_W)

    def diag_blocks(w):
        w = w.reshape(L, 2, ncb, LRU_CB, ncb, LRU_CB)
        return jnp.stack([w[:, :, c, :, c, :] for c in range(ncb)], axis=2)

    wg = jnp.concatenate([diag_blocks(dense(w_r)), diag_blocks(dense(w_i))], axis=-1).astype(BF16)
    bg = jnp.concatenate([b_r.reshape(L, 2, ncb, 1, LRU_CB), b_i.reshape(L, 2, ncb, 1, LRU_CB)], axis=-1)
    return 0.5 * wg, 0.5 * bg


def kernel(x, c, ctx, c_ctx, w_ada, b_ada, w_in, mla_q_norm_g, mla_kv_norm_g, mla_w_uq, mla_w_ukv,
           lru_conv_w, lru_conv_b, lru_w_r, lru_b_r, lru_w_i, lru_b_i, lru_lambda, ret_decay,
           w_out, ln_g, ln_b):
    B, T, D = x.shape
    L = w_in.shape[0]
    n_ctx = ctx.shape[1]
    S = n_ctx + T
    assert D == 2 * MLA_W and w_in.shape[2] == 4928 and n_ctx % 256 == 0 and T % 256 == 0
    assert B + 1 <= 8
    alpha = (2 * L) ** 0.25
    tm = 768 if S % 768 == 0 else 256

    cos_m, sin_m = _rope_tables(T // GRID_W, MLA_ROPE, n_ctx)
    cos_r, sin_r = _rope_tables(T // GRID_W, RET_DH, n_ctx)

    cond = jnp.zeros((8, D), F32).at[:B].set(c).at[B].set(c_ctx)
    mod_all = _ada_call(cond, w_ada, b_ada)

    h = _ln0_call(ctx, x)
    log_g = jax.nn.log_sigmoid(ret_decay.astype(F32)).reshape(2 * L, RET_HEADS)

    w_in_l = _layout_w_in(w_in)
    wuq = _layout_w_uq(mla_w_uq)
    wk, wvt = _layout_w_ukv(mla_w_ukv)
    wg, bg = _layout_lru_gates(lru_w_r, lru_b_r, lru_w_i, lru_b_i)
    w_out_b = w_out.astype(BF16)
    gq, gkv = mla_q_norm_g[:, None, :], mla_kv_norm_g[:, None, :]
    conv_b, lng, lnb = lru_conv_b[:, None, :], ln_g[:, None, :], ln_b[:, None, :]

    for l in range(L):
        last = l == L - 1
        u = _inproj_call(h, mod_all, w_in_l, l, n_ctx, tm)
        q, k, vt = _mla_prep_call(u, gq, gkv, wuq, wk, wvt, l, cos_m, sin_m)
        y_mla = _attn_call(q, k, vt, u, n_ctx)
        y_lru = _lru_call(u, lru_conv_w, conv_b, wg, bg, lru_lambda, l, n_ctx)
        y_ret = _ret_call(u, log_g, l, cos_r, sin_r, n_ctx)
        h = _outproj_call(y_mla, y_lru, y_ret, h, mod_all, w_out_b, lng, lnb, l, n_ctx,
                          256 if last else tm, alpha, last)
    return h
```

```python
import functools

import jax
import jax.numpy as jnp
from jax import lax
from jax.experimental import pallas as pl
from jax.experimental.pallas import tpu as pltpu

F32 = jnp.float32
BF16 = jnp.bfloat16

GRID_W = 64
MLA_V = 128
MLA_NOPE = 128
MLA_ROPE = 64
MLA_HEADS = 8
MLA_W = MLA_HEADS * MLA_V
MLA_Q_RANK = 512
MLA_KV_RANK = 256
MLA_SCALE = (MLA_NOPE + MLA_ROPE) ** -0.5
LOG2_E = 1.4426950408889634
MLA_Q_SCALE = MLA_SCALE * LOG2_E
MLA_HP = 256
LRU_W = 512
LRU_BLOCKS = 8
LRU_BW = LRU_W // LRU_BLOCKS
LRU_CONV = 4
LRU_C = 8.0
LRU_CB = 256
RET_HEADS = 4
RET_DH = 128
RET_W = RET_HEADS * RET_DH
RET_CHUNK = 128
RET_K_SCALE = RET_DH ** -0.5
ROPE_BASE = 10000.0
LN_EPS = 1e-5
RMS_EPS = 1e-6
LANES = 128
BF16_ROWS = 16
KEY_CHUNK = 768
ATTN_BUFS = 3
ATTN_HEADS = 2
SUB_ROWS = 256

U_QLAT = 0
U_LRU_X = 512
U_LRU_G = 1024
U_RET_Q = 1536
U_RET_K = 2048
U_RET_V = 2560
U_RET_G = 3072
U_MLA_G = 3584
U_KVLAT = 4608
U_KR = 4864
U_USED = 4992
MXU_COLS = 256
U_CHUNKS = 2
U_TN = -(-U_USED // (U_CHUNKS * MXU_COLS)) * MXU_COLS
U_W = U_CHUNKS * U_TN

VMEM_LIMIT = 56 * 1024 * 1024


def _cparams(sem):
    return pltpu.CompilerParams(dimension_semantics=sem, vmem_limit_bytes=VMEM_LIMIT)


def _dot_nt(a, b):
    return lax.dot_general(a, b, (((1,), (1,)), ((), ())), preferred_element_type=F32)


def _sigmoid(x):
    return 0.5 * jnp.tanh(0.5 * x) + 0.5


def _silu(x):
    return x * _sigmoid(x)


def _softplus(x):
    return jnp.maximum(x, 0.0) + jnp.log1p(jnp.exp(-jnp.abs(x)))


def _rope(x, cos, sin):
    return x * cos + pltpu.roll(x, LANES // 2, axis=x.ndim - 1) * sin


def _ada_kernel(cond_ref, w_ref, b_ref, o_ref):
    c = cond_ref[...]
    a = _silu(c).astype(BF16)
    o_ref[...] = jnp.dot(a, w_ref[...].astype(BF16), preferred_element_type=F32) + b_ref[...]


def _ada_call(cond, w_ada, b_ada):
    L, D, D3 = w_ada.shape
    tn = 1024
    return pl.pallas_call(
        _ada_kernel,
        out_shape=jax.ShapeDtypeStruct((L, 8, D3), F32),
        grid=(L, D3 // tn),
        in_specs=[
            pl.BlockSpec((8, D), lambda l, n: (0, 0)),
            pl.BlockSpec((None, D, tn), lambda l, n: (l, 0, n)),
            pl.BlockSpec((None, 1, tn), lambda l, n: (l, 0, n)),
        ],
        out_specs=pl.BlockSpec((None, 8, tn), lambda l, n: (l, 0, n)),
        compiler_params=_cparams(("parallel", "parallel")),
        name="ada_mod",
    )(cond, w_ada, b_ada.reshape(L, 1, D3))


def _ln0_kernel(c_ref, x_ref, o_ref, *, n_ctx_blocks):
    def norm(ref):
        x = ref[...]
        mu = jnp.mean(x, axis=-1, keepdims=True)
        xc = x - mu
        var = jnp.mean(xc * xc, axis=-1, keepdims=True)
        o_ref[...] = xc * lax.rsqrt(var + LN_EPS)

    j = pl.program_id(1)
    pl.when(j < n_ctx_blocks)(lambda: norm(c_ref))
    pl.when(j >= n_ctx_blocks)(lambda: norm(x_ref))


def _ln0_call(ctx, x):
    B, T, D = x.shape
    n_ctx = ctx.shape[1]
    tm = 256
    nc = n_ctx // tm
    return pl.pallas_call(
        functools.partial(_ln0_kernel, n_ctx_blocks=nc),
        out_shape=jax.ShapeDtypeStruct((B, n_ctx + T, D), F32),
        grid=(B, (n_ctx + T) // tm),
        in_specs=[pl.BlockSpec((None, tm, D), lambda b, j: (b, jnp.minimum(j, nc - 1), 0)),
                  pl.BlockSpec((None, tm, D), lambda b, j: (b, jnp.maximum(j - nc, 0), 0))],
        out_specs=pl.BlockSpec((None, tm, D), lambda b, j: (b, j, 0)),
        compiler_params=_cparams(("parallel", "parallel")),
        name="ln_entry",
    )(ctx, x)


def _inproj_kernel(h_ref, sh_ref, sc_ref, *rest, tm, ctx, ctx_row):
    w_refs, (u_ref, xs_ref) = rest[:U_CHUNKS], rest[U_CHUNKS:]
    b = pl.program_id(0)
    j = pl.program_id(1)
    n = pl.program_id(2)

    @pl.when(n == 0)
    def _():
        for r0 in range(0, tm, SUB_ROWS):
            rs = slice(r0, r0 + SUB_ROWS)
            rows = j * tm + r0 + lax.broadcasted_iota(jnp.int32, (SUB_ROWS, 1), 0)
            is_ctx = rows < ctx
            sh = jnp.where(is_ctx, sh_ref[ctx_row:ctx_row + 1, :], sh_ref[pl.ds(b, 1), :])
            sc = jnp.where(is_ctx, sc_ref[ctx_row:ctx_row + 1, :], sc_ref[pl.ds(b, 1), :])
            xs = (h_ref[rs, :] * (1.0 + sc) + sh).astype(BF16)
            xs_ref[rs, :] = xs
            u_ref[rs, :] = _dot_nt(xs, w_refs[0][...]).astype(u_ref.dtype)

    for c in range(1, U_CHUNKS):
        @pl.when(n == c)
        def _():
            u_ref[...] = _dot_nt(xs_ref[...], w_refs[c][...]).astype(u_ref.dtype)


def _inproj_call(h, mod_all, w_all, l, ctx, tm):
    B, S, D = h.shape
    return pl.pallas_call(
        functools.partial(_inproj_kernel, tm=tm, ctx=ctx, ctx_row=B),
        out_shape=jax.ShapeDtypeStruct((B, S, U_W), F32),
        grid=(B, S // tm, U_CHUNKS),
        in_specs=[
            pl.BlockSpec((None, tm, D), lambda b, j, n: (b, j, 0)),
            pl.BlockSpec((None, 8, D), lambda b, j, n: (l, 0, 0)),
            pl.BlockSpec((None, 8, D), lambda b, j, n: (l, 0, 1)),
        ] + [
            pl.BlockSpec((None, U_TN, D), functools.partial(lambda b, j, n, c: (l, c, 0), c=c),
                         pipeline_mode=pl.Buffered(1))
            for c in range(U_CHUNKS)
        ],
        out_specs=pl.BlockSpec((None, tm, U_TN), lambda b, j, n: (b, j, n)),
        scratch_shapes=[pltpu.VMEM((tm, D), BF16)],
        compiler_params=_cparams(("parallel", "parallel", "arbitrary")),
        name="in_proj",
    )(h, mod_all, mod_all, *([w_all] * U_CHUNKS))


def _rms(x, g):
    return x * lax.rsqrt(jnp.mean(x * x, axis=-1, keepdims=True) + RMS_EPS) * g


def _mla_prep_kernel(ql_ref, kvl_ref, kr_ref, gq_ref, gkv_ref, wuq_ref, wk_ref, wvt_ref,
                     cos_ref, sin_ref, q_ref, k_ref, vt_ref):
    cos = cos_ref[...]
    sin = sin_ref[...]

    zq = _rms(ql_ref[...].astype(F32), gq_ref[...]).astype(BF16)
    q = jnp.dot(zq, wuq_ref[...], preferred_element_type=F32)
    for hd in range(MLA_HEADS):
        c0 = hd * MLA_HP
        q_ref[:, c0:c0 + MLA_NOPE] = (q[:, c0:c0 + MLA_NOPE] * MLA_Q_SCALE).astype(q_ref.dtype)
        qr = _rope(q[:, c0 + MLA_NOPE:c0 + MLA_HP], cos, sin)
        q_ref[:, c0 + MLA_NOPE:c0 + MLA_HP] = (qr * MLA_Q_SCALE).astype(q_ref.dtype)

    zk = _rms(kvl_ref[...].astype(F32), gkv_ref[...]).astype(BF16)
    kn = jnp.dot(zk, wk_ref[...], preferred_element_type=F32)
    kr = _rope(kr_ref[...].astype(F32), cos, sin).astype(k_ref.dtype)
    for hd in range(MLA_HEADS):
        c0 = hd * MLA_HP
        k_ref[:, c0:c0 + MLA_NOPE] = kn[:, hd * MLA_NOPE:(hd + 1) * MLA_NOPE].astype(k_ref.dtype)
        k_ref[:, c0 + MLA_NOPE:c0 + MLA_HP] = kr
    vt = lax.dot_general(wvt_ref[...], zk, (((1,), (1,)), ((), ())), preferred_element_type=F32)
    vt_ref[...] = vt.astype(vt_ref.dtype)


def _mla_prep_call(u, gq, gkv, wuq, wk, wvt, l, cos, sin):
    B, S, _ = u.shape
    tp = 768 if S % 768 == 0 else 256
    HW = MLA_HEADS * MLA_HP
    return pl.pallas_call(
        _mla_prep_kernel,
        out_shape=(
            jax.ShapeDtypeStruct((B, S, HW), BF16),
            jax.ShapeDtypeStruct((B, S, HW), BF16),
            jax.ShapeDtypeStruct((B, MLA_W, S), BF16),
        ),
        grid=(B, S // tp),
        in_specs=[
            pl.BlockSpec((None, tp, MLA_Q_RANK), lambda b, j: (b, j, U_QLAT // MLA_Q_RANK)),
            pl.BlockSpec((None, tp, MLA_KV_RANK), lambda b, j: (b, j, U_KVLAT // MLA_KV_RANK)),
            pl.BlockSpec((None, tp, LANES), lambda b, j: (b, j, U_KR // LANES)),
            pl.BlockSpec((None, 1, MLA_Q_RANK), lambda b, j: (l, 0, 0)),
            pl.BlockSpec((None, 1, MLA_KV_RANK), lambda b, j: (l, 0, 0)),
            pl.BlockSpec((None, MLA_Q_RANK, HW), lambda b, j: (l, 0, 0)),
            pl.BlockSpec((None, MLA_KV_RANK, MLA_W), lambda b, j: (l, 0, 0)),
            pl.BlockSpec((None, MLA_W, MLA_KV_RANK), lambda b, j: (l, 0, 0)),
            pl.BlockSpec((tp, LANES), lambda b, j: (j, 0)),
            pl.BlockSpec((tp, LANES), lambda b, j: (j, 0)),
        ],
        out_specs=(
            pl.BlockSpec((None, tp, HW), lambda b, j: (b, j, 0)),
            pl.BlockSpec((None, tp, HW), lambda b, j: (b, j, 0)),
            pl.BlockSpec((None, MLA_W, tp), lambda b, j: (b, 0, j)),
        ),
        compiler_params=_cparams(("parallel", "parallel")),
        name="mla_prep",
    )(u, u, u, gq, gkv, wuq, wk, wvt, cos, sin)


def _attn_kernel(q_ref, k_ref, vt_ref, g_ref, y_ref, st_ref, m_ref, p_ref, vt1_ref, *, ctx, tq, heads):
    S = k_ref.shape[0]
    n_lat = S // tq - 1
    assert n_lat >= 2 and S % KEY_CHUNK == 0
    items = [(hd, blk) for hd in range(heads) for blk in range(1, n_lat + 1)]

    for hd in range(heads):
        vt1_ref[hd, 0:MLA_V, :] = vt_ref[hd * MLA_V:(hd + 1) * MLA_V, :]
        vt1_ref[hd, MLA_V:, :] = jnp.ones((BF16_ROWS, S), BF16)

    def rows_of(blk):
        return slice(blk * tq, (blk + 1) * tq)

    def qk_cols(hd):
        return slice(hd * MLA_HP, (hd + 1) * MLA_HP)

    def numerators(st, m):
        return jnp.exp2(st - m).astype(BF16)

    def write_out(hd, blk, acc):
        rows, cols = rows_of(blk), slice(hd * MLA_V, (hd + 1) * MLA_V)
        o = (acc[0:MLA_V, :] / acc[MLA_V:MLA_V + 1, :]).T
        y_ref[rows, cols] = (o * _silu(g_ref[rows, cols].astype(F32))).astype(y_ref.dtype)

    def step(score=None, num=None, att=None):
        def pick(i):
            return (None, None, None) if i is None or i >= len(items) else (i % ATTN_BUFS,) + items[i]

        (sb, s_hd, s_blk), (nb, _, _), (ab, a_hd, a_blk) = pick(score), pick(num), pick(att)
        qb = None if sb is None else q_ref[rows_of(s_blk), qk_cols(s_hd)]
        m_num = None if nb is None else m_ref[nb]
        m_run, acc = None, None
        for c0 in range(0, S, KEY_CHUNK):
            ks = slice(c0, c0 + KEY_CHUNK)
            if nb is not None:
                p_ref[nb, ks, :] = numerators(st_ref[nb, ks, :], m_num)
            if sb is not None:
                st = _dot_nt(k_ref[ks, qk_cols(s_hd)], qb)
                st_ref[sb, ks, :] = st
                m_c = jnp.max(st, axis=0, keepdims=True)
                m_run = m_c if m_run is None else jnp.maximum(m_run, m_c)
            if ab is not None:
                part = jnp.dot(vt1_ref[a_hd, :, ks], p_ref[ab, ks, :], preferred_element_type=F32)
                acc = part if acc is None else acc + part
        if sb is not None:
            m_ref[sb] = m_run
        if ab is not None:
            write_out(a_hd, a_blk, acc)

    step(score=0)
    step(score=1, num=0)
    for hd in range(heads):
        st = _dot_nt(k_ref[0:ctx, qk_cols(hd)], q_ref[rows_of(0), qk_cols(hd)])
        p = numerators(st, jnp.max(st, axis=0, keepdims=True))
        write_out(hd, 0, jnp.dot(vt1_ref[hd, :, 0:ctx], p, preferred_element_type=F32))
    for t in range(len(items)):
        step(score=t + 2, num=t + 1, att=t)


def _attn_call(q, k, vt, u, ctx):
    B, S, _ = q.shape
    tq = 256
    assert ctx == tq
    nh = ATTN_HEADS
    gcol = U_MLA_G // (nh * MLA_V)
    assert MLA_HEADS % nh == 0 and U_MLA_G % (nh * MLA_V) == 0
    return pl.pallas_call(
        functools.partial(_attn_kernel, ctx=ctx, tq=tq, heads=nh),
        out_shape=jax.ShapeDtypeStruct((B, S, MLA_W), BF16),
        grid=(B, MLA_HEADS // nh),
        in_specs=[
            pl.BlockSpec((None, S, nh * MLA_HP), lambda b, h: (b, 0, h)),
            pl.BlockSpec((None, S, nh * MLA_HP), lambda b, h: (b, 0, h)),
            pl.BlockSpec((None, nh * MLA_V, S), lambda b, h: (b, h, 0)),
            pl.BlockSpec((None, S, nh * MLA_V), lambda b, h: (b, 0, gcol + h)),
        ],
        out_specs=pl.BlockSpec((None, S, nh * MLA_V), lambda b, h: (b, 0, h)),
        scratch_shapes=[pltpu.VMEM((ATTN_BUFS, S, tq), F32), pltpu.VMEM((ATTN_BUFS, 1, tq), F32),
                        pltpu.VMEM((ATTN_BUFS, S, tq), BF16),
                        pltpu.VMEM((nh, MLA_V + BF16_ROWS, S), BF16)],
        compiler_params=_cparams(("parallel", "parallel")),
        name="mla_attn",
    )(q, k, vt, u)


def _lru_kernel(x_ref, g_ref, cw_ref, cb_ref, wg_ref, bg_ref, lam_ref, y_ref,
                af_ref, bf_ref, ab_ref, bb_ref, hlf_ref, plf_ref, hlb_ref, plb_ref, *, ctx):
    S, CB = x_ref.shape
    NG = CB // LANES
    x = x_ref[...].astype(F32)
    rows = lax.broadcasted_iota(jnp.int32, (S, 1), 0)
    in_ctx = rows < ctx

    xc = x * cw_ref[2:3, :] + cb_ref[...]
    for tap, off in ((0, -2), (1, -1), (3, 1)):
        xs = pltpu.roll(x, (-off) % S, axis=0)
        src = rows + off
        valid = (src >= 0) & (src < S) & ((src < ctx) == in_ctx)
        xc = xc + jnp.where(valid, xs, 0.0) * cw_ref[tap:tap + 1, :]

    xb = xc.astype(BF16)
    xh = 0.5 * xc
    for d, (a_ref, b_ref) in enumerate(((af_ref, bf_ref), (ab_ref, bb_ref))):
        t = jnp.tanh(jnp.dot(xb, wg_ref[d], preferred_element_type=F32) + bg_ref[d])
        t_r, t_i = t[:, :CB], t[:, CB:]
        c2 = (-0.5 * LRU_C * LOG2_E) * _softplus(-lam_ref[d:d + 1, :])
        a = jnp.exp2(t_r * c2 + c2)
        w = (1.0 - a) * (1.0 + a)
        b = jnp.where(w > 0.0, w * lax.rsqrt(w), 0.0) * (t_i * xh + xh)
        for g in range(NG):
            gs = slice(g * LANES, (g + 1) * LANES)
            a_ref[g] = a[:, gs]
            b_ref[g] = b[:, gs]

    R = 8
    zeros, ones = jnp.zeros((R, LANES), F32), jnp.ones((R, LANES), F32)
    row = lax.broadcasted_iota(jnp.int32, (R, LANES), 0)

    def tile_scan(a, b, enter, rev):
        for d in (1, 2, 4):
            shift = R - d if rev else d
            keep = (row < R - d) if rev else (row >= d)
            a_s = pltpu.roll(a, shift, axis=0)
            b_s = pltpu.roll(b, shift, axis=0)
            b = jnp.where(keep, a * b_s, 0.0) + b
            a = jnp.where(keep, a * a_s, a)
        return a * enter + b

    def scan_part(base, n_rows, enter):
        seg = n_rows // R - 1
        assert seg % 2 == 1
        last = pl.ds(base + R * seg, R)

        def strided(i):
            return pl.ds(base + i, R, stride=seg)

        def packed(i):
            return pl.ds(pl.multiple_of(base + i * R, R), R)

        def local(i, carry):
            ib = seg - 1 - i
            out = []
            for g, (hf, pf, hb, pb) in enumerate(carry):
                a = af_ref[g, strided(i), :]
                hf = a * hf + bf_ref[g, strided(i), :]
                pf = a * pf
                hlf_ref[g, packed(i), :] = hf
                plf_ref[g, packed(i), :] = pf
                a = ab_ref[g, strided(ib), :]
                hb = a * hb + bb_ref[g, strided(ib), :]
                pb = a * pb
                hlb_ref[g, packed(ib), :] = hb
                plb_ref[g, packed(ib), :] = pb
                out.append((hf, pf, hb, pb))
            return tuple(out)

        ends = lax.fori_loop(0, seg, local, ((zeros, ones, zeros, ones),) * NG, unroll=4)

        leave, fixes = [], []
        for g, ((hf, pf, hb, pb), (c_f, c_b)) in enumerate(zip(ends, enter)):
            rows_f, rows_b = [], [None] * R
            for s in range(R):
                rows_f.append(c_f)
                c_f = hf[s:s + 1, :] + pf[s:s + 1, :] * c_f
            tail_f = tile_scan(af_ref[g, last, :], bf_ref[g, last, :], c_f, False)
            tail_b = tile_scan(ab_ref[g, last, :], bb_ref[g, last, :], c_b, True)
            c_f, c_b = tail_f[R - 1:R, :], tail_b[0:1, :]
            for s in reversed(range(R)):
                rows_b[s] = c_b
                c_b = hb[s:s + 1, :] + pb[s:s + 1, :] * c_b
            af_ref[g, last, :] = tail_f
            ab_ref[g, last, :] = tail_b
            leave.append((c_f, c_b))
            fixes.append((jnp.concatenate(rows_f, axis=0), jnp.concatenate(rows_b, axis=0)))

        def fix(i, carry):
            for g, (cf, cb) in enumerate(fixes):
                af_ref[g, strided(i), :] = hlf_ref[g, packed(i), :] + plf_ref[g, packed(i), :] * cf
                ab_ref[g, strided(i), :] = hlb_ref[g, packed(i), :] + plb_ref[g, packed(i), :] * cb
            return carry

        lax.fori_loop(0, seg, fix, 0, unroll=4)
        return leave

    row0 = jnp.zeros((1, LANES), F32)
    scan_part(ctx, S - ctx, scan_part(0, ctx, [(row0, row0)] * NG))

    g_all = g_ref[...].astype(F32)
    for g in range(NG):
        gs = slice(g * LANES, (g + 1) * LANES)
        y_ref[:, gs] = ((af_ref[g] + ab_ref[g]) * _silu(g_all[:, gs])).astype(y_ref.dtype)


def _lru_call(u, cw, cb, wg, bg, lam, l, ctx):
    B, S, _ = u.shape
    ncb = LRU_W // LRU_CB
    return pl.pallas_call(
        functools.partial(_lru_kernel, ctx=ctx),
        out_shape=jax.ShapeDtypeStruct((B, S, LRU_W), BF16),
        grid=(B, ncb),
        in_specs=[
            pl.BlockSpec((None, S, LRU_CB), lambda b, c: (b, 0, U_LRU_X // LRU_CB + c)),
            pl.BlockSpec((None, S, LRU_CB), lambda b, c: (b, 0, U_LRU_G // LRU_CB + c)),
            pl.BlockSpec((None, LRU_CONV, LRU_CB), lambda b, c: (l, 0, c)),
            pl.BlockSpec((None, 1, LRU_CB), lambda b, c: (l, 0, c)),
            pl.BlockSpec((None, 2, None, LRU_CB, 2 * LRU_CB), lambda b, c: (l, 0, c, 0, 0)),
            pl.BlockSpec((None, 2, None, 1, 2 * LRU_CB), lambda b, c: (l, 0, c, 0, 0)),
            pl.BlockSpec((None, 2, LRU_CB), lambda b, c: (l, 0, c)),
        ],
        out_specs=pl.BlockSpec((None, S, LRU_CB), lambda b, c: (b, 0, c)),
        scratch_shapes=[pltpu.VMEM((LRU_CB // LANES, S, LANES), F32) for _ in range(8)],
        compiler_params=_cparams(("parallel", "parallel")),
        name="rg_lru",
    )(u, u, cw, cb, wg, bg, lam)


def _ret_kernel(lg_ref, q_ref, k_ref, v_ref, g_ref, cos_ref, sin_ref, y_ref, ks_ref, kv_ref, st_ref,
                *, n_ctx, layer):
    C = RET_CHUNK
    S = q_ref.shape[0]
    n_all = S // C
    unroll = next(f for f in (9, 6, 3, 2, 1) if n_all % f == 0)
    hd = pl.program_id(1)
    lgf = lg_ref[2 * layer, hd]
    lgb = lg_ref[2 * layer + 1, hd]

    ri = lax.broadcasted_iota(jnp.int32, (C, C), 0).astype(F32)
    ci = lax.broadcasted_iota(jnp.int32, (C, C), 1).astype(F32)
    diff = ri - ci
    decay = jnp.where(diff >= 0, jnp.exp(jnp.maximum(diff, 0.0) * lgf),
                      jnp.exp(jnp.maximum(-diff, 0.0) * lgb))
    zeta_f = jnp.exp((C - 1 - ri) * lgf)
    xi_f = jnp.exp((ri + 1) * lgf)
    gc_f = jnp.exp(jnp.full((C, C), C, F32) * lgf)
    zeta_b = jnp.exp(ri * lgb)
    xi_b = jnp.exp((C - ri) * lgb)
    gc_b = jnp.exp(jnp.full((C, C), C, F32) * lgb)
    def rows_of(c):
        return pl.ds(pl.multiple_of(c * C, C), C)

    def chunk_kv(c, carry):
        sl = rows_of(c)
        k = _rope(k_ref[sl, :].astype(F32), cos_ref[sl, :], sin_ref[sl, :]) * RET_K_SCALE
        ks_ref[sl, :] = k.astype(BF16)
        kz = jnp.concatenate([k * zeta_f, k * zeta_b], axis=1).astype(BF16)
        kv_ref[c] = lax.dot_general(v_ref[sl, :].astype(BF16), kz, (((0,), (0,)), ((), ())),
                                    preferred_element_type=F32)
        return carry

    lax.fori_loop(0, n_all, chunk_kv, 0, unroll=unroll)

    def fwd_state(c, r):
        st_ref[c, :, 0:C] = r.astype(BF16)
        return gc_f * r + kv_ref[c, :, 0:C]

    def bwd_state(i, r):
        c = jnp.where(i < n_ctx, n_ctx - 1 - i, n_all - 1 - (i - n_ctx))
        st_ref[c, :, C:2 * C] = r.astype(BF16)
        return gc_b * r + kv_ref[c, :, C:2 * C]

    lax.fori_loop(0, n_all, fwd_state, jnp.zeros((C, C), F32))
    lax.fori_loop(0, n_all, bwd_state, jnp.zeros((C, C), F32))

    group = next(f for f in (6, 3, 2, 1) if n_all % f == 0)

    def chunks_out(i, carry):
        cs = [i * group + g for g in range(group)]
        sls = [rows_of(c) for c in cs]
        qs = [_rope(q_ref[sl, :].astype(F32), cos_ref[sl, :], sin_ref[sl, :]) for sl in sls]
        ss = [_dot_nt(q.astype(BF16), ks_ref[sl, :]) for q, sl in zip(qs, sls)]
        qxs = [jnp.concatenate([q * xi_f, q * xi_b], axis=1).astype(BF16) for q in qs]
        os = [jnp.dot((s * decay).astype(BF16), v_ref[sl, :].astype(BF16), preferred_element_type=F32)
              + _dot_nt(qx, st_ref[c]) for s, sl, qx, c in zip(ss, sls, qxs, cs)]
        for o, sl in zip(os, sls):
            mu = jnp.mean(o, axis=-1, keepdims=True)
            oc = o - mu
            var = jnp.mean(oc * oc, axis=-1, keepdims=True)
            y = oc * lax.rsqrt(var + LN_EPS) * _silu(g_ref[sl, :].astype(F32))
            y_ref[sl, :] = y.astype(y_ref.dtype)
        return carry

    lax.fori_loop(0, n_all // group, chunks_out, 0)


def _ret_call(u, log_g, l, cos, sin, ctx):
    B, S, _ = u.shape
    n_all = S // RET_CHUNK

    def col(base):
        return lambda b, h: (b, 0, base // RET_DH + h)

    return pl.pallas_call(
        functools.partial(_ret_kernel, n_ctx=ctx // RET_CHUNK, layer=l),
        out_shape=jax.ShapeDtypeStruct((B, S, RET_W), BF16),
        grid=(B, RET_HEADS),
        in_specs=[
            pl.BlockSpec(memory_space=pltpu.SMEM),
            pl.BlockSpec((None, S, RET_DH), col(U_RET_Q)),
            pl.BlockSpec((None, S, RET_DH), col(U_RET_K)),
            pl.BlockSpec((None, S, RET_DH), col(U_RET_V)),
            pl.BlockSpec((None, S, RET_DH), col(U_RET_G)),
            pl.BlockSpec((S, RET_DH), lambda b, h: (0, 0)),
            pl.BlockSpec((S, RET_DH), lambda b, h: (0, 0)),
        ],
        out_specs=pl.BlockSpec((None, S, RET_DH), lambda b, h: (b, 0, h)),
        scratch_shapes=[pltpu.VMEM((S, RET_DH), BF16),
                        pltpu.VMEM((n_all, RET_DH, 2 * RET_DH), F32),
                        pltpu.VMEM((n_all, RET_DH, 2 * RET_DH), BF16)],
        compiler_params=_cparams(("parallel", "parallel")),
        name="retention",
    )(log_g, u, u, u, u, cos, sin)


def _outproj_kernel(ym_ref, yl_ref, yr_ref, h_ref, gt_ref, w_ref, lng_ref, lnb_ref, o_ref,
                    *, tm, ctx, alpha, skip, ctx_row):
    b = pl.program_id(0)
    j = pl.program_id(1) + skip

    def project(r0):
        rs = slice(r0, r0 + SUB_ROWS)
        y = jnp.concatenate([ym_ref[rs, :], yl_ref[rs, :], yr_ref[rs, :]], axis=1)
        return jnp.dot(y, w_ref[...], preferred_element_type=F32)

    starts = list(range(0, tm, SUB_ROWS))
    acc_next = project(starts[0])
    for i, r0 in enumerate(starts):
        acc = acc_next
        if i + 1 < len(starts):
            acc_next = project(starts[i + 1])
        rs = slice(r0, r0 + SUB_ROWS)
        rows = j * tm + r0 + lax.broadcasted_iota(jnp.int32, (SUB_ROWS, 1), 0)
        gt = jnp.where(rows < ctx, gt_ref[ctx_row:ctx_row + 1, :], gt_ref[pl.ds(b, 1), :])
        z = alpha * h_ref[rs, :] + gt * acc
        mu = jnp.mean(z, axis=-1, keepdims=True)
        zc = z - mu
        var = jnp.mean(zc * zc, axis=-1, keepdims=True)
        o_ref[rs, :] = zc * lax.rsqrt(var + LN_EPS) * lng_ref[...] + lnb_ref[...]


def _outproj_call(ym, yl, yr, h, mod, w, lng, lnb, l, ctx, tm, alpha, latent_only):
    B, S, D = h.shape
    skip = ctx // tm if latent_only else 0
    assert skip * tm == (ctx if latent_only else 0)
    n_out = S // tm - skip
    return pl.pallas_call(
        functools.partial(_outproj_kernel, tm=tm, ctx=ctx, alpha=alpha, skip=skip, ctx_row=B),
        out_shape=jax.ShapeDtypeStruct((B, n_out * tm, D), F32),
        grid=(B, n_out),
        in_specs=[
            pl.BlockSpec((None, tm, MLA_W), lambda b, j: (b, j + skip, 0)),
            pl.BlockSpec((None, tm, LRU_W), lambda b, j: (b, j + skip, 0)),
            pl.BlockSpec((None, tm, RET_W), lambda b, j: (b, j + skip, 0)),
            pl.BlockSpec((None, tm, D), lambda b, j: (b, j + skip, 0)),
            pl.BlockSpec((None, 8, D), lambda b, j: (l, 0, 2)),
            pl.BlockSpec((None, D, D), lambda b, j: (l, 0, 0), pipeline_mode=pl.Buffered(1)),
            pl.BlockSpec((None, 1, D), lambda b, j: (l, 0, 0)),
            pl.BlockSpec((None, 1, D), lambda b, j: (l, 0, 0)),
        ],
        out_specs=pl.BlockSpec((None, tm, D), lambda b, j: (b, j, 0)),
        compiler_params=_cparams(("parallel", "parallel")),
        name="out_proj_ln",
    )(ym, yl, yr, h, mod, w, lng, lnb)


def _rope_tables(rows, dim, ctx):
    row = jnp.repeat(jnp.arange(rows, dtype=F32), GRID_W)
    col = jnp.tile(jnp.arange(GRID_W, dtype=F32), rows)
    quarter = dim // 4
    inv = ROPE_BASE ** (-jnp.arange(quarter, dtype=F32) / quarter)
    ang = jnp.stack([row[:, None] * inv, col[:, None] * inv], axis=1)
    cos, sin = jnp.cos(ang), jnp.sin(ang)
    pad = ((0, 0), (0, LANES // 2 - 2 * quarter))
    cos_h = jnp.pad(jnp.concatenate([cos[:, 0], cos[:, 1]], axis=-1), pad)
    sin_h = jnp.pad(jnp.concatenate([sin[:, 0], sin[:, 1]], axis=-1), pad)
    cos_l = jnp.concatenate([cos_h, cos_h], axis=-1)
    sin_l = jnp.concatenate([-sin_h, sin_h], axis=-1)
    cos_c = jnp.ones((ctx, LANES), F32)
    sin_c = jnp.zeros((ctx, LANES), F32)
    return jnp.concatenate([cos_c, cos_l], axis=0), jnp.concatenate([sin_c, sin_l], axis=0)


def _rotary_lane_order(dim):
    quarter = dim // 4
    return ((0, 0), (quarter, LANES // 2), (2 * quarter, quarter), (3 * quarter, LANES // 2 + quarter))


def _rotary_moves(src, dst, dim, groups):
    return tuple((src + g * dim + s, dst + g * LANES + d, dim // 4)
                 for g in range(groups) for s, d in _rotary_lane_order(dim))


W_IN_MOVES = ((0, U_QLAT, 512), (512, U_KVLAT, 256), (832, U_MLA_G, 1024),
              (1856, U_LRU_X, 512), (2368, U_LRU_G, 512), (3904, U_RET_V, 512), (4416, U_RET_G, 512)
              ) + _rotary_moves(768, U_KR, MLA_ROPE, 1) + _rotary_moves(2880, U_RET_Q, RET_DH, RET_HEADS
              ) + _rotary_moves(3392, U_RET_K, RET_DH, RET_HEADS)
W_IN_ZERO = tuple((U_KR + o, LANES // 2 - MLA_ROPE // 2) for o in (MLA_ROPE // 2, LANES // 2 + MLA_ROPE // 2)
                  ) + ((U_USED, U_W - U_USED),)


def _w_in_layout_kernel(w_ref, o_ref):
    for src, dst, width in W_IN_MOVES:
        o_ref[dst:dst + width, :] = w_ref[src:src + width, :].astype(BF16)
    for dst, width in W_IN_ZERO:
        o_ref[dst:dst + width, :] = jnp.zeros((width, o_ref.shape[1]), BF16)


def _layout_w_in(w_in):
    L, D, W = w_in.shape
    tc = 512
    return pl.pallas_call(
        _w_in_layout_kernel,
        out_shape=jax.ShapeDtypeStruct((L, U_W, D), BF16),
        grid=(L, D // tc),
        in_specs=[pl.BlockSpec((None, W, tc), lambda l, r: (l, 0, r))],
        out_specs=pl.BlockSpec((None, U_W, tc), lambda l, r: (l, 0, r)),
        compiler_params=_cparams(("parallel", "parallel")),
        name="w_in_layout",
    )(jnp.swapaxes(w_in, 1, 2))


def _layout_w_uq(w_uq):
    L = w_uq.shape[0]
    w = w_uq.reshape(L, MLA_Q_RANK, MLA_HEADS, MLA_NOPE + MLA_ROPE)
    quarter = MLA_ROPE // 4
    pieces, lane = [w[..., :MLA_NOPE]], 0
    for s, d in sorted(_rotary_lane_order(MLA_ROPE), key=lambda sd: sd[1]):
        pieces += [jnp.zeros(w.shape[:-1] + (d - lane,), w.dtype),
                   w[..., MLA_NOPE + s:MLA_NOPE + s + quarter]]
        lane = d + quarter
    pieces.append(jnp.zeros(w.shape[:-1] + (LANES - lane,), w.dtype))
    w = jnp.concatenate(pieces, axis=-1)
    return w.reshape(L, MLA_Q_RANK, MLA_HEADS * MLA_HP).astype(BF16)


def _layout_w_ukv(w_ukv):
    L = w_ukv.shape[0]
    w = w_ukv.reshape(L, MLA_KV_RANK, MLA_HEADS, MLA_NOPE + MLA_V)
    wk = w[..., :MLA_NOPE].reshape(L, MLA_KV_RANK, MLA_W).astype(BF16)
    wvt = w[..., MLA_NOPE:].reshape(L, MLA_KV_RANK, MLA_W).transpose(0, 2, 1).astype(BF16)
    return wk, wvt


def _layout_lru_gates(w_r, b_r, w_i, b_i):
    L = w_r.shape[0]
    ncb = LRU_W // LRU_CB
    per = LRU_BLOCKS // ncb
    eye = jnp.eye(per, dtype=w_r.dtype)

    def diag_blocks(w):
        w = w.reshape(L, 2, ncb, per, LRU_BW, LRU_BW)
        return jnp.einsum("ldcgij,gh->ldcgihj", w, eye).reshape(L, 2, ncb, LRU_CB, LRU_CB)

    wg = jnp.concatenate([diag_blocks(w_r), diag_blocks(w_i)], axis=-1).astype(BF16)
    bg = jnp.concatenate([b_r.reshape(L, 2, ncb, 1, LRU_CB), b_i.reshape(L, 2, ncb, 1, LRU_CB)], axis=-1)
    return 0.5 * wg, 0.5 * bg


def kernel(x, c, ctx, c_ctx, w_ada, b_ada, w_in, mla_q_norm_g, mla_kv_norm_g, mla_w_uq, mla_w_ukv,
           lru_conv_w, lru_conv_b, lru_w_r, lru_b_r, lru_w_i, lru_b_i, lru_lambda, ret_decay,
           w_out, ln_g, ln_b):
    B, T, D = x.shape
    L = w_in.shape[0]
    n_ctx = ctx.shape[1]
    S = n_ctx + T
    assert D == 2 * MLA_W and w_in.shape[2] == 4928 and n_ctx % 256 == 0 and T % 256 == 0
    assert B + 1 <= 8
    alpha = (2 * L) ** 0.25
    tm = 768 if S % 768 == 0 else 256

    cos_m, sin_m = _rope_tables(T // GRID_W, MLA_ROPE, n_ctx)
    cos_r, sin_r = _rope_tables(T // GRID_W, RET_DH, n_ctx)

    cond = jnp.zeros((8, D), F32).at[:B].set(c).at[B].set(c_ctx)
    mod_all = _ada_call(cond, w_ada, b_ada)

    h = _ln0_call(ctx, x)
    log_g = jax.nn.log_sigmoid(ret_decay.astype(F32)).reshape(2 * L, RET_HEADS)

    w_in_l = _layout_w_in(w_in)
    wuq = _layout_w_uq(mla_w_uq)
    wk, wvt = _layout_w_ukv(mla_w_ukv)
    wg, bg = _layout_lru_gates(lru_w_r, lru_b_r, lru_w_i, lru_b_i)
    w_out_b = w_out.astype(BF16)
    gq, gkv = mla_q_norm_g[:, None, :], mla_kv_norm_g[:, None, :]
    conv_b, lng, lnb = lru_conv_b[:, None, :], ln_g[:, None, :], ln_b[:, None, :]

    for l in range(L):
        last = l == L - 1
        u = _inproj_call(h, mod_all, w_in_l, l, n_ctx, tm)
        q, k, vt = _mla_prep_call(u, gq, gkv, wuq, wk, wvt, l, cos_m, sin_m)
        y_mla = _attn_call(q, k, vt, u, n_ctx)
        y_lru = _lru_call(u, lru_conv_w, conv_b, wg, bg, lru_lambda, l, n_ctx)
        y_ret = _ret_call(u, log_g, l, cos_r, sin_r, n_ctx)
        h = _outproj_call(y_mla, y_lru, y_ret, h, mod_all, w_out_b, lng, lnb, l, n_ctx,
                          256 if last else tm, alpha, last)
    return h
```

```python
import functools

import jax
import jax.numpy as jnp
from jax import lax
from jax.experimental import pallas as pl
from jax.experimental.pallas import tpu as pltpu

F32 = jnp.float32
BF16 = jnp.bfloat16

GRID_W = 64
MLA_V = 128
MLA_NOPE = 128
MLA_ROPE = 64
MLA_HEADS = 8
MLA_W = MLA_HEADS * MLA_V
MLA_Q_RANK = 512
MLA_KV_RANK = 256
MLA_SCALE = (MLA_NOPE + MLA_ROPE) ** -0.5
LOG2_E = 1.4426950408889634
MLA_Q_SCALE = MLA_SCALE * LOG2_E
MLA_HP = 256
LRU_W = 512
LRU_BLOCKS = 8
LRU_BW = LRU_W // LRU_BLOCKS
LRU_CONV = 4
LRU_C = 8.0
LRU_CB = 256
RET_HEADS = 4
RET_DH = 128
RET_W = RET_HEADS * RET_DH
RET_CHUNK = 128
RET_K_SCALE = RET_DH ** -0.5
ROPE_BASE = 10000.0
LN_EPS = 1e-5
RMS_EPS = 1e-6
LANES = 128
BF16_ROWS = 16
KEY_CHUNK = 768
ATTN_BUFS = 3
ATTN_HEADS = 2
SUB_ROWS = 256

U_QLAT = 0
U_LRU_X = 512
U_LRU_G = 1024
U_RET_Q = 1536
U_RET_K = 2048
U_RET_V = 2560
U_RET_G = 3072
U_MLA_G = 3584
U_KVLAT = 4608
U_KR = 4864
U_USED = 4992
MXU_COLS = 256
U_CHUNKS = 2
U_TN = -(-U_USED // (U_CHUNKS * MXU_COLS)) * MXU_COLS
U_W = U_CHUNKS * U_TN

VMEM_LIMIT = 56 * 1024 * 1024


def _cparams(sem):
    return pltpu.CompilerParams(dimension_semantics=sem, vmem_limit_bytes=VMEM_LIMIT)


def _dot_nt(a, b):
    return lax.dot_general(a, b, (((1,), (1,)), ((), ())), preferred_element_type=F32)


def _sigmoid(x):
    return 0.5 * jnp.tanh(0.5 * x) + 0.5


def _silu(x):
    return x * _sigmoid(x)


def _softplus(x):
    return jnp.maximum(x, 0.0) + jnp.log1p(jnp.exp(-jnp.abs(x)))


def _rope(x, cos, sin):
    return x * cos + pltpu.roll(x, LANES // 2, axis=x.ndim - 1) * sin


def _ada_kernel(cond_ref, w_ref, b_ref, o_ref):
    c = cond_ref[...]
    a = _silu(c).astype(BF16)
    o_ref[...] = jnp.dot(a, w_ref[...].astype(BF16), preferred_element_type=F32) + b_ref[...]


def _ada_call(cond, w_ada, b_ada):
    L, D, D3 = w_ada.shape
    tn = 1024
    return pl.pallas_call(
        _ada_kernel,
        out_shape=jax.ShapeDtypeStruct((L, 8, D3), F32),
        grid=(L, D3 // tn),
        in_specs=[
            pl.BlockSpec((8, D), lambda l, n: (0, 0)),
            pl.BlockSpec((None, D, tn), lambda l, n: (l, 0, n)),
            pl.BlockSpec((None, 1, tn), lambda l, n: (l, 0, n)),
        ],
        out_specs=pl.BlockSpec((None, 8, tn), lambda l, n: (l, 0, n)),
        compiler_params=_cparams(("parallel", "parallel")),
        name="ada_mod",
    )(cond, w_ada, b_ada.reshape(L, 1, D3))


def _ln0_kernel(c_ref, x_ref, o_ref, *, n_ctx_blocks):
    def norm(ref):
        x = ref[...]
        mu = jnp.mean(x, axis=-1, keepdims=True)
        xc = x - mu
        var = jnp.mean(xc * xc, axis=-1, keepdims=True)
        o_ref[...] = xc * lax.rsqrt(var + LN_EPS)

    j = pl.program_id(1)
    pl.when(j < n_ctx_blocks)(lambda: norm(c_ref))
    pl.when(j >= n_ctx_blocks)(lambda: norm(x_ref))


def _ln0_call(ctx, x):
    B, T, D = x.shape
    n_ctx = ctx.shape[1]
    tm = 256
    nc = n_ctx // tm
    return pl.pallas_call(
        functools.partial(_ln0_kernel, n_ctx_blocks=nc),
        out_shape=jax.ShapeDtypeStruct((B, n_ctx + T, D), F32),
        grid=(B, (n_ctx + T) // tm),
        in_specs=[pl.BlockSpec((None, tm, D), lambda b, j: (b, jnp.minimum(j, nc - 1), 0)),
                  pl.BlockSpec((None, tm, D), lambda b, j: (b, jnp.maximum(j - nc, 0), 0))],
        out_specs=pl.BlockSpec((None, tm, D), lambda b, j: (b, j, 0)),
        compiler_params=_cparams(("parallel", "parallel")),
        name="ln_entry",
    )(ctx, x)


def _inproj_kernel(h_ref, sh_ref, sc_ref, *rest, tm, ctx, ctx_row):
    w_refs, (u_ref, xs_ref) = rest[:U_CHUNKS], rest[U_CHUNKS:]
    b = pl.program_id(0)
    j = pl.program_id(1)
    n = pl.program_id(2)

    @pl.when(n == 0)
    def _():
        for r0 in range(0, tm, SUB_ROWS):
            rs = slice(r0, r0 + SUB_ROWS)
            rows = j * tm + r0 + lax.broadcasted_iota(jnp.int32, (SUB_ROWS, 1), 0)
            is_ctx = rows < ctx
            sh = jnp.where(is_ctx, sh_ref[ctx_row:ctx_row + 1, :], sh_ref[pl.ds(b, 1), :])
            sc = jnp.where(is_ctx, sc_ref[ctx_row:ctx_row + 1, :], sc_ref[pl.ds(b, 1), :])
            xs = (h_ref[rs, :] * (1.0 + sc) + sh).astype(BF16)
            xs_ref[rs, :] = xs
            u_ref[rs, :] = _dot_nt(xs, w_refs[0][...]).astype(u_ref.dtype)

    for c in range(1, U_CHUNKS):
        @pl.when(n == c)
        def _():
            u_ref[...] = _dot_nt(xs_ref[...], w_refs[c][...]).astype(u_ref.dtype)


def _inproj_call(h, mod_all, w_t, l, ctx, tm):
    B, S, D = h.shape
    return pl.pallas_call(
        functools.partial(_inproj_kernel, tm=tm, ctx=ctx, ctx_row=B),
        out_shape=jax.ShapeDtypeStruct((B, S, U_W), F32),
        grid=(B, S // tm, U_CHUNKS),
        in_specs=[
            pl.BlockSpec((None, tm, D), lambda b, j, n: (b, j, 0)),
            pl.BlockSpec((None, 8, D), lambda b, j, n: (l, 0, 0)),
            pl.BlockSpec((None, 8, D), lambda b, j, n: (l, 0, 1)),
        ] + [
            pl.BlockSpec((U_TN, D), functools.partial(lambda b, j, n, c: (c, 0), c=c),
                         pipeline_mode=pl.Buffered(1))
            for c in range(U_CHUNKS)
        ],
        out_specs=pl.BlockSpec((None, tm, U_TN), lambda b, j, n: (b, j, n)),
        scratch_shapes=[pltpu.VMEM((tm, D), BF16)],
        compiler_params=_cparams(("parallel", "parallel", "arbitrary")),
        name="in_proj",
    )(h, mod_all, mod_all, *([w_t] * U_CHUNKS))


def _rms(x, g):
    return x * lax.rsqrt(jnp.mean(x * x, axis=-1, keepdims=True) + RMS_EPS) * g


def _mla_prep_kernel(ql_ref, kvl_ref, kr_ref, gq_ref, gkv_ref, wuq_ref, wk_ref, wvt_ref,
                     cos_ref, sin_ref, q_ref, k_ref, vt_ref):
    cos = cos_ref[...]
    sin = sin_ref[...]

    zq = _rms(ql_ref[...].astype(F32), gq_ref[...]).astype(BF16)
    q = jnp.dot(zq, wuq_ref[...], preferred_element_type=F32)
    for hd in range(MLA_HEADS):
        c0 = hd * MLA_HP
        q_ref[:, c0:c0 + MLA_NOPE] = (q[:, c0:c0 + MLA_NOPE] * MLA_Q_SCALE).astype(q_ref.dtype)
        qr = _rope(q[:, c0 + MLA_NOPE:c0 + MLA_HP], cos, sin)
        q_ref[:, c0 + MLA_NOPE:c0 + MLA_HP] = (qr * MLA_Q_SCALE).astype(q_ref.dtype)

    zk = _rms(kvl_ref[...].astype(F32), gkv_ref[...]).astype(BF16)
    kn = jnp.dot(zk, wk_ref[...], preferred_element_type=F32)
    kr = _rope(kr_ref[...].astype(F32), cos, sin).astype(k_ref.dtype)
    for hd in range(MLA_HEADS):
        c0 = hd * MLA_HP
        k_ref[:, c0:c0 + MLA_NOPE] = kn[:, hd * MLA_NOPE:(hd + 1) * MLA_NOPE].astype(k_ref.dtype)
        k_ref[:, c0 + MLA_NOPE:c0 + MLA_HP] = kr
    vt = lax.dot_general(wvt_ref[...], zk, (((1,), (1,)), ((), ())), preferred_element_type=F32)
    vt_ref[...] = vt.astype(vt_ref.dtype)


def _mla_prep_call(u, gq, gkv, wuq, wk, wvt, l, cos, sin):
    B, S, _ = u.shape
    tp = 768 if S % 768 == 0 else 256
    HW = MLA_HEADS * MLA_HP
    return pl.pallas_call(
        _mla_prep_kernel,
        out_shape=(
            jax.ShapeDtypeStruct((B, S, HW), BF16),
            jax.ShapeDtypeStruct((B, S, HW), BF16),
            jax.ShapeDtypeStruct((B, MLA_W, S), BF16),
        ),
        grid=(B, S // tp),
        in_specs=[
            pl.BlockSpec((None, tp, MLA_Q_RANK), lambda b, j: (b, j, U_QLAT // MLA_Q_RANK)),
            pl.BlockSpec((None, tp, MLA_KV_RANK), lambda b, j: (b, j, U_KVLAT // MLA_KV_RANK)),
            pl.BlockSpec((None, tp, LANES), lambda b, j: (b, j, U_KR // LANES)),
            pl.BlockSpec((None, 1, MLA_Q_RANK), lambda b, j: (l, 0, 0)),
            pl.BlockSpec((None, 1, MLA_KV_RANK), lambda b, j: (l, 0, 0)),
            pl.BlockSpec((None, MLA_Q_RANK, HW), lambda b, j: (l, 0, 0)),
            pl.BlockSpec((None, MLA_KV_RANK, MLA_W), lambda b, j: (l, 0, 0)),
            pl.BlockSpec((None, MLA_W, MLA_KV_RANK), lambda b, j: (l, 0, 0)),
            pl.BlockSpec((tp, LANES), lambda b, j: (j, 0)),
            pl.BlockSpec((tp, LANES), lambda b, j: (j, 0)),
        ],
        out_specs=(
            pl.BlockSpec((None, tp, HW), lambda b, j: (b, j, 0)),
            pl.BlockSpec((None, tp, HW), lambda b, j: (b, j, 0)),
            pl.BlockSpec((None, MLA_W, tp), lambda b, j: (b, 0, j)),
        ),
        compiler_params=_cparams(("parallel", "parallel")),
        name="mla_prep",
    )(u, u, u, gq, gkv, wuq, wk, wvt, cos, sin)


def _attn_kernel(*refs, ctx, tq, heads, relayout):
    if relayout:
        q_ref, k_ref, vt_ref, g_ref, w_src_ref, y_ref, w_dst_ref = refs[:7]
        _w_in_layout_kernel(w_src_ref, w_dst_ref)
    else:
        q_ref, k_ref, vt_ref, g_ref, y_ref = refs[:5]
    st_ref, m_ref, p_ref, vt1_ref = refs[-4:]
    S = k_ref.shape[0]
    n_lat = S // tq - 1
    assert n_lat >= 2 and S % KEY_CHUNK == 0
    items = [(hd, blk) for hd in range(heads) for blk in range(1, n_lat + 1)]

    for hd in range(heads):
        vt1_ref[hd, 0:MLA_V, :] = vt_ref[hd * MLA_V:(hd + 1) * MLA_V, :]
        vt1_ref[hd, MLA_V:, :] = jnp.ones((BF16_ROWS, S), BF16)

    def rows_of(blk):
        return slice(blk * tq, (blk + 1) * tq)

    def qk_cols(hd):
        return slice(hd * MLA_HP, (hd + 1) * MLA_HP)

    def numerators(st, m):
        return jnp.exp2(st - m).astype(BF16)

    def write_out(hd, blk, acc):
        rows, cols = rows_of(blk), slice(hd * MLA_V, (hd + 1) * MLA_V)
        o = (acc[0:MLA_V, :] / acc[MLA_V:MLA_V + 1, :]).T
        y_ref[rows, cols] = (o * _silu(g_ref[rows, cols].astype(F32))).astype(y_ref.dtype)

    def step(score=None, num=None, att=None):
        def pick(i):
            return (None, None, None) if i is None or i >= len(items) else (i % ATTN_BUFS,) + items[i]

        (sb, s_hd, s_blk), (nb, _, _), (ab, a_hd, a_blk) = pick(score), pick(num), pick(att)
        qb = None if sb is None else q_ref[rows_of(s_blk), qk_cols(s_hd)]
        m_num = None if nb is None else m_ref[nb]
        m_run, acc = None, None
        for c0 in range(0, S, KEY_CHUNK):
            ks = slice(c0, c0 + KEY_CHUNK)
            if nb is not None:
                p_ref[nb, ks, :] = numerators(st_ref[nb, ks, :], m_num)
            if sb is not None:
                st = _dot_nt(k_ref[ks, qk_cols(s_hd)], qb)
                st_ref[sb, ks, :] = st
                m_c = jnp.max(st, axis=0, keepdims=True)
                m_run = m_c if m_run is None else jnp.maximum(m_run, m_c)
            if ab is not None:
                part = jnp.dot(vt1_ref[a_hd, :, ks], p_ref[ab, ks, :], preferred_element_type=F32)
                acc = part if acc is None else acc + part
        if sb is not None:
            m_ref[sb] = m_run
        if ab is not None:
            write_out(a_hd, a_blk, acc)

    step(score=0)
    step(score=1, num=0)
    for hd in range(heads):
        st = _dot_nt(k_ref[0:ctx, qk_cols(hd)], q_ref[rows_of(0), qk_cols(hd)])
        p = numerators(st, jnp.max(st, axis=0, keepdims=True))
        write_out(hd, 0, jnp.dot(vt1_ref[hd, :, 0:ctx], p, preferred_element_type=F32))
    for t in range(len(items)):
        step(score=t + 2, num=t + 1, att=t)


def _attn_call(q, k, vt, u, ctx, w_in_t=None, next_layer=None):
    B, S, _ = q.shape
    tq = 256
    assert ctx == tq
    nh = ATTN_HEADS
    gcol = U_MLA_G // (nh * MLA_V)
    assert MLA_HEADS % nh == 0 and U_MLA_G % (nh * MLA_V) == 0
    n_h = MLA_HEADS // nh
    in_specs = [
        pl.BlockSpec((None, S, nh * MLA_HP), lambda b, h: (b, 0, h)),
        pl.BlockSpec((None, S, nh * MLA_HP), lambda b, h: (b, 0, h)),
        pl.BlockSpec((None, nh * MLA_V, S), lambda b, h: (b, h, 0)),
        pl.BlockSpec((None, S, nh * MLA_V), lambda b, h: (b, 0, gcol + h)),
    ]
    out_shape = [jax.ShapeDtypeStruct((B, S, MLA_W), BF16)]
    out_specs = [pl.BlockSpec((None, S, nh * MLA_V), lambda b, h: (b, 0, h))]
    args = [q, k, vt, u]
    relayout = w_in_t is not None
    if relayout:
        _, W, D = w_in_t.shape
        tc = D // (B * n_h)
        assert tc % LANES == 0
        in_specs.append(pl.BlockSpec((None, W, tc), lambda b, h: (next_layer, 0, b * n_h + h)))
        out_shape.append(jax.ShapeDtypeStruct((U_W, D), BF16))
        out_specs.append(pl.BlockSpec((U_W, tc), lambda b, h: (0, b * n_h + h)))
        args.append(w_in_t)
    out = pl.pallas_call(
        functools.partial(_attn_kernel, ctx=ctx, tq=tq, heads=nh, relayout=relayout),
        out_shape=out_shape,
        grid=(B, n_h),
        in_specs=in_specs,
        out_specs=out_specs,
        scratch_shapes=[pltpu.VMEM((ATTN_BUFS, S, tq), F32), pltpu.VMEM((ATTN_BUFS, 1, tq), F32),
                        pltpu.VMEM((ATTN_BUFS, S, tq), BF16),
                        pltpu.VMEM((nh, MLA_V + BF16_ROWS, S), BF16)],
        compiler_params=_cparams(("parallel", "parallel")),
        name="mla_attn",
    )(*args)
    return (out[0], out[1]) if relayout else (out[0], None)


def _lru_kernel(x_ref, g_ref, cw_ref, cb_ref, wg_ref, bg_ref, lam_ref, y_ref,
                af_ref, bf_ref, ab_ref, bb_ref, hlf_ref, plf_ref, hlb_ref, plb_ref, *, ctx):
    S, CB = x_ref.shape
    NG = CB // LANES
    x = x_ref[...].astype(F32)
    rows = lax.broadcasted_iota(jnp.int32, (S, 1), 0)
    in_ctx = rows < ctx

    xc = x * cw_ref[2:3, :] + cb_ref[...]
    for tap, off in ((0, -2), (1, -1), (3, 1)):
        xs = pltpu.roll(x, (-off) % S, axis=0)
        src = rows + off
        valid = (src >= 0) & (src < S) & ((src < ctx) == in_ctx)
        xc = xc + jnp.where(valid, xs, 0.0) * cw_ref[tap:tap + 1, :]

    xb = xc.astype(BF16)
    xh = 0.5 * xc
    for d, (a_ref, b_ref) in enumerate(((af_ref, bf_ref), (ab_ref, bb_ref))):
        t = jnp.tanh(jnp.dot(xb, wg_ref[d], preferred_element_type=F32) + bg_ref[d])
        t_r, t_i = t[:, :CB], t[:, CB:]
        c2 = (-0.5 * LRU_C * LOG2_E) * _softplus(-lam_ref[d:d + 1, :])
        a = jnp.exp2(t_r * c2 + c2)
        w = (1.0 - a) * (1.0 + a)
        b = jnp.where(w > 0.0, w * lax.rsqrt(w), 0.0) * (t_i * xh + xh)
        for g in range(NG):
            gs = slice(g * LANES, (g + 1) * LANES)
            a_ref[g] = a[:, gs]
            b_ref[g] = b[:, gs]

    R = 8
    zeros, ones = jnp.zeros((R, LANES), F32), jnp.ones((R, LANES), F32)
    row = lax.broadcasted_iota(jnp.int32, (R, LANES), 0)

    def tile_scan(a, b, enter, rev):
        for d in (1, 2, 4):
            shift = R - d if rev else d
            keep = (row < R - d) if rev else (row >= d)
            a_s = pltpu.roll(a, shift, axis=0)
            b_s = pltpu.roll(b, shift, axis=0)
            b = jnp.where(keep, a * b_s, 0.0) + b
            a = jnp.where(keep, a * a_s, a)
        return a * enter + b

    def scan_part(base, n_rows, enter):
        seg = n_rows // R - 1
        assert seg % 2 == 1
        last = pl.ds(base + R * seg, R)

        def strided(i):
            return pl.ds(base + i, R, stride=seg)

        def packed(i):
            return pl.ds(pl.multiple_of(base + i * R, R), R)

        def local(i, carry):
            ib = seg - 1 - i
            out = []
            for g, (hf, pf, hb, pb) in enumerate(carry):
                a = af_ref[g, strided(i), :]
                hf = a * hf + bf_ref[g, strided(i), :]
                pf = a * pf
                hlf_ref[g, packed(i), :] = hf
                plf_ref[g, packed(i), :] = pf
                a = ab_ref[g, strided(ib), :]
                hb = a * hb + bb_ref[g, strided(ib), :]
                pb = a * pb
                hlb_ref[g, packed(ib), :] = hb
                plb_ref[g, packed(ib), :] = pb
                out.append((hf, pf, hb, pb))
            return tuple(out)

        ends = lax.fori_loop(0, seg, local, ((zeros, ones, zeros, ones),) * NG, unroll=4)

        leave, fixes = [], []
        for g, ((hf, pf, hb, pb), (c_f, c_b)) in enumerate(zip(ends, enter)):
            rows_f, rows_b = [], [None] * R
            for s in range(R):
                rows_f.append(c_f)
                c_f = hf[s:s + 1, :] + pf[s:s + 1, :] * c_f
            tail_f = tile_scan(af_ref[g, last, :], bf_ref[g, last, :], c_f, False)
            tail_b = tile_scan(ab_ref[g, last, :], bb_ref[g, last, :], c_b, True)
            c_f, c_b = tail_f[R - 1:R, :], tail_b[0:1, :]
            for s in reversed(range(R)):
                rows_b[s] = c_b
                c_b = hb[s:s + 1, :] + pb[s:s + 1, :] * c_b
            af_ref[g, last, :] = tail_f
            ab_ref[g, last, :] = tail_b
            leave.append((c_f, c_b))
            fixes.append((jnp.concatenate(rows_f, axis=0), jnp.concatenate(rows_b, axis=0)))

        def fix(i, carry):
            for g, (cf, cb) in enumerate(fixes):
                af_ref[g, strided(i), :] = hlf_ref[g, packed(i), :] + plf_ref[g, packed(i), :] * cf
                ab_ref[g, strided(i), :] = hlb_ref[g, packed(i), :] + plb_ref[g, packed(i), :] * cb
            return carry

        lax.fori_loop(0, seg, fix, 0, unroll=4)
        return leave

    row0 = jnp.zeros((1, LANES), F32)
    scan_part(ctx, S - ctx, scan_part(0, ctx, [(row0, row0)] * NG))

    g_all = g_ref[...].astype(F32)
    for g in range(NG):
        gs = slice(g * LANES, (g + 1) * LANES)
        y_ref[:, gs] = ((af_ref[g] + ab_ref[g]) * _silu(g_all[:, gs])).astype(y_ref.dtype)


def _lru_call(u, cw, cb, wg, bg, lam, l, ctx):
    B, S, _ = u.shape
    ncb = LRU_W // LRU_CB
    return pl.pallas_call(
        functools.partial(_lru_kernel, ctx=ctx),
        out_shape=jax.ShapeDtypeStruct((B, S, LRU_W), BF16),
        grid=(B, ncb),
        in_specs=[
            pl.BlockSpec((None, S, LRU_CB), lambda b, c: (b, 0, U_LRU_X // LRU_CB + c)),
            pl.BlockSpec((None, S, LRU_CB), lambda b, c: (b, 0, U_LRU_G // LRU_CB + c)),
            pl.BlockSpec((None, LRU_CONV, LRU_CB), lambda b, c: (l, 0, c)),
            pl.BlockSpec((None, 1, LRU_CB), lambda b, c: (l, 0, c)),
            pl.BlockSpec((None, 2, None, LRU_CB, 2 * LRU_CB), lambda b, c: (l, 0, c, 0, 0)),
            pl.BlockSpec((None, 2, None, 1, 2 * LRU_CB), lambda b, c: (l, 0, c, 0, 0)),
            pl.BlockSpec((None, 2, LRU_CB), lambda b, c: (l, 0, c)),
        ],
        out_specs=pl.BlockSpec((None, S, LRU_CB), lambda b, c: (b, 0, c)),
        scratch_shapes=[pltpu.VMEM((LRU_CB // LANES, S, LANES), F32) for _ in range(8)],
        compiler_params=_cparams(("parallel", "parallel")),
        name="rg_lru",
    )(u, u, cw, cb, wg, bg, lam)


def _ret_kernel(lg_ref, q_ref, k_ref, v_ref, g_ref, cos_ref, sin_ref, wo_src_ref, y_ref, wo_dst_ref,
                ks_ref, kv_ref, st_ref, *, n_ctx, layer):
    wo_dst_ref[...] = wo_src_ref[...].astype(BF16)
    C = RET_CHUNK
    S = q_ref.shape[0]
    n_all = S // C
    unroll = next(f for f in (9, 6, 3, 2, 1) if n_all % f == 0)
    hd = pl.program_id(1)
    lgf = lg_ref[2 * layer, hd]
    lgb = lg_ref[2 * layer + 1, hd]

    ri = lax.broadcasted_iota(jnp.int32, (C, C), 0).astype(F32)
    ci = lax.broadcasted_iota(jnp.int32, (C, C), 1).astype(F32)
    diff = ri - ci
    decay = jnp.where(diff >= 0, jnp.exp(jnp.maximum(diff, 0.0) * lgf),
                      jnp.exp(jnp.maximum(-diff, 0.0) * lgb))
    zeta_f = jnp.exp((C - 1 - ri) * lgf)
    xi_f = jnp.exp((ri + 1) * lgf)
    gc_f = jnp.exp(jnp.full((C, C), C, F32) * lgf)
    zeta_b = jnp.exp(ri * lgb)
    xi_b = jnp.exp((C - ri) * lgb)
    gc_b = jnp.exp(jnp.full((C, C), C, F32) * lgb)
    def rows_of(c):
        return pl.ds(pl.multiple_of(c * C, C), C)

    def chunk_kv(c, carry):
        sl = rows_of(c)
        k = _rope(k_ref[sl, :].astype(F32), cos_ref[sl, :], sin_ref[sl, :]) * RET_K_SCALE
        ks_ref[sl, :] = k.astype(BF16)
        kz = jnp.concatenate([k * zeta_f, k * zeta_b], axis=1).astype(BF16)
        kv_ref[c] = lax.dot_general(v_ref[sl, :].astype(BF16), kz, (((0,), (0,)), ((), ())),
                                    preferred_element_type=F32)
        return carry

    lax.fori_loop(0, n_all, chunk_kv, 0, unroll=unroll)

    def fwd_state(c, r):
        st_ref[c, :, 0:C] = r.astype(BF16)
        return gc_f * r + kv_ref[c, :, 0:C]

    def bwd_state(i, r):
        c = jnp.where(i < n_ctx, n_ctx - 1 - i, n_all - 1 - (i - n_ctx))
        st_ref[c, :, C:2 * C] = r.astype(BF16)
        return gc_b * r + kv_ref[c, :, C:2 * C]

    lax.fori_loop(0, n_all, fwd_state, jnp.zeros((C, C), F32))
    lax.fori_loop(0, n_all, bwd_state, jnp.zeros((C, C), F32))

    group = next(f for f in (6, 3, 2, 1) if n_all % f == 0)

    def chunks_out(i, carry):
        cs = [i * group + g for g in range(group)]
        sls = [rows_of(c) for c in cs]
        qs = [_rope(q_ref[sl, :].astype(F32), cos_ref[sl, :], sin_ref[sl, :]) for sl in sls]
        ss = [_dot_nt(q.astype(BF16), ks_ref[sl, :]) for q, sl in zip(qs, sls)]
        qxs = [jnp.concatenate([q * xi_f, q * xi_b], axis=1).astype(BF16) for q in qs]
        os = [jnp.dot((s * decay).astype(BF16), v_ref[sl, :].astype(BF16), preferred_element_type=F32)
              + _dot_nt(qx, st_ref[c]) for s, sl, qx, c in zip(ss, sls, qxs, cs)]
        for o, sl in zip(os, sls):
            mu = jnp.mean(o, axis=-1, keepdims=True)
            oc = o - mu
            var = jnp.mean(oc * oc, axis=-1, keepdims=True)
            y = oc * lax.rsqrt(var + LN_EPS) * _silu(g_ref[sl, :].astype(F32))
            y_ref[sl, :] = y.astype(y_ref.dtype)
        return carry

    lax.fori_loop(0, n_all // group, chunks_out, 0)


def _ret_call(u, log_g, l, cos, sin, ctx, w_out):
    B, S, _ = u.shape
    n_all = S // RET_CHUNK
    D = w_out.shape[1]
    tc = D // (B * RET_HEADS)
    assert tc % LANES == 0

    def col(base):
        return lambda b, h: (b, 0, base // RET_DH + h)

    return pl.pallas_call(
        functools.partial(_ret_kernel, n_ctx=ctx // RET_CHUNK, layer=l),
        out_shape=(jax.ShapeDtypeStruct((B, S, RET_W), BF16), jax.ShapeDtypeStruct((D, D), BF16)),
        grid=(B, RET_HEADS),
        in_specs=[
            pl.BlockSpec(memory_space=pltpu.SMEM),
            pl.BlockSpec((None, S, RET_DH), col(U_RET_Q)),
            pl.BlockSpec((None, S, RET_DH), col(U_RET_K)),
            pl.BlockSpec((None, S, RET_DH), col(U_RET_V)),
            pl.BlockSpec((None, S, RET_DH), col(U_RET_G)),
            pl.BlockSpec((S, RET_DH), lambda b, h: (0, 0)),
            pl.BlockSpec((S, RET_DH), lambda b, h: (0, 0)),
            pl.BlockSpec((None, D, tc), lambda b, h: (l, 0, b * RET_HEADS + h)),
        ],
        out_specs=(pl.BlockSpec((None, S, RET_DH), lambda b, h: (b, 0, h)),
                   pl.BlockSpec((D, tc), lambda b, h: (0, b * RET_HEADS + h))),
        scratch_shapes=[pltpu.VMEM((S, RET_DH), BF16),
                        pltpu.VMEM((n_all, RET_DH, 2 * RET_DH), F32),
                        pltpu.VMEM((n_all, RET_DH, 2 * RET_DH), BF16)],
        compiler_params=_cparams(("parallel", "parallel")),
        name="retention",
    )(log_g, u, u, u, u, cos, sin, w_out)


def _outproj_kernel(ym_ref, yl_ref, yr_ref, h_ref, gt_ref, w_ref, lng_ref, lnb_ref, o_ref,
                    *, tm, ctx, alpha, skip, ctx_row):
    b = pl.program_id(0)
    j = pl.program_id(1) + skip

    def project(r0):
        rs = slice(r0, r0 + SUB_ROWS)
        y = jnp.concatenate([ym_ref[rs, :], yl_ref[rs, :], yr_ref[rs, :]], axis=1)
        return jnp.dot(y, w_ref[...], preferred_element_type=F32)

    starts = list(range(0, tm, SUB_ROWS))
    acc_next = project(starts[0])
    for i, r0 in enumerate(starts):
        acc = acc_next
        if i + 1 < len(starts):
            acc_next = project(starts[i + 1])
        rs = slice(r0, r0 + SUB_ROWS)
        rows = j * tm + r0 + lax.broadcasted_iota(jnp.int32, (SUB_ROWS, 1), 0)
        gt = jnp.where(rows < ctx, gt_ref[ctx_row:ctx_row + 1, :], gt_ref[pl.ds(b, 1), :])
        z = alpha * h_ref[rs, :] + gt * acc
        mu = jnp.mean(z, axis=-1, keepdims=True)
        zc = z - mu
        var = jnp.mean(zc * zc, axis=-1, keepdims=True)
        o_ref[rs, :] = zc * lax.rsqrt(var + LN_EPS) * lng_ref[...] + lnb_ref[...]


def _outproj_call(ym, yl, yr, h, mod, w, lng, lnb, l, ctx, tm, alpha, latent_only):
    B, S, D = h.shape
    skip = ctx // tm if latent_only else 0
    assert skip * tm == (ctx if latent_only else 0)
    n_out = S // tm - skip
    return pl.pallas_call(
        functools.partial(_outproj_kernel, tm=tm, ctx=ctx, alpha=alpha, skip=skip, ctx_row=B),
        out_shape=jax.ShapeDtypeStruct((B, n_out * tm, D), F32),
        grid=(B, n_out),
        in_specs=[
            pl.BlockSpec((None, tm, MLA_W), lambda b, j: (b, j + skip, 0)),
            pl.BlockSpec((None, tm, LRU_W), lambda b, j: (b, j + skip, 0)),
            pl.BlockSpec((None, tm, RET_W), lambda b, j: (b, j + skip, 0)),
            pl.BlockSpec((None, tm, D), lambda b, j: (b, j + skip, 0)),
            pl.BlockSpec((None, 8, D), lambda b, j: (l, 0, 2)),
            pl.BlockSpec((D, D), lambda b, j: (0, 0), pipeline_mode=pl.Buffered(1)),
            pl.BlockSpec((None, 1, D), lambda b, j: (l, 0, 0)),
            pl.BlockSpec((None, 1, D), lambda b, j: (l, 0, 0)),
        ],
        out_specs=pl.BlockSpec((None, tm, D), lambda b, j: (b, j, 0)),
        compiler_params=_cparams(("parallel", "parallel")),
        name="out_proj_ln",
    )(ym, yl, yr, h, mod, w, lng, lnb)


def _rope_tables(rows, dim, ctx):
    row = jnp.repeat(jnp.arange(rows, dtype=F32), GRID_W)
    col = jnp.tile(jnp.arange(GRID_W, dtype=F32), rows)
    quarter = dim // 4
    inv = ROPE_BASE ** (-jnp.arange(quarter, dtype=F32) / quarter)
    ang = jnp.stack([row[:, None] * inv, col[:, None] * inv], axis=1)
    cos, sin = jnp.cos(ang), jnp.sin(ang)
    pad = ((0, 0), (0, LANES // 2 - 2 * quarter))
    cos_h = jnp.pad(jnp.concatenate([cos[:, 0], cos[:, 1]], axis=-1), pad)
    sin_h = jnp.pad(jnp.concatenate([sin[:, 0], sin[:, 1]], axis=-1), pad)
    cos_l = jnp.concatenate([cos_h, cos_h], axis=-1)
    sin_l = jnp.concatenate([-sin_h, sin_h], axis=-1)
    cos_c = jnp.ones((ctx, LANES), F32)
    sin_c = jnp.zeros((ctx, LANES), F32)
    return jnp.concatenate([cos_c, cos_l], axis=0), jnp.concatenate([sin_c, sin_l], axis=0)


def _rotary_lane_order(dim):
    quarter = dim // 4
    return ((0, 0), (quarter, LANES // 2), (2 * quarter, quarter), (3 * quarter, LANES // 2 + quarter))


def _rotary_moves(src, dst, dim, groups):
    return tuple((src + g * dim + s, dst + g * LANES + d, dim // 4)
                 for g in range(groups) for s, d in _rotary_lane_order(dim))


W_IN_MOVES = ((0, U_QLAT, 512), (512, U_KVLAT, 256), (832, U_MLA_G, 1024),
              (1856, U_LRU_X, 512), (2368, U_LRU_G, 512), (3904, U_RET_V, 512), (4416, U_RET_G, 512)
              ) + _rotary_moves(768, U_KR, MLA_ROPE, 1) + _rotary_moves(2880, U_RET_Q, RET_DH, RET_HEADS
              ) + _rotary_moves(3392, U_RET_K, RET_DH, RET_HEADS)
W_IN_ZERO = tuple((U_KR + o, LANES // 2 - MLA_ROPE // 2) for o in (MLA_ROPE // 2, LANES // 2 + MLA_ROPE // 2)
                  ) + ((U_USED, U_W - U_USED),)


def _w_in_layout_kernel(w_ref, o_ref):
    for src, dst, width in W_IN_MOVES:
        o_ref[dst:dst + width, :] = w_ref[src:src + width, :].astype(BF16)
    for dst, width in W_IN_ZERO:
        o_ref[dst:dst + width, :] = jnp.zeros((width, o_ref.shape[1]), BF16)


def _layout_w_in(w_in_t, layer):
    _, W, D = w_in_t.shape
    tc = 512
    return pl.pallas_call(
        _w_in_layout_kernel,
        out_shape=jax.ShapeDtypeStruct((U_W, D), BF16),
        grid=(D // tc,),
        in_specs=[pl.BlockSpec((None, W, tc), lambda r: (layer, 0, r))],
        out_specs=pl.BlockSpec((U_W, tc), lambda r: (0, r)),
        compiler_params=_cparams(("parallel",)),
        name="w_in_layout",
    )(w_in_t)


def _layout_w_uq(w_uq):
    L = w_uq.shape[0]
    w = w_uq.reshape(L, MLA_Q_RANK, MLA_HEADS, MLA_NOPE + MLA_ROPE)
    quarter = MLA_ROPE // 4
    pieces, lane = [w[..., :MLA_NOPE]], 0
    for s, d in sorted(_rotary_lane_order(MLA_ROPE), key=lambda sd: sd[1]):
        pieces += [jnp.zeros(w.shape[:-1] + (d - lane,), w.dtype),
                   w[..., MLA_NOPE + s:MLA_NOPE + s + quarter]]
        lane = d + quarter
    pieces.append(jnp.zeros(w.shape[:-1] + (LANES - lane,), w.dtype))
    w = jnp.concatenate(pieces, axis=-1)
    return w.reshape(L, MLA_Q_RANK, MLA_HEADS * MLA_HP).astype(BF16)


def _layout_w_ukv(w_ukv):
    L = w_ukv.shape[0]
    w = w_ukv.reshape(L, MLA_KV_RANK, MLA_HEADS, MLA_NOPE + MLA_V)
    wk = w[..., :MLA_NOPE].reshape(L, MLA_KV_RANK, MLA_W).astype(BF16)
    wvt = w[..., MLA_NOPE:].reshape(L, MLA_KV_RANK, MLA_W).transpose(0, 2, 1).astype(BF16)
    return wk, wvt


def _layout_lru_gates(w_r, b_r, w_i, b_i):
    L = w_r.shape[0]
    ncb = LRU_W // LRU_CB
    per = LRU_BLOCKS // ncb
    eye = jnp.eye(per, dtype=w_r.dtype)

    def diag_blocks(w):
        w = w.reshape(L, 2, ncb, per, LRU_BW, LRU_BW)
        return jnp.einsum("ldcgij,gh->ldcgihj", w, eye).reshape(L, 2, ncb, LRU_CB, LRU_CB)

    wg = jnp.concatenate([diag_blocks(w_r), diag_blocks(w_i)], axis=-1).astype(BF16)
    bg = jnp.concatenate([b_r.reshape(L, 2, ncb, 1, LRU_CB), b_i.reshape(L, 2, ncb, 1, LRU_CB)], axis=-1)
    return 0.5 * wg, 0.5 * bg


def kernel(x, c, ctx, c_ctx, w_ada, b_ada, w_in, mla_q_norm_g, mla_kv_norm_g, mla_w_uq, mla_w_ukv,
           lru_conv_w, lru_conv_b, lru_w_r, lru_b_r, lru_w_i, lru_b_i, lru_lambda, ret_decay,
           w_out, ln_g, ln_b):
    B, T, D = x.shape
    L = w_in.shape[0]
    n_ctx = ctx.shape[1]
    S = n_ctx + T
    assert D == 2 * MLA_W and w_in.shape[2] == 4928 and n_ctx % 256 == 0 and T % 256 == 0
    assert B + 1 <= 8
    alpha = (2 * L) ** 0.25
    tm = 768 if S % 768 == 0 else 256

    cos_m, sin_m = _rope_tables(T // GRID_W, MLA_ROPE, n_ctx)
    cos_r, sin_r = _rope_tables(T // GRID_W, RET_DH, n_ctx)

    cond = jnp.zeros((8, D), F32).at[:B].set(c).at[B].set(c_ctx)
    mod_all = _ada_call(cond, w_ada, b_ada)

    h = _ln0_call(ctx, x)
    log_g = jax.nn.log_sigmoid(ret_decay.astype(F32)).reshape(2 * L, RET_HEADS)

    w_in_t = jnp.swapaxes(w_in, 1, 2)
    w_t = _layout_w_in(w_in_t, 0)
    wuq = _layout_w_uq(mla_w_uq)
    wk, wvt = _layout_w_ukv(mla_w_ukv)
    wg, bg = _layout_lru_gates(lru_w_r, lru_b_r, lru_w_i, lru_b_i)
    gq, gkv = mla_q_norm_g[:, None, :], mla_kv_norm_g[:, None, :]
    conv_b, lng, lnb = lru_conv_b[:, None, :], ln_g[:, None, :], ln_b[:, None, :]

    for l in range(L):
        last = l == L - 1
        u = _inproj_call(h, mod_all, w_t, l, n_ctx, tm)
        q, k, vt = _mla_prep_call(u, gq, gkv, wuq, wk, wvt, l, cos_m, sin_m)
        y_mla, w_t = _attn_call(q, k, vt, u, n_ctx, *(() if last else (w_in_t, l + 1)))
        y_lru = _lru_call(u, lru_conv_w, conv_b, wg, bg, lru_lambda, l, n_ctx)
        y_ret, w_out_b = _ret_call(u, log_g, l, cos_r, sin_r, n_ctx, w_out)
        h = _outproj_call(y_mla, y_lru, y_ret, h, mod_all, w_out_b, lng, lnb, l, n_ctx,
                          256 if last else tm, alpha, last)
    return h
```

```python
import functools

import jax
import jax.numpy as jnp
from jax import lax
from jax.experimental import pallas as pl
from jax.experimental.pallas import tpu as pltpu

F32 = jnp.float32
BF16 = jnp.bfloat16

GRID_W = 64
MLA_V = 128
MLA_NOPE = 128
MLA_ROPE = 64
MLA_HEADS = 8
MLA_W = MLA_HEADS * MLA_V
MLA_Q_RANK = 512
MLA_KV_RANK = 256
MLA_SCALE = (MLA_NOPE + MLA_ROPE) ** -0.5
LOG2_E = 1.4426950408889634
MLA_Q_SCALE = MLA_SCALE * LOG2_E
MLA_HP = 256
LRU_W = 512
LRU_BLOCKS = 8
LRU_BW = LRU_W // LRU_BLOCKS
LRU_CONV = 4
LRU_C = 8.0
LRU_CB = 256
RET_HEADS = 4
RET_DH = 128
RET_W = RET_HEADS * RET_DH
RET_CHUNK = 128
RET_K_SCALE = RET_DH ** -0.5
ROPE_BASE = 10000.0
LN_EPS = 1e-5
RMS_EPS = 1e-6
LANES = 128
BF16_ROWS = 16
KEY_CHUNK = 768
ATTN_BUFS = 3
ATTN_HEADS = 2
SUB_ROWS = 256

U_QLAT = 0
U_LRU_X = 512
U_LRU_G = 1024
U_RET_Q = 1536
U_RET_K = 2048
U_RET_V = 2560
U_RET_G = 3072
U_MLA_G = 3584
U_KVLAT = 4608
U_KR = 4864
U_USED = 4992
MXU_COLS = 256
U_CHUNKS = 2
U_TN = -(-U_USED // (U_CHUNKS * MXU_COLS)) * MXU_COLS
U_W = U_CHUNKS * U_TN

VMEM_LIMIT = 56 * 1024 * 1024


def _cparams(sem):
    return pltpu.CompilerParams(dimension_semantics=sem, vmem_limit_bytes=VMEM_LIMIT)


def _dot_nt(a, b):
    return lax.dot_general(a, b, (((1,), (1,)), ((), ())), preferred_element_type=F32)


def _sigmoid(x):
    return 0.5 * jnp.tanh(0.5 * x) + 0.5


def _silu(x):
    return x * _sigmoid(x)


def _softplus(x):
    return jnp.maximum(x, 0.0) + jnp.log1p(jnp.exp(-jnp.abs(x)))


def _rope(x, cos, sin):
    return x * cos + pltpu.roll(x, LANES // 2, axis=x.ndim - 1) * sin


def _ada_kernel(cond_ref, w_ref, b_ref, o_ref):
    c = cond_ref[...]
    a = _silu(c).astype(BF16)
    o_ref[...] = jnp.dot(a, w_ref[...].astype(BF16), preferred_element_type=F32) + b_ref[...]


def _ada_call(cond, w_ada, b_ada3, layer):
    _, D, D3 = w_ada.shape
    tn = 1024
    return pl.pallas_call(
        _ada_kernel,
        out_shape=jax.ShapeDtypeStruct((8, D3), F32),
        grid=(D3 // tn,),
        in_specs=[
            pl.BlockSpec((8, D), lambda n: (0, 0)),
            pl.BlockSpec((None, D, tn), lambda n: (layer, 0, n)),
            pl.BlockSpec((None, 1, tn), lambda n: (layer, 0, n)),
        ],
        out_specs=pl.BlockSpec((8, tn), lambda n: (0, n)),
        compiler_params=_cparams(("parallel",)),
        name="ada_mod",
    )(cond, w_ada, b_ada3)


def _ln0_kernel(c_ref, x_ref, o_ref, *, n_ctx_blocks):
    def norm(ref):
        x = ref[...]
        mu = jnp.mean(x, axis=-1, keepdims=True)
        xc = x - mu
        var = jnp.mean(xc * xc, axis=-1, keepdims=True)
        o_ref[...] = xc * lax.rsqrt(var + LN_EPS)

    j = pl.program_id(1)
    pl.when(j < n_ctx_blocks)(lambda: norm(c_ref))
    pl.when(j >= n_ctx_blocks)(lambda: norm(x_ref))


def _ln0_call(ctx, x):
    B, T, D = x.shape
    n_ctx = ctx.shape[1]
    tm = 256
    nc = n_ctx // tm
    return pl.pallas_call(
        functools.partial(_ln0_kernel, n_ctx_blocks=nc),
        out_shape=jax.ShapeDtypeStruct((B, n_ctx + T, D), F32),
        grid=(B, (n_ctx + T) // tm),
        in_specs=[pl.BlockSpec((None, tm, D), lambda b, j: (b, jnp.minimum(j, nc - 1), 0)),
                  pl.BlockSpec((None, tm, D), lambda b, j: (b, jnp.maximum(j - nc, 0), 0))],
        out_specs=pl.BlockSpec((None, tm, D), lambda b, j: (b, j, 0)),
        compiler_params=_cparams(("parallel", "parallel")),
        name="ln_entry",
    )(ctx, x)


def _inproj_kernel(h_ref, sh_ref, sc_ref, *rest, tm, ctx, ctx_row):
    w_refs, (u_ref, xs_ref) = rest[:U_CHUNKS], rest[U_CHUNKS:]
    b = pl.program_id(0)
    j = pl.program_id(1)
    n = pl.program_id(2)

    @pl.when(n == 0)
    def _():
        for r0 in range(0, tm, SUB_ROWS):
            rs = slice(r0, r0 + SUB_ROWS)
            rows = j * tm + r0 + lax.broadcasted_iota(jnp.int32, (SUB_ROWS, 1), 0)
            is_ctx = rows < ctx
            sh = jnp.where(is_ctx, sh_ref[ctx_row:ctx_row + 1, :], sh_ref[pl.ds(b, 1), :])
            sc = jnp.where(is_ctx, sc_ref[ctx_row:ctx_row + 1, :], sc_ref[pl.ds(b, 1), :])
            xs = (h_ref[rs, :] * (1.0 + sc) + sh).astype(BF16)
            xs_ref[rs, :] = xs
            u_ref[rs, :] = _dot_nt(xs, w_refs[0][...]).astype(u_ref.dtype)

    for c in range(1, U_CHUNKS):
        @pl.when(n == c)
        def _():
            u_ref[...] = _dot_nt(xs_ref[...], w_refs[c][...]).astype(u_ref.dtype)


def _inproj_call(h, mod, w_t, ctx, tm):
    B, S, D = h.shape
    return pl.pallas_call(
        functools.partial(_inproj_kernel, tm=tm, ctx=ctx, ctx_row=B),
        out_shape=jax.ShapeDtypeStruct((B, S, U_W), F32),
        grid=(B, S // tm, U_CHUNKS),
        in_specs=[
            pl.BlockSpec((None, tm, D), lambda b, j, n: (b, j, 0)),
            pl.BlockSpec((8, D), lambda b, j, n: (0, 0)),
            pl.BlockSpec((8, D), lambda b, j, n: (0, 1)),
        ] + [
            pl.BlockSpec((U_TN, D), functools.partial(lambda b, j, n, c: (c, 0), c=c),
                         pipeline_mode=pl.Buffered(1))
            for c in range(U_CHUNKS)
        ],
        out_specs=pl.BlockSpec((None, tm, U_TN), lambda b, j, n: (b, j, n)),
        scratch_shapes=[pltpu.VMEM((tm, D), BF16)],
        compiler_params=_cparams(("parallel", "parallel", "arbitrary")),
        name="in_proj",
    )(h, mod, mod, *([w_t] * U_CHUNKS))


def _rms(x, g):
    return x * lax.rsqrt(jnp.mean(x * x, axis=-1, keepdims=True) + RMS_EPS) * g


def _mla_prep_kernel(ql_ref, kvl_ref, kr_ref, gq_ref, gkv_ref, wuq_ref, wk_ref, wvt_ref,
                     cos_ref, sin_ref, q_ref, k_ref, vt_ref):
    cos = cos_ref[...]
    sin = sin_ref[...]

    zq = _rms(ql_ref[...].astype(F32), gq_ref[...]).astype(BF16)
    q = jnp.dot(zq, wuq_ref[...], preferred_element_type=F32)
    for hd in range(MLA_HEADS):
        c0 = hd * MLA_HP
        q_ref[:, c0:c0 + MLA_NOPE] = (q[:, c0:c0 + MLA_NOPE] * MLA_Q_SCALE).astype(q_ref.dtype)
        qr = _rope(q[:, c0 + MLA_NOPE:c0 + MLA_HP], cos, sin)
        q_ref[:, c0 + MLA_NOPE:c0 + MLA_HP] = (qr * MLA_Q_SCALE).astype(q_ref.dtype)

    zk = _rms(kvl_ref[...].astype(F32), gkv_ref[...]).astype(BF16)
    kn = jnp.dot(zk, wk_ref[...], preferred_element_type=F32)
    kr = _rope(kr_ref[...].astype(F32), cos, sin).astype(k_ref.dtype)
    for hd in range(MLA_HEADS):
        c0 = hd * MLA_HP
        k_ref[:, c0:c0 + MLA_NOPE] = kn[:, hd * MLA_NOPE:(hd + 1) * MLA_NOPE].astype(k_ref.dtype)
        k_ref[:, c0 + MLA_NOPE:c0 + MLA_HP] = kr
    vt = lax.dot_general(wvt_ref[...], zk, (((1,), (1,)), ((), ())), preferred_element_type=F32)
    vt_ref[...] = vt.astype(vt_ref.dtype)


def _mla_prep_call(u, gq, gkv, wuq, wk, wvt, l, cos, sin):
    B, S, _ = u.shape
    tp = 768 if S % 768 == 0 else 256
    HW = MLA_HEADS * MLA_HP
    return pl.pallas_call(
        _mla_prep_kernel,
        out_shape=(
            jax.ShapeDtypeStruct((B, S, HW), BF16),
            jax.ShapeDtypeStruct((B, S, HW), BF16),
            jax.ShapeDtypeStruct((B, MLA_W, S), BF16),
        ),
        grid=(B, S // tp),
        in_specs=[
            pl.BlockSpec((None, tp, MLA_Q_RANK), lambda b, j: (b, j, U_QLAT // MLA_Q_RANK)),
            pl.BlockSpec((None, tp, MLA_KV_RANK), lambda b, j: (b, j, U_KVLAT // MLA_KV_RANK)),
            pl.BlockSpec((None, tp, LANES), lambda b, j: (b, j, U_KR // LANES)),
            pl.BlockSpec((None, 1, MLA_Q_RANK), lambda b, j: (l, 0, 0)),
            pl.BlockSpec((None, 1, MLA_KV_RANK), lambda b, j: (l, 0, 0)),
            pl.BlockSpec((None, MLA_Q_RANK, HW), lambda b, j: (l, 0, 0)),
            pl.BlockSpec((None, MLA_KV_RANK, MLA_W), lambda b, j: (l, 0, 0)),
            pl.BlockSpec((None, MLA_W, MLA_KV_RANK), lambda b, j: (l, 0, 0)),
            pl.BlockSpec((tp, LANES), lambda b, j: (j, 0)),
            pl.BlockSpec((tp, LANES), lambda b, j: (j, 0)),
        ],
        out_specs=(
            pl.BlockSpec((None, tp, HW), lambda b, j: (b, j, 0)),
            pl.BlockSpec((None, tp, HW), lambda b, j: (b, j, 0)),
            pl.BlockSpec((None, MLA_W, tp), lambda b, j: (b, 0, j)),
        ),
        compiler_params=_cparams(("parallel", "parallel")),
        name="mla_prep",
    )(u, u, u, gq, gkv, wuq, wk, wvt, cos, sin)


def _attn_kernel(*refs, ctx, tq, heads, relayout):
    if relayout:
        q_ref, k_ref, vt_ref, g_ref, w_src_ref, y_ref, w_dst_ref = refs[:7]
        _w_in_layout_kernel(w_src_ref, w_dst_ref)
    else:
        q_ref, k_ref, vt_ref, g_ref, y_ref = refs[:5]
    st_ref, m_ref, p_ref, vt1_ref = refs[-4:]
    S = k_ref.shape[0]
    n_lat = S // tq - 1
    assert n_lat >= 2 and S % KEY_CHUNK == 0
    items = [(hd, blk) for hd in range(heads) for blk in range(1, n_lat + 1)]

    for hd in range(heads):
        vt1_ref[hd, 0:MLA_V, :] = vt_ref[hd * MLA_V:(hd + 1) * MLA_V, :]
        vt1_ref[hd, MLA_V:, :] = jnp.ones((BF16_ROWS, S), BF16)

    def rows_of(blk):
        return slice(blk * tq, (blk + 1) * tq)

    def qk_cols(hd):
        return slice(hd * MLA_HP, (hd + 1) * MLA_HP)

    def numerators(st, m):
        return jnp.exp2(st - m).astype(BF16)

    def write_out(hd, blk, acc):
        rows, cols = rows_of(blk), slice(hd * MLA_V, (hd + 1) * MLA_V)
        o = (acc[0:MLA_V, :] / acc[MLA_V:MLA_V + 1, :]).T
        y_ref[rows, cols] = (o * _silu(g_ref[rows, cols].astype(F32))).astype(y_ref.dtype)

    def step(score=None, num=None, att=None):
        def pick(i):
            return (None, None, None) if i is None or i >= len(items) else (i % ATTN_BUFS,) + items[i]

        (sb, s_hd, s_blk), (nb, _, _), (ab, a_hd, a_blk) = pick(score), pick(num), pick(att)
        qb = None if sb is None else q_ref[rows_of(s_blk), qk_cols(s_hd)]
        m_num = None if nb is None else m_ref[nb]
        m_run, acc = None, None
        for c0 in range(0, S, KEY_CHUNK):
            ks = slice(c0, c0 + KEY_CHUNK)
            if nb is not None:
                p_ref[nb, ks, :] = numerators(st_ref[nb, ks, :], m_num)
            if sb is not None:
                st = _dot_nt(k_ref[ks, qk_cols(s_hd)], qb)
                st_ref[sb, ks, :] = st
                m_c = jnp.max(st, axis=0, keepdims=True)
                m_run = m_c if m_run is None else jnp.maximum(m_run, m_c)
            if ab is not None:
                part = jnp.dot(vt1_ref[a_hd, :, ks], p_ref[ab, ks, :], preferred_element_type=F32)
                acc = part if acc is None else acc + part
        if sb is not None:
            m_ref[sb] = m_run
        if ab is not None:
            write_out(a_hd, a_blk, acc)

    step(score=0)
    step(score=1, num=0)
    for hd in range(heads):
        st = _dot_nt(k_ref[0:ctx, qk_cols(hd)], q_ref[rows_of(0), qk_cols(hd)])
        p = numerators(st, jnp.max(st, axis=0, keepdims=True))
        write_out(hd, 0, jnp.dot(vt1_ref[hd, :, 0:ctx], p, preferred_element_type=F32))
    for t in range(len(items)):
        step(score=t + 2, num=t + 1, att=t)


def _attn_call(q, k, vt, u, ctx, w_in_t=None, next_layer=None):
    B, S, _ = q.shape
    tq = 256
    assert ctx == tq
    nh = ATTN_HEADS
    gcol = U_MLA_G // (nh * MLA_V)
    assert MLA_HEADS % nh == 0 and U_MLA_G % (nh * MLA_V) == 0
    n_h = MLA_HEADS // nh
    in_specs = [
        pl.BlockSpec((None, S, nh * MLA_HP), lambda b, h: (b, 0, h)),
        pl.BlockSpec((None, S, nh * MLA_HP), lambda b, h: (b, 0, h)),
        pl.BlockSpec((None, nh * MLA_V, S), lambda b, h: (b, h, 0)),
        pl.BlockSpec((None, S, nh * MLA_V), lambda b, h: (b, 0, gcol + h)),
    ]
    out_shape = [jax.ShapeDtypeStruct((B, S, MLA_W), BF16)]
    out_specs = [pl.BlockSpec((None, S, nh * MLA_V), lambda b, h: (b, 0, h))]
    args = [q, k, vt, u]
    relayout = w_in_t is not None
    if relayout:
        _, W, D = w_in_t.shape
        tc = D // (B * n_h)
        assert tc % LANES == 0
        in_specs.append(pl.BlockSpec((None, W, tc), lambda b, h: (next_layer, 0, b * n_h + h)))
        out_shape.append(jax.ShapeDtypeStruct((U_W, D), BF16))
        out_specs.append(pl.BlockSpec((U_W, tc), lambda b, h: (0, b * n_h + h)))
        args.append(w_in_t)
    out = pl.pallas_call(
        functools.partial(_attn_kernel, ctx=ctx, tq=tq, heads=nh, relayout=relayout),
        out_shape=out_shape,
        grid=(B, n_h),
        in_specs=in_specs,
        out_specs=out_specs,
        scratch_shapes=[pltpu.VMEM((ATTN_BUFS, S, tq), F32), pltpu.VMEM((ATTN_BUFS, 1, tq), F32),
                        pltpu.VMEM((ATTN_BUFS, S, tq), BF16),
                        pltpu.VMEM((nh, MLA_V + BF16_ROWS, S), BF16)],
        compiler_params=_cparams(("parallel", "parallel")),
        name="mla_attn",
    )(*args)
    return (out[0], out[1]) if relayout else (out[0], None)


def _lru_kernel(*refs, ctx, ada):
    x_ref, g_ref, cw_ref, cb_ref, wg_ref, bg_ref, lam_ref = refs[:7]
    af_ref, bf_ref, ab_ref, bb_ref, hlf_ref, plf_ref, hlb_ref, plb_ref = refs[-8:]
    if ada:
        cond_ref, wa_ref, ba_ref, y_ref, mod_ref = refs[7:12]
        _ada_kernel(cond_ref, wa_ref, ba_ref, mod_ref)
    else:
        y_ref = refs[7]
    S, CB = x_ref.shape
    NG = CB // LANES
    x = x_ref[...].astype(F32)
    rows = lax.broadcasted_iota(jnp.int32, (S, 1), 0)
    in_ctx = rows < ctx

    xc = x * cw_ref[2:3, :] + cb_ref[...]
    for tap, off in ((0, -2), (1, -1), (3, 1)):
        xs = pltpu.roll(x, (-off) % S, axis=0)
        src = rows + off
        valid = (src >= 0) & (src < S) & ((src < ctx) == in_ctx)
        xc = xc + jnp.where(valid, xs, 0.0) * cw_ref[tap:tap + 1, :]

    xb = xc.astype(BF16)
    xh = 0.5 * xc
    for d, (a_ref, b_ref) in enumerate(((af_ref, bf_ref), (ab_ref, bb_ref))):
        t = jnp.tanh(jnp.dot(xb, wg_ref[d], preferred_element_type=F32) + bg_ref[d])
        t_r, t_i = t[:, :CB], t[:, CB:]
        c2 = (-0.5 * LRU_C * LOG2_E) * _softplus(-lam_ref[d:d + 1, :])
        a = jnp.exp2(t_r * c2 + c2)
        w = (1.0 - a) * (1.0 + a)
        b = jnp.where(w > 0.0, w * lax.rsqrt(w), 0.0) * (t_i * xh + xh)
        for g in range(NG):
            gs = slice(g * LANES, (g + 1) * LANES)
            a_ref[g] = a[:, gs]
            b_ref[g] = b[:, gs]

    R = 8
    zeros, ones = jnp.zeros((R, LANES), F32), jnp.ones((R, LANES), F32)
    row = lax.broadcasted_iota(jnp.int32, (R, LANES), 0)

    def tile_scan(a, b, enter, rev):
        for d in (1, 2, 4):
            shift = R - d if rev else d
            keep = (row < R - d) if rev else (row >= d)
            a_s = pltpu.roll(a, shift, axis=0)
            b_s = pltpu.roll(b, shift, axis=0)
            b = jnp.where(keep, a * b_s, 0.0) + b
            a = jnp.where(keep, a * a_s, a)
        return a * enter + b

    def scan_part(base, n_rows, enter):
        seg = n_rows // R - 1
        assert seg % 2 == 1
        last = pl.ds(base + R * seg, R)

        def strided(i):
            return pl.ds(base + i, R, stride=seg)

        def packed(i):
            return pl.ds(pl.multiple_of(base + i * R, R), R)

        def local(i, carry):
            ib = seg - 1 - i
            out = []
            for g, (hf, pf, hb, pb) in enumerate(carry):
                a = af_ref[g, strided(i), :]
                hf = a * hf + bf_ref[g, strided(i), :]
                pf = a * pf
                hlf_ref[g, packed(i), :] = hf
                plf_ref[g, packed(i), :] = pf
                a = ab_ref[g, strided(ib), :]
                hb = a * hb + bb_ref[g, strided(ib), :]
                pb = a * pb
                hlb_ref[g, packed(ib), :] = hb
                plb_ref[g, packed(ib), :] = pb
                out.append((hf, pf, hb, pb))
            return tuple(out)

        ends = lax.fori_loop(0, seg, local, ((zeros, ones, zeros, ones),) * NG, unroll=4)

        leave, fixes = [], []
        for g, ((hf, pf, hb, pb), (c_f, c_b)) in enumerate(zip(ends, enter)):
            rows_f, rows_b = [], [None] * R
            for s in range(R):
                rows_f.append(c_f)
                c_f = hf[s:s + 1, :] + pf[s:s + 1, :] * c_f
            tail_f = tile_scan(af_ref[g, last, :], bf_ref[g, last, :], c_f, False)
            tail_b = tile_scan(ab_ref[g, last, :], bb_ref[g, last, :], c_b, True)
            c_f, c_b = tail_f[R - 1:R, :], tail_b[0:1, :]
            for s in reversed(range(R)):
                rows_b[s] = c_b
                c_b = hb[s:s + 1, :] + pb[s:s + 1, :] * c_b
            af_ref[g, last, :] = tail_f
            ab_ref[g, last, :] = tail_b
            leave.append((c_f, c_b))
            fixes.append((jnp.concatenate(rows_f, axis=0), jnp.concatenate(rows_b, axis=0)))

        def fix(i, carry):
            for g, (cf, cb) in enumerate(fixes):
                af_ref[g, strided(i), :] = hlf_ref[g, packed(i), :] + plf_ref[g, packed(i), :] * cf
                ab_ref[g, strided(i), :] = hlb_ref[g, packed(i), :] + plb_ref[g, packed(i), :] * cb
            return carry

        lax.fori_loop(0, seg, fix, 0, unroll=4)
        return leave

    row0 = jnp.zeros((1, LANES), F32)
    scan_part(ctx, S - ctx, scan_part(0, ctx, [(row0, row0)] * NG))

    g_all = g_ref[...].astype(F32)
    for g in range(NG):
        gs = slice(g * LANES, (g + 1) * LANES)
        y_ref[:, gs] = ((af_ref[g] + ab_ref[g]) * _silu(g_all[:, gs])).astype(y_ref.dtype)


def _lru_call(u, cw, cb, wg, bg, lam, l, ctx, ada_next=None):
    B, S, _ = u.shape
    ncb = LRU_W // LRU_CB
    in_specs = [
        pl.BlockSpec((None, S, LRU_CB), lambda b, c: (b, 0, U_LRU_X // LRU_CB + c)),
        pl.BlockSpec((None, S, LRU_CB), lambda b, c: (b, 0, U_LRU_G // LRU_CB + c)),
        pl.BlockSpec((None, LRU_CONV, LRU_CB), lambda b, c: (l, 0, c)),
        pl.BlockSpec((None, 1, LRU_CB), lambda b, c: (l, 0, c)),
        pl.BlockSpec((None, 2, None, LRU_CB, 2 * LRU_CB), lambda b, c: (l, 0, c, 0, 0)),
        pl.BlockSpec((None, 2, None, 1, 2 * LRU_CB), lambda b, c: (l, 0, c, 0, 0)),
        pl.BlockSpec((None, 2, LRU_CB), lambda b, c: (l, 0, c)),
    ]
    out_shape = [jax.ShapeDtypeStruct((B, S, LRU_W), BF16)]
    out_specs = [pl.BlockSpec((None, S, LRU_CB), lambda b, c: (b, 0, c))]
    args = [u, u, cw, cb, wg, bg, lam]
    if ada_next is not None:
        cond, w_ada, b_ada3 = ada_next
        _, D, D3 = w_ada.shape
        tn = D3 // (B * ncb)
        assert tn % LANES == 0
        in_specs += [
            pl.BlockSpec((8, D), lambda b, c: (0, 0)),
            pl.BlockSpec((None, D, tn), lambda b, c: (l + 1, 0, b * ncb + c)),
            pl.BlockSpec((None, 1, tn), lambda b, c: (l + 1, 0, b * ncb + c)),
        ]
        out_shape.append(jax.ShapeDtypeStruct((8, D3), F32))
        out_specs.append(pl.BlockSpec((8, tn), lambda b, c: (0, b * ncb + c)))
        args += [cond, w_ada, b_ada3]
    out = pl.pallas_call(
        functools.partial(_lru_kernel, ctx=ctx, ada=ada_next is not None),
        out_shape=out_shape,
        grid=(B, ncb),
        in_specs=in_specs,
        out_specs=out_specs,
        scratch_shapes=[pltpu.VMEM((LRU_CB // LANES, S, LANES), F32) for _ in range(8)],
        compiler_params=_cparams(("parallel", "parallel")),
        name="rg_lru",
    )(*args)
    return (out[0], out[1]) if ada_next is not None else (out[0], None)


def _ret_kernel(lg_ref, q_ref, k_ref, v_ref, g_ref, cos_ref, sin_ref, wo_src_ref, y_ref, wo_dst_ref,
                ks_ref, kv_ref, st_ref, *, n_ctx, layer):
    wo_dst_ref[...] = wo_src_ref[...].astype(BF16)
    C = RET_CHUNK
    S = q_ref.shape[0]
    n_all = S // C
    unroll = next(f for f in (9, 6, 3, 2, 1) if n_all % f == 0)
    hd = pl.program_id(1)
    lgf = lg_ref[2 * layer, hd]
    lgb = lg_ref[2 * layer + 1, hd]

    ri = lax.broadcasted_iota(jnp.int32, (C, C), 0).astype(F32)
    ci = lax.broadcasted_iota(jnp.int32, (C, C), 1).astype(F32)
    diff = ri - ci
    decay = jnp.where(diff >= 0, jnp.exp(jnp.maximum(diff, 0.0) * lgf),
                      jnp.exp(jnp.maximum(-diff, 0.0) * lgb))
    zeta_f = jnp.exp((C - 1 - ri) * lgf)
    xi_f = jnp.exp((ri + 1) * lgf)
    gc_f = jnp.exp(jnp.full((C, C), C, F32) * lgf)
    zeta_b = jnp.exp(ri * lgb)
    xi_b = jnp.exp((C - ri) * lgb)
    gc_b = jnp.exp(jnp.full((C, C), C, F32) * lgb)
    def rows_of(c):
        return pl.ds(pl.multiple_of(c * C, C), C)

    def chunk_kv(c, carry):
        sl = rows_of(c)
        k = _rope(k_ref[sl, :].astype(F32), cos_ref[sl, :], sin_ref[sl, :]) * RET_K_SCALE
        ks_ref[sl, :] = k.astype(BF16)
        kz = jnp.concatenate([k * zeta_f, k * zeta_b], axis=1).astype(BF16)
        kv_ref[c] = lax.dot_general(v_ref[sl, :].astype(BF16), kz, (((0,), (0,)), ((), ())),
                                    preferred_element_type=F32)
        return carry

    lax.fori_loop(0, n_all, chunk_kv, 0, unroll=unroll)

    def fwd_state(c, r):
        st_ref[c, :, 0:C] = r.astype(BF16)
        return gc_f * r + kv_ref[c, :, 0:C]

    def bwd_state(i, r):
        c = jnp.where(i < n_ctx, n_ctx - 1 - i, n_all - 1 - (i - n_ctx))
        st_ref[c, :, C:2 * C] = r.astype(BF16)
        return gc_b * r + kv_ref[c, :, C:2 * C]

    lax.fori_loop(0, n_all, fwd_state, jnp.zeros((C, C), F32))
    lax.fori_loop(0, n_all, bwd_state, jnp.zeros((C, C), F32))

    group = next(f for f in (6, 3, 2, 1) if n_all % f == 0)

    def chunks_out(i, carry):
        cs = [i * group + g for g in range(group)]
        sls = [rows_of(c) for c in cs]
        qs = [_rope(q_ref[sl, :].astype(F32), cos_ref[sl, :], sin_ref[sl, :]) for sl in sls]
        ss = [_dot_nt(q.astype(BF16), ks_ref[sl, :]) for q, sl in zip(qs, sls)]
        qxs = [jnp.concatenate([q * xi_f, q * xi_b], axis=1).astype(BF16) for q in qs]
        os = [jnp.dot((s * decay).astype(BF16), v_ref[sl, :].astype(BF16), preferred_element_type=F32)
              + _dot_nt(qx, st_ref[c]) for s, sl, qx, c in zip(ss, sls, qxs, cs)]
        for o, sl in zip(os, sls):
            mu = jnp.mean(o, axis=-1, keepdims=True)
            oc = o - mu
            var = jnp.mean(oc * oc, axis=-1, keepdims=True)
            y = oc * lax.rsqrt(var + LN_EPS) * _silu(g_ref[sl, :].astype(F32))
            y_ref[sl, :] = y.astype(y_ref.dtype)
        return carry

    lax.fori_loop(0, n_all // group, chunks_out, 0)


def _ret_call(u, log_g, l, cos, sin, ctx, w_out):
    B, S, _ = u.shape
    n_all = S // RET_CHUNK
    D = w_out.shape[1]
    tc = D // (B * RET_HEADS)
    assert tc % LANES == 0

    def col(base):
        return lambda b, h: (b, 0, base // RET_DH + h)

    return pl.pallas_call(
        functools.partial(_ret_kernel, n_ctx=ctx // RET_CHUNK, layer=l),
        out_shape=(jax.ShapeDtypeStruct((B, S, RET_W), BF16), jax.ShapeDtypeStruct((D, D), BF16)),
        grid=(B, RET_HEADS),
        in_specs=[
            pl.BlockSpec(memory_space=pltpu.SMEM),
            pl.BlockSpec((None, S, RET_DH), col(U_RET_Q)),
            pl.BlockSpec((None, S, RET_DH), col(U_RET_K)),
            pl.BlockSpec((None, S, RET_DH), col(U_RET_V)),
            pl.BlockSpec((None, S, RET_DH), col(U_RET_G)),
            pl.BlockSpec((S, RET_DH), lambda b, h: (0, 0)),
            pl.BlockSpec((S, RET_DH), lambda b, h: (0, 0)),
            pl.BlockSpec((None, D, tc), lambda b, h: (l, 0, b * RET_HEADS + h)),
        ],
        out_specs=(pl.BlockSpec((None, S, RET_DH), lambda b, h: (b, 0, h)),
                   pl.BlockSpec((D, tc), lambda b, h: (0, b * RET_HEADS + h))),
        scratch_shapes=[pltpu.VMEM((S, RET_DH), BF16),
                        pltpu.VMEM((n_all, RET_DH, 2 * RET_DH), F32),
                        pltpu.VMEM((n_all, RET_DH, 2 * RET_DH), BF16)],
        compiler_params=_cparams(("parallel", "parallel")),
        name="retention",
    )(log_g, u, u, u, u, cos, sin, w_out)


def _outproj_kernel(ym_ref, yl_ref, yr_ref, h_ref, gt_ref, w_ref, lng_ref, lnb_ref, o_ref,
                    *, tm, ctx, alpha, skip, ctx_row):
    b = pl.program_id(0)
    j = pl.program_id(1) + skip

    def project(r0):
        rs = slice(r0, r0 + SUB_ROWS)
        y = jnp.concatenate([ym_ref[rs, :], yl_ref[rs, :], yr_ref[rs, :]], axis=1)
        return jnp.dot(y, w_ref[...], preferred_element_type=F32)

    starts = list(range(0, tm, SUB_ROWS))
    acc_next = project(starts[0])
    for i, r0 in enumerate(starts):
        acc = acc_next
        if i + 1 < len(starts):
            acc_next = project(starts[i + 1])
        rs = slice(r0, r0 + SUB_ROWS)
        rows = j * tm + r0 + lax.broadcasted_iota(jnp.int32, (SUB_ROWS, 1), 0)
        gt = jnp.where(rows < ctx, gt_ref[ctx_row:ctx_row + 1, :], gt_ref[pl.ds(b, 1), :])
        z = alpha * h_ref[rs, :] + gt * acc
        mu = jnp.mean(z, axis=-1, keepdims=True)
        zc = z - mu
        var = jnp.mean(zc * zc, axis=-1, keepdims=True)
        o_ref[rs, :] = zc * lax.rsqrt(var + LN_EPS) * lng_ref[...] + lnb_ref[...]


def _outproj_call(ym, yl, yr, h, mod, w, lng, lnb, l, ctx, tm, alpha, latent_only):
    B, S, D = h.shape
    skip = ctx // tm if latent_only else 0
    assert skip * tm == (ctx if latent_only else 0)
    n_out = S // tm - skip
    return pl.pallas_call(
        functools.partial(_outproj_kernel, tm=tm, ctx=ctx, alpha=alpha, skip=skip, ctx_row=B),
        out_shape=jax.ShapeDtypeStruct((B, n_out * tm, D), F32),
        grid=(B, n_out),
        in_specs=[
            pl.BlockSpec((None, tm, MLA_W), lambda b, j: (b, j + skip, 0)),
            pl.BlockSpec((None, tm, LRU_W), lambda b, j: (b, j + skip, 0)),
            pl.BlockSpec((None, tm, RET_W), lambda b, j: (b, j + skip, 0)),
            pl.BlockSpec((None, tm, D), lambda b, j: (b, j + skip, 0)),
            pl.BlockSpec((8, D), lambda b, j: (0, 2)),
            pl.BlockSpec((D, D), lambda b, j: (0, 0), pipeline_mode=pl.Buffered(1)),
            pl.BlockSpec((None, 1, D), lambda b, j: (l, 0, 0)),
            pl.BlockSpec((None, 1, D), lambda b, j: (l, 0, 0)),
        ],
        out_specs=pl.BlockSpec((None, tm, D), lambda b, j: (b, j, 0)),
        compiler_params=_cparams(("parallel", "parallel")),
        name="out_proj_ln",
    )(ym, yl, yr, h, mod, w, lng, lnb)


def _rope_tables(rows, dim, ctx):
    row = jnp.repeat(jnp.arange(rows, dtype=F32), GRID_W)
    col = jnp.tile(jnp.arange(GRID_W, dtype=F32), rows)
    quarter = dim // 4
    inv = ROPE_BASE ** (-jnp.arange(quarter, dtype=F32) / quarter)
    ang = jnp.stack([row[:, None] * inv, col[:, None] * inv], axis=1)
    cos, sin = jnp.cos(ang), jnp.sin(ang)
    pad = ((0, 0), (0, LANES // 2 - 2 * quarter))
    cos_h = jnp.pad(jnp.concatenate([cos[:, 0], cos[:, 1]], axis=-1), pad)
    sin_h = jnp.pad(jnp.concatenate([sin[:, 0], sin[:, 1]], axis=-1), pad)
    cos_l = jnp.concatenate([cos_h, cos_h], axis=-1)
    sin_l = jnp.concatenate([-sin_h, sin_h], axis=-1)
    cos_c = jnp.ones((ctx, LANES), F32)
    sin_c = jnp.zeros((ctx, LANES), F32)
    return jnp.concatenate([cos_c, cos_l], axis=0), jnp.concatenate([sin_c, sin_l], axis=0)


def _rotary_lane_order(dim):
    quarter = dim // 4
    return ((0, 0), (quarter, LANES // 2), (2 * quarter, quarter), (3 * quarter, LANES // 2 + quarter))


def _rotary_moves(src, dst, dim, groups):
    return tuple((src + g * dim + s, dst + g * LANES + d, dim // 4)
                 for g in range(groups) for s, d in _rotary_lane_order(dim))


W_IN_MOVES = ((0, U_QLAT, 512), (512, U_KVLAT, 256), (832, U_MLA_G, 1024),
              (1856, U_LRU_X, 512), (2368, U_LRU_G, 512), (3904, U_RET_V, 512), (4416, U_RET_G, 512)
              ) + _rotary_moves(768, U_KR, MLA_ROPE, 1) + _rotary_moves(2880, U_RET_Q, RET_DH, RET_HEADS
              ) + _rotary_moves(3392, U_RET_K, RET_DH, RET_HEADS)
W_IN_ZERO = tuple((U_KR + o, LANES // 2 - MLA_ROPE // 2) for o in (MLA_ROPE // 2, LANES // 2 + MLA_ROPE // 2)
                  ) + ((U_USED, U_W - U_USED),)


def _w_in_layout_kernel(w_ref, o_ref):
    for src, dst, width in W_IN_MOVES:
        o_ref[dst:dst + width, :] = w_ref[src:src + width, :].astype(BF16)
    for dst, width in W_IN_ZERO:
        o_ref[dst:dst + width, :] = jnp.zeros((width, o_ref.shape[1]), BF16)


def _layout_w_in(w_in_t, layer):
    _, W, D = w_in_t.shape
    tc = 512
    return pl.pallas_call(
        _w_in_layout_kernel,
        out_shape=jax.ShapeDtypeStruct((U_W, D), BF16),
        grid=(D // tc,),
        in_specs=[pl.BlockSpec((None, W, tc), lambda r: (layer, 0, r))],
        out_specs=pl.BlockSpec((U_W, tc), lambda r: (0, r)),
        compiler_params=_cparams(("parallel",)),
        name="w_in_layout",
    )(w_in_t)


def _layout_w_uq(w_uq):
    L = w_uq.shape[0]
    w = w_uq.reshape(L, MLA_Q_RANK, MLA_HEADS, MLA_NOPE + MLA_ROPE)
    quarter = MLA_ROPE // 4
    pieces, lane = [w[..., :MLA_NOPE]], 0
    for s, d in sorted(_rotary_lane_order(MLA_ROPE), key=lambda sd: sd[1]):
        pieces += [jnp.zeros(w.shape[:-1] + (d - lane,), w.dtype),
                   w[..., MLA_NOPE + s:MLA_NOPE + s + quarter]]
        lane = d + quarter
    pieces.append(jnp.zeros(w.shape[:-1] + (LANES - lane,), w.dtype))
    w = jnp.concatenate(pieces, axis=-1)
    return w.reshape(L, MLA_Q_RANK, MLA_HEADS * MLA_HP).astype(BF16)


def _layout_w_ukv(w_ukv):
    L = w_ukv.shape[0]
    w = w_ukv.reshape(L, MLA_KV_RANK, MLA_HEADS, MLA_NOPE + MLA_V)
    wk = w[..., :MLA_NOPE].reshape(L, MLA_KV_RANK, MLA_W).astype(BF16)
    wvt = w[..., MLA_NOPE:].reshape(L, MLA_KV_RANK, MLA_W).transpose(0, 2, 1).astype(BF16)
    return wk, wvt


def _layout_lru_gates(w_r, b_r, w_i, b_i):
    L = w_r.shape[0]
    ncb = LRU_W // LRU_CB
    per = LRU_BLOCKS // ncb
    eye = jnp.eye(per, dtype=w_r.dtype)

    def diag_blocks(w):
        w = w.reshape(L, 2, ncb, per, LRU_BW, LRU_BW)
        return jnp.einsum("ldcgij,gh->ldcgihj", w, eye).reshape(L, 2, ncb, LRU_CB, LRU_CB)

    wg = jnp.concatenate([diag_blocks(w_r), diag_blocks(w_i)], axis=-1).astype(BF16)
    bg = jnp.concatenate([b_r.reshape(L, 2, ncb, 1, LRU_CB), b_i.reshape(L, 2, ncb, 1, LRU_CB)], axis=-1)
    return 0.5 * wg, 0.5 * bg


def kernel(x, c, ctx, c_ctx, w_ada, b_ada, w_in, mla_q_norm_g, mla_kv_norm_g, mla_w_uq, mla_w_ukv,
           lru_conv_w, lru_conv_b, lru_w_r, lru_b_r, lru_w_i, lru_b_i, lru_lambda, ret_decay,
           w_out, ln_g, ln_b):
    B, T, D = x.shape
    L = w_in.shape[0]
    n_ctx = ctx.shape[1]
    S = n_ctx + T
    assert D == 2 * MLA_W and w_in.shape[2] == 4928 and n_ctx % 256 == 0 and T % 256 == 0
    assert B + 1 <= 8
    alpha = (2 * L) ** 0.25
    tm = 768 if S % 768 == 0 else 256

    cos_m, sin_m = _rope_tables(T // GRID_W, MLA_ROPE, n_ctx)
    cos_r, sin_r = _rope_tables(T // GRID_W, RET_DH, n_ctx)

    cond = jnp.zeros((8, D), F32).at[:B].set(c).at[B].set(c_ctx)
    b_ada3 = b_ada[:, None, :]
    mod = _ada_call(cond, w_ada, b_ada3, 0)

    h = _ln0_call(ctx, x)
    log_g = jax.nn.log_sigmoid(ret_decay.astype(F32)).reshape(2 * L, RET_HEADS)

    w_in_t = jnp.swapaxes(w_in, 1, 2)
    w_t = _layout_w_in(w_in_t, 0)
    wuq = _layout_w_uq(mla_w_uq)
    wk, wvt = _layout_w_ukv(mla_w_ukv)
    wg, bg = _layout_lru_gates(lru_w_r, lru_b_r, lru_w_i, lru_b_i)
    gq, gkv = mla_q_norm_g[:, None, :], mla_kv_norm_g[:, None, :]
    conv_b, lng, lnb = lru_conv_b[:, None, :], ln_g[:, None, :], ln_b[:, None, :]

    for l in range(L):
        last = l == L - 1
        u = _inproj_call(h, mod, w_t, n_ctx, tm)
        q, k, vt = _mla_prep_call(u, gq, gkv, wuq, wk, wvt, l, cos_m, sin_m)
        y_mla, w_t = _attn_call(q, k, vt, u, n_ctx, *(() if last else (w_in_t, l + 1)))
        y_lru, mod_next = _lru_call(u, lru_conv_w, conv_b, wg, bg, lru_lambda, l, n_ctx,
                                    None if last else (cond, w_ada, b_ada3))
        y_ret, w_out_b = _ret_call(u, log_g, l, cos_r, sin_r, n_ctx, w_out)
        h = _outproj_call(y_mla, y_lru, y_ret, h, mod, w_out_b, lng, lnb, l, n_ctx,
                          256 if last else tm, alpha, last)
        mod = mod_next
    return h
```

```python
import functools

import jax
import jax.numpy as jnp
from jax import lax
from jax.experimental import pallas as pl
from jax.experimental.pallas import tpu as pltpu

F32 = jnp.float32
BF16 = jnp.bfloat16

GRID_W = 64
MLA_V = 128
MLA_NOPE = 128
MLA_ROPE = 64
MLA_HEADS = 8
MLA_W = MLA_HEADS * MLA_V
MLA_Q_RANK = 512
MLA_KV_RANK = 256
MLA_SCALE = (MLA_NOPE + MLA_ROPE) ** -0.5
LOG2_E = 1.4426950408889634
MLA_Q_SCALE = MLA_SCALE * LOG2_E
MLA_HP = 256
LRU_W = 512
LRU_BLOCKS = 8
LRU_BW = LRU_W // LRU_BLOCKS
LRU_CONV = 4
LRU_C = 8.0
LRU_CB = 256
RET_HEADS = 4
RET_DH = 128
RET_W = RET_HEADS * RET_DH
RET_CHUNK = 128
RET_K_SCALE = RET_DH ** -0.5
ROPE_BASE = 10000.0
LN_EPS = 1e-5
RMS_EPS = 1e-6
LANES = 128
BF16_ROWS = 16
KEY_CHUNK = 768
ATTN_BUFS = 3
ATTN_HEADS = 2
SUB_ROWS = 256

U_QLAT = 0
U_LRU_X = 512
U_LRU_G = 1024
U_RET_Q = 1536
U_RET_K = 2048
U_RET_V = 2560
U_RET_G = 3072
U_MLA_G = 3584
U_KVLAT = 4608
U_KR = 4864
U_USED = 4992
MXU_COLS = 256
U_CHUNKS = 2
U_TN = -(-U_USED // (U_CHUNKS * MXU_COLS)) * MXU_COLS
U_W = U_CHUNKS * U_TN

V7X_VMEM_BYTES = 64 * 1024 * 1024
VMEM_LIMIT = V7X_VMEM_BYTES // 8 * 7


def _cparams(sem):
    return pltpu.CompilerParams(dimension_semantics=sem, vmem_limit_bytes=VMEM_LIMIT)


def _dot_nt(a, b):
    return lax.dot_general(a, b, (((1,), (1,)), ((), ())), preferred_element_type=F32)


def _sigmoid(x):
    return 0.5 * jnp.tanh(0.5 * x) + 0.5


def _silu(x):
    return x * _sigmoid(x)


def _softplus(x):
    return jnp.maximum(x, 0.0) + jnp.log1p(jnp.exp(-jnp.abs(x)))


def _rope(x, cos, sin):
    return x * cos + pltpu.roll(x, LANES // 2, axis=x.ndim - 1) * sin


def _ada_kernel(cond_ref, w_ref, b_ref, o_ref):
    c = cond_ref[...]
    a = _silu(c).astype(BF16)
    o_ref[...] = jnp.dot(a, w_ref[...].astype(BF16), preferred_element_type=F32) + b_ref[...]


def _ada_call(cond, w_ada, b_ada3, layer):
    _, D, D3 = w_ada.shape
    tn = 1024
    return pl.pallas_call(
        _ada_kernel,
        out_shape=jax.ShapeDtypeStruct((8, D3), F32),
        grid=(D3 // tn,),
        in_specs=[
            pl.BlockSpec((8, D), lambda n: (0, 0)),
            pl.BlockSpec((None, D, tn), lambda n: (layer, 0, n)),
            pl.BlockSpec((None, 1, tn), lambda n: (layer, 0, n)),
        ],
        out_specs=pl.BlockSpec((8, tn), lambda n: (0, n)),
        compiler_params=_cparams(("parallel",)),
        name="ada_mod",
    )(cond, w_ada, b_ada3)


def _ln0_kernel(c_ref, *refs):
    x_refs, o_ref = refs[:-1], refs[-1]
    rows = c_ref.shape[0]

    def norm(rs, ref):
        x = ref[...]
        mu = jnp.mean(x, axis=-1, keepdims=True)
        xc = x - mu
        var = jnp.mean(xc * xc, axis=-1, keepdims=True)
        o_ref[rs, :] = xc * lax.rsqrt(var + LN_EPS)

    j = pl.program_id(1)
    for k, x_ref in enumerate(x_refs):
        rs = slice(k * rows, (k + 1) * rows)
        if k == 0:
            pl.when(j == 0)(functools.partial(norm, rs, c_ref))
            pl.when(j > 0)(functools.partial(norm, rs, x_ref))
        else:
            norm(rs, x_ref)


def _ln0_call(ctx, x):
    B, T, D = x.shape
    rows = ctx.shape[1]
    n_blocks = (rows + T) // rows
    group = next(g for g in (3, 2, 1) if n_blocks % g == 0)
    assert T % rows == 0

    def x_spec(k):
        return pl.BlockSpec((None, rows, D), lambda b, j: (b, jnp.maximum(j * group + k - 1, 0), 0))

    return pl.pallas_call(
        _ln0_kernel,
        out_shape=jax.ShapeDtypeStruct((B, rows + T, D), F32),
        grid=(B, n_blocks // group),
        in_specs=[pl.BlockSpec((None, rows, D), lambda b, j: (b, 0, 0))] + [x_spec(k) for k in range(group)],
        out_specs=pl.BlockSpec((None, group * rows, D), lambda b, j: (b, j, 0)),
        compiler_params=_cparams(("parallel", "parallel")),
        name="ln_entry",
    )(ctx, *([x] * group))


def _inproj_kernel(h_ref, sh_ref, sc_ref, *rest, tm, ctx, ctx_row):
    w_refs, (u_ref, xs_ref) = rest[:U_CHUNKS], rest[U_CHUNKS:]
    b = pl.program_id(0)
    j = pl.program_id(1)
    n = pl.program_id(2)

    @pl.when(n == 0)
    def _():
        for r0 in range(0, tm, SUB_ROWS):
            rs = slice(r0, r0 + SUB_ROWS)
            rows = j * tm + r0 + lax.broadcasted_iota(jnp.int32, (SUB_ROWS, 1), 0)
            is_ctx = rows < ctx
            sh = jnp.where(is_ctx, sh_ref[ctx_row:ctx_row + 1, :], sh_ref[pl.ds(b, 1), :])
            sc = jnp.where(is_ctx, sc_ref[ctx_row:ctx_row + 1, :], sc_ref[pl.ds(b, 1), :])
            xs = (h_ref[rs, :] * (1.0 + sc) + sh).astype(BF16)
            xs_ref[rs, :] = xs
            u_ref[rs, :] = _dot_nt(xs, w_refs[0][...]).astype(u_ref.dtype)

    for c in range(1, U_CHUNKS):
        @pl.when(n == c)
        def _():
            u_ref[...] = _dot_nt(xs_ref[...], w_refs[c][...]).astype(u_ref.dtype)


def _inproj_call(h, mod, w_t, ctx, tm):
    B, S, D = h.shape
    return pl.pallas_call(
        functools.partial(_inproj_kernel, tm=tm, ctx=ctx, ctx_row=B),
        out_shape=jax.ShapeDtypeStruct((B, S, U_W), F32),
        grid=(B, S // tm, U_CHUNKS),
        in_specs=[
            pl.BlockSpec((None, tm, D), lambda b, j, n: (b, j, 0)),
            pl.BlockSpec((8, D), lambda b, j, n: (0, 0)),
            pl.BlockSpec((8, D), lambda b, j, n: (0, 1)),
        ] + [
            pl.BlockSpec((U_TN, D), functools.partial(lambda b, j, n, c: (c, 0), c=c),
                         pipeline_mode=pl.Buffered(1))
            for c in range(U_CHUNKS)
        ],
        out_specs=pl.BlockSpec((None, tm, U_TN), lambda b, j, n: (b, j, n)),
        scratch_shapes=[pltpu.VMEM((tm, D), BF16)],
        compiler_params=_cparams(("parallel", "parallel", "arbitrary")),
        name="in_proj",
    )(h, mod, mod, *([w_t] * U_CHUNKS))


def _rms(x, g):
    return x * lax.rsqrt(jnp.mean(x * x, axis=-1, keepdims=True) + RMS_EPS) * g


def _mla_prep_kernel(ql_ref, kvl_ref, kr_ref, gq_ref, gkv_ref, wuq_ref, wk_ref, wvt_ref,
                     cos_ref, sin_ref, q_ref, k_ref, vt_ref):
    cos = cos_ref[...]
    sin = sin_ref[...]

    zq = _rms(ql_ref[...].astype(F32), gq_ref[...]).astype(BF16)
    q = jnp.dot(zq, wuq_ref[...], preferred_element_type=F32)
    for hd in range(MLA_HEADS):
        c0 = hd * MLA_HP
        q_ref[:, c0:c0 + MLA_NOPE] = (q[:, c0:c0 + MLA_NOPE] * MLA_Q_SCALE).astype(q_ref.dtype)
        qr = _rope(q[:, c0 + MLA_NOPE:c0 + MLA_HP], cos, sin)
        q_ref[:, c0 + MLA_NOPE:c0 + MLA_HP] = (qr * MLA_Q_SCALE).astype(q_ref.dtype)

    zk = _rms(kvl_ref[...].astype(F32), gkv_ref[...]).astype(BF16)
    kn = jnp.dot(zk, wk_ref[...], preferred_element_type=F32)
    kr = _rope(kr_ref[...].astype(F32), cos, sin).astype(k_ref.dtype)
    for hd in range(MLA_HEADS):
        c0 = hd * MLA_HP
        k_ref[:, c0:c0 + MLA_NOPE] = kn[:, hd * MLA_NOPE:(hd + 1) * MLA_NOPE].astype(k_ref.dtype)
        k_ref[:, c0 + MLA_NOPE:c0 + MLA_HP] = kr
    vt = lax.dot_general(wvt_ref[...], zk, (((1,), (1,)), ((), ())), preferred_element_type=F32)
    vt_ref[...] = vt.astype(vt_ref.dtype)


def _mla_prep_call(u, gq, gkv, wuq, wk, wvt, l, cos, sin):
    B, S, _ = u.shape
    tp = 768 if S % 768 == 0 else 256
    HW = MLA_HEADS * MLA_HP
    return pl.pallas_call(
        _mla_prep_kernel,
        out_shape=(
            jax.ShapeDtypeStruct((B, S, HW), BF16),
            jax.ShapeDtypeStruct((B, S, HW), BF16),
            jax.ShapeDtypeStruct((B, MLA_W, S), BF16),
        ),
        grid=(B, S // tp),
        in_specs=[
            pl.BlockSpec((None, tp, MLA_Q_RANK), lambda b, j: (b, j, U_QLAT // MLA_Q_RANK)),
            pl.BlockSpec((None, tp, MLA_KV_RANK), lambda b, j: (b, j, U_KVLAT // MLA_KV_RANK)),
            pl.BlockSpec((None, tp, LANES), lambda b, j: (b, j, U_KR // LANES)),
            pl.BlockSpec((None, 1, MLA_Q_RANK), lambda b, j: (l, 0, 0)),
            pl.BlockSpec((None, 1, MLA_KV_RANK), lambda b, j: (l, 0, 0)),
            pl.BlockSpec((None, MLA_Q_RANK, HW), lambda b, j: (l, 0, 0)),
            pl.BlockSpec((None, MLA_KV_RANK, MLA_W), lambda b, j: (l, 0, 0)),
            pl.BlockSpec((None, MLA_W, MLA_KV_RANK), lambda b, j: (l, 0, 0)),
            pl.BlockSpec((tp, LANES), lambda b, j: (j, 0)),
            pl.BlockSpec((tp, LANES), lambda b, j: (j, 0)),
        ],
        out_specs=(
            pl.BlockSpec((None, tp, HW), lambda b, j: (b, j, 0)),
            pl.BlockSpec((None, tp, HW), lambda b, j: (b, j, 0)),
            pl.BlockSpec((None, MLA_W, tp), lambda b, j: (b, 0, j)),
        ),
        compiler_params=_cparams(("parallel", "parallel")),
        name="mla_prep",
    )(u, u, u, gq, gkv, wuq, wk, wvt, cos, sin)


def _attn_kernel(*refs, ctx, tq, heads, relayout):
    if relayout:
        q_ref, k_ref, vt_ref, g_ref, w_src_ref, y_ref, w_dst_ref = refs[:7]
        _w_in_layout_kernel(w_src_ref, w_dst_ref)
    else:
        q_ref, k_ref, vt_ref, g_ref, y_ref = refs[:5]
    st_ref, m_ref, p_ref, vt1_ref = refs[-4:]
    S = k_ref.shape[0]
    n_lat = S // tq - 1
    assert n_lat >= 2 and S % KEY_CHUNK == 0
    items = [(hd, blk) for hd in range(heads) for blk in range(1, n_lat + 1)]

    for hd in range(heads):
        vt1_ref[hd, 0:MLA_V, :] = vt_ref[hd * MLA_V:(hd + 1) * MLA_V, :]
        vt1_ref[hd, MLA_V:, :] = jnp.ones((BF16_ROWS, S), BF16)

    def rows_of(blk):
        return slice(blk * tq, (blk + 1) * tq)

    def qk_cols(hd):
        return slice(hd * MLA_HP, (hd + 1) * MLA_HP)

    def numerators(st, m):
        return jnp.exp2(st - m).astype(BF16)

    def write_out(hd, blk, acc):
        rows, cols = rows_of(blk), slice(hd * MLA_V, (hd + 1) * MLA_V)
        o = (acc[0:MLA_V, :] / acc[MLA_V:MLA_V + 1, :]).T
        y_ref[rows, cols] = (o * _silu(g_ref[rows, cols].astype(F32))).astype(y_ref.dtype)

    def step(score=None, num=None, att=None):
        def pick(i):
            return (None, None, None) if i is None or i >= len(items) else (i % ATTN_BUFS,) + items[i]

        (sb, s_hd, s_blk), (nb, _, _), (ab, a_hd, a_blk) = pick(score), pick(num), pick(att)
        qb = None if sb is None else q_ref[rows_of(s_blk), qk_cols(s_hd)]
        m_num = None if nb is None else m_ref[nb]
        m_run, acc = None, None
        for c0 in range(0, S, KEY_CHUNK):
            ks = slice(c0, c0 + KEY_CHUNK)
            if nb is not None:
                p_ref[nb, ks, :] = numerators(st_ref[nb, ks, :], m_num)
            if sb is not None:
                st = _dot_nt(k_ref[ks, qk_cols(s_hd)], qb)
                st_ref[sb, ks, :] = st
                m_c = jnp.max(st, axis=0, keepdims=True)
                m_run = m_c if m_run is None else jnp.maximum(m_run, m_c)
            if ab is not None:
                part = jnp.dot(vt1_ref[a_hd, :, ks], p_ref[ab, ks, :], preferred_element_type=F32)
                acc = part if acc is None else acc + part
        if sb is not None:
            m_ref[sb] = m_run
        if ab is not None:
            write_out(a_hd, a_blk, acc)

    step(score=0)
    step(score=1, num=0)
    for hd in range(heads):
        st = _dot_nt(k_ref[0:ctx, qk_cols(hd)], q_ref[rows_of(0), qk_cols(hd)])
        p = numerators(st, jnp.max(st, axis=0, keepdims=True))
        write_out(hd, 0, jnp.dot(vt1_ref[hd, :, 0:ctx], p, preferred_element_type=F32))
    for t in range(len(items)):
        step(score=t + 2, num=t + 1, att=t)


def _attn_call(q, k, vt, u, ctx, w_in_t=None, next_layer=None):
    B, S, _ = q.shape
    tq = 256
    assert ctx == tq
    nh = ATTN_HEADS
    gcol = U_MLA_G // (nh * MLA_V)
    assert MLA_HEADS % nh == 0 and U_MLA_G % (nh * MLA_V) == 0
    n_h = MLA_HEADS // nh
    in_specs = [
        pl.BlockSpec((None, S, nh * MLA_HP), lambda b, h: (b, 0, h)),
        pl.BlockSpec((None, S, nh * MLA_HP), lambda b, h: (b, 0, h)),
        pl.BlockSpec((None, nh * MLA_V, S), lambda b, h: (b, h, 0)),
        pl.BlockSpec((None, S, nh * MLA_V), lambda b, h: (b, 0, gcol + h)),
    ]
    out_shape = [jax.ShapeDtypeStruct((B, S, MLA_W), BF16)]
    out_specs = [pl.BlockSpec((None, S, nh * MLA_V), lambda b, h: (b, 0, h))]
    args = [q, k, vt, u]
    relayout = w_in_t is not None
    if relayout:
        _, W, D = w_in_t.shape
        tc = D // (B * n_h)
        assert tc % LANES == 0
        in_specs.append(pl.BlockSpec((None, W, tc), lambda b, h: (next_layer, 0, b * n_h + h)))
        out_shape.append(jax.ShapeDtypeStruct((U_W, D), BF16))
        out_specs.append(pl.BlockSpec((U_W, tc), lambda b, h: (0, b * n_h + h)))
        args.append(w_in_t)
    out = pl.pallas_call(
        functools.partial(_attn_kernel, ctx=ctx, tq=tq, heads=nh, relayout=relayout),
        out_shape=out_shape,
        grid=(B, n_h),
        in_specs=in_specs,
        out_specs=out_specs,
        scratch_shapes=[pltpu.VMEM((ATTN_BUFS, S, tq), F32), pltpu.VMEM((ATTN_BUFS, 1, tq), F32),
                        pltpu.VMEM((ATTN_BUFS, S, tq), BF16),
                        pltpu.VMEM((nh, MLA_V + BF16_ROWS, S), BF16)],
        compiler_params=_cparams(("parallel", "parallel")),
        name="mla_attn",
    )(*args)
    return (out[0], out[1]) if relayout else (out[0], None)


def _lru_kernel(*refs, ctx, ada):
    x_ref, g_ref, cw_ref, cb_ref, wg_ref, bg_ref, lam_ref = refs[:7]
    af_ref, bf_ref, ab_ref, bb_ref, hlf_ref, plf_ref, hlb_ref, plb_ref = refs[-8:]
    if ada:
        cond_ref, wa_ref, ba_ref, y_ref, mod_ref = refs[7:12]
        _ada_kernel(cond_ref, wa_ref, ba_ref, mod_ref)
    else:
        y_ref = refs[7]
    S, CB = x_ref.shape
    NG = CB // LANES
    x = x_ref[...].astype(F32)
    rows = lax.broadcasted_iota(jnp.int32, (S, 1), 0)
    in_ctx = rows < ctx

    xc = x * cw_ref[2:3, :] + cb_ref[...]
    for tap, off in ((0, -2), (1, -1), (3, 1)):
        xs = pltpu.roll(x, (-off) % S, axis=0)
        src = rows + off
        valid = (src >= 0) & (src < S) & ((src < ctx) == in_ctx)
        xc = xc + jnp.where(valid, xs, 0.0) * cw_ref[tap:tap + 1, :]

    xb = xc.astype(BF16)
    xh = 0.5 * xc
    for d, (a_ref, b_ref) in enumerate(((af_ref, bf_ref), (ab_ref, bb_ref))):
        t = jnp.tanh(jnp.dot(xb, wg_ref[d], preferred_element_type=F32) + bg_ref[d])
        t_r, t_i = t[:, :CB], t[:, CB:]
        c2 = (-0.5 * LRU_C * LOG2_E) * _softplus(-lam_ref[d:d + 1, :])
        a = jnp.exp2(t_r * c2 + c2)
        w = (1.0 - a) * (1.0 + a)
        b = jnp.where(w > 0.0, w * lax.rsqrt(w), 0.0) * (t_i * xh + xh)
        for g in range(NG):
            gs = slice(g * LANES, (g + 1) * LANES)
            a_ref[g] = a[:, gs]
            b_ref[g] = b[:, gs]

    R = 8
    zeros, ones = jnp.zeros((R, LANES), F32), jnp.ones((R, LANES), F32)
    row = lax.broadcasted_iota(jnp.int32, (R, LANES), 0)

    def tile_scan(a, b, enter, rev):
        for d in (1, 2, 4):
            shift = R - d if rev else d
            keep = (row < R - d) if rev else (row >= d)
            a_s = pltpu.roll(a, shift, axis=0)
            b_s = pltpu.roll(b, shift, axis=0)
            b = jnp.where(keep, a * b_s, 0.0) + b
            a = jnp.where(keep, a * a_s, a)
        return a * enter + b

    def scan_part(base, n_rows, enter):
        seg = n_rows // R - 1
        assert seg % 2 == 1
        last = pl.ds(base + R * seg, R)

        def strided(i):
            return pl.ds(base + i, R, stride=seg)

        def packed(i):
            return pl.ds(pl.multiple_of(base + i * R, R), R)

        def local(i, carry):
            ib = seg - 1 - i
            out = []
            for g, (hf, pf, hb, pb) in enumerate(carry):
                a = af_ref[g, strided(i), :]
                hf = a * hf + bf_ref[g, strided(i), :]
                pf = a * pf
                hlf_ref[g, packed(i), :] = hf
                plf_ref[g, packed(i), :] = pf
                a = ab_ref[g, strided(ib), :]
                hb = a * hb + bb_ref[g, strided(ib), :]
                pb = a * pb
                hlb_ref[g, packed(ib), :] = hb
                plb_ref[g, packed(ib), :] = pb
                out.append((hf, pf, hb, pb))
            return tuple(out)

        ends = lax.fori_loop(0, seg, local, ((zeros, ones, zeros, ones),) * NG, unroll=4)

        leave, fixes = [], []
        for g, ((hf, pf, hb, pb), (c_f, c_b)) in enumerate(zip(ends, enter)):
            rows_f, rows_b = [], [None] * R
            for s in range(R):
                rows_f.append(c_f)
                c_f = hf[s:s + 1, :] + pf[s:s + 1, :] * c_f
            tail_f = tile_scan(af_ref[g, last, :], bf_ref[g, last, :], c_f, False)
            tail_b = tile_scan(ab_ref[g, last, :], bb_ref[g, last, :], c_b, True)
            c_f, c_b = tail_f[R - 1:R, :], tail_b[0:1, :]
            for s in reversed(range(R)):
                rows_b[s] = c_b
                c_b = hb[s:s + 1, :] + pb[s:s + 1, :] * c_b
            af_ref[g, last, :] = tail_f
            ab_ref[g, last, :] = tail_b
            leave.append((c_f, c_b))
            fixes.append((jnp.concatenate(rows_f, axis=0), jnp.concatenate(rows_b, axis=0)))

        def fix(i, carry):
            for g, (cf, cb) in enumerate(fixes):
                af_ref[g, strided(i), :] = hlf_ref[g, packed(i), :] + plf_ref[g, packed(i), :] * cf
                ab_ref[g, strided(i), :] = hlb_ref[g, packed(i), :] + plb_ref[g, packed(i), :] * cb
            return carry

        lax.fori_loop(0, seg, fix, 0, unroll=4)
        return leave

    row0 = jnp.zeros((1, LANES), F32)
    scan_part(ctx, S - ctx, scan_part(0, ctx, [(row0, row0)] * NG))

    g_all = g_ref[...].astype(F32)
    for g in range(NG):
        gs = slice(g * LANES, (g + 1) * LANES)
        y_ref[:, gs] = ((af_ref[g] + ab_ref[g]) * _silu(g_all[:, gs])).astype(y_ref.dtype)


def _lru_call(u, cw, cb, wg, bg, lam, l, ctx, ada_next=None):
    B, S, _ = u.shape
    ncb = LRU_W // LRU_CB
    in_specs = [
        pl.BlockSpec((None, S, LRU_CB), lambda b, c: (b, 0, U_LRU_X // LRU_CB + c)),
        pl.BlockSpec((None, S, LRU_CB), lambda b, c: (b, 0, U_LRU_G // LRU_CB + c)),
        pl.BlockSpec((None, LRU_CONV, LRU_CB), lambda b, c: (l, 0, c)),
        pl.BlockSpec((None, 1, LRU_CB), lambda b, c: (l, 0, c)),
        pl.BlockSpec((None, 2, None, LRU_CB, 2 * LRU_CB), lambda b, c: (l, 0, c, 0, 0)),
        pl.BlockSpec((None, 2, None, 1, 2 * LRU_CB), lambda b, c: (l, 0, c, 0, 0)),
        pl.BlockSpec((None, 2, LRU_CB), lambda b, c: (l, 0, c)),
    ]
    out_shape = [jax.ShapeDtypeStruct((B, S, LRU_W), BF16)]
    out_specs = [pl.BlockSpec((None, S, LRU_CB), lambda b, c: (b, 0, c))]
    args = [u, u, cw, cb, wg, bg, lam]
    if ada_next is not None:
        cond, w_ada, b_ada3 = ada_next
        _, D, D3 = w_ada.shape
        tn = D3 // (B * ncb)
        assert tn % LANES == 0
        in_specs += [
            pl.BlockSpec((8, D), lambda b, c: (0, 0)),
            pl.BlockSpec((None, D, tn), lambda b, c: (l + 1, 0, b * ncb + c)),
            pl.BlockSpec((None, 1, tn), lambda b, c: (l + 1, 0, b * ncb + c)),
        ]
        out_shape.append(jax.ShapeDtypeStruct((8, D3), F32))
        out_specs.append(pl.BlockSpec((8, tn), lambda b, c: (0, b * ncb + c)))
        args += [cond, w_ada, b_ada3]
    out = pl.pallas_call(
        functools.partial(_lru_kernel, ctx=ctx, ada=ada_next is not None),
        out_shape=out_shape,
        grid=(B, ncb),
        in_specs=in_specs,
        out_specs=out_specs,
        scratch_shapes=[pltpu.VMEM((LRU_CB // LANES, S, LANES), F32) for _ in range(8)],
        compiler_params=_cparams(("parallel", "parallel")),
        name="rg_lru",
    )(*args)
    return (out[0], out[1]) if ada_next is not None else (out[0], None)


def _ret_kernel(lg_ref, q_ref, k_ref, v_ref, g_ref, cos_ref, sin_ref, wo_src_ref, y_ref, wo_dst_ref,
                ks_ref, kv_ref, st_ref, *, n_ctx, layer):
    wo_dst_ref[...] = wo_src_ref[...].astype(BF16)
    C = RET_CHUNK
    S = q_ref.shape[0]
    n_all = S // C
    unroll = next(f for f in (9, 6, 3, 2, 1) if n_all % f == 0)
    hd = pl.program_id(1)
    lgf = lg_ref[2 * layer, hd]
    lgb = lg_ref[2 * layer + 1, hd]

    ri = lax.broadcasted_iota(jnp.int32, (C, C), 0).astype(F32)
    ci = lax.broadcasted_iota(jnp.int32, (C, C), 1).astype(F32)
    diff = ri - ci
    decay = jnp.where(diff >= 0, jnp.exp(jnp.maximum(diff, 0.0) * lgf),
                      jnp.exp(jnp.maximum(-diff, 0.0) * lgb))
    zeta_f = jnp.exp((C - 1 - ri) * lgf)
    xi_f = jnp.exp((ri + 1) * lgf)
    gc_f = jnp.exp(jnp.full((C, C), C, F32) * lgf)
    zeta_b = jnp.exp(ri * lgb)
    xi_b = jnp.exp((C - ri) * lgb)
    gc_b = jnp.exp(jnp.full((C, C), C, F32) * lgb)
    def rows_of(c):
        return pl.ds(pl.multiple_of(c * C, C), C)

    def chunk_kv(c, carry):
        sl = rows_of(c)
        k = _rope(k_ref[sl, :].astype(F32), cos_ref[sl, :], sin_ref[sl, :]) * RET_K_SCALE
        ks_ref[sl, :] = k.astype(BF16)
        kz = jnp.concatenate([k * zeta_f, k * zeta_b], axis=1).astype(BF16)
        kv_ref[c] = lax.dot_general(v_ref[sl, :].astype(BF16), kz, (((0,), (0,)), ((), ())),
                                    preferred_element_type=F32)
        return carry

    lax.fori_loop(0, n_all, chunk_kv, 0, unroll=unroll)

    def fwd_state(c, r):
        st_ref[c, :, 0:C] = r.astype(BF16)
        return gc_f * r + kv_ref[c, :, 0:C]

    def bwd_state(i, r):
        c = jnp.where(i < n_ctx, n_ctx - 1 - i, n_all - 1 - (i - n_ctx))
        st_ref[c, :, C:2 * C] = r.astype(BF16)
        return gc_b * r + kv_ref[c, :, C:2 * C]

    lax.fori_loop(0, n_all, fwd_state, jnp.zeros((C, C), F32))
    lax.fori_loop(0, n_all, bwd_state, jnp.zeros((C, C), F32))

    group = next(f for f in (6, 3, 2, 1) if n_all % f == 0)

    def chunks_out(i, carry):
        cs = [i * group + g for g in range(group)]
        sls = [rows_of(c) for c in cs]
        qs = [_rope(q_ref[sl, :].astype(F32), cos_ref[sl, :], sin_ref[sl, :]) for sl in sls]
        ss = [_dot_nt(q.astype(BF16), ks_ref[sl, :]) for q, sl in zip(qs, sls)]
        qxs = [jnp.concatenate([q * xi_f, q * xi_b], axis=1).astype(BF16) for q in qs]
        os = [jnp.dot((s * decay).astype(BF16), v_ref[sl, :].astype(BF16), preferred_element_type=F32)
              + _dot_nt(qx, st_ref[c]) for s, sl, qx, c in zip(ss, sls, qxs, cs)]
        for o, sl in zip(os, sls):
            mu = jnp.mean(o, axis=-1, keepdims=True)
            oc = o - mu
            var = jnp.mean(oc * oc, axis=-1, keepdims=True)
            y = oc * lax.rsqrt(var + LN_EPS) * _silu(g_ref[sl, :].astype(F32))
            y_ref[sl, :] = y.astype(y_ref.dtype)
        return carry

    lax.fori_loop(0, n_all // group, chunks_out, 0)


def _ret_call(u, log_g, l, cos, sin, ctx, w_out):
    B, S, _ = u.shape
    n_all = S // RET_CHUNK
    D = w_out.shape[1]
    tc = D // (B * RET_HEADS)
    assert tc % LANES == 0

    def col(base):
        return lambda b, h: (b, 0, base // RET_DH + h)

    return pl.pallas_call(
        functools.partial(_ret_kernel, n_ctx=ctx // RET_CHUNK, layer=l),
        out_shape=(jax.ShapeDtypeStruct((B, S, RET_W), BF16), jax.ShapeDtypeStruct((D, D), BF16)),
        grid=(B, RET_HEADS),
        in_specs=[
            pl.BlockSpec(memory_space=pltpu.SMEM),
            pl.BlockSpec((None, S, RET_DH), col(U_RET_Q)),
            pl.BlockSpec((None, S, RET_DH), col(U_RET_K)),
            pl.BlockSpec((None, S, RET_DH), col(U_RET_V)),
            pl.BlockSpec((None, S, RET_DH), col(U_RET_G)),
            pl.BlockSpec((S, RET_DH), lambda b, h: (0, 0)),
            pl.BlockSpec((S, RET_DH), lambda b, h: (0, 0)),
            pl.BlockSpec((None, D, tc), lambda b, h: (l, 0, b * RET_HEADS + h)),
        ],
        out_specs=(pl.BlockSpec((None, S, RET_DH), lambda b, h: (b, 0, h)),
                   pl.BlockSpec((D, tc), lambda b, h: (0, b * RET_HEADS + h))),
        scratch_shapes=[pltpu.VMEM((S, RET_DH), BF16),
                        pltpu.VMEM((n_all, RET_DH, 2 * RET_DH), F32),
                        pltpu.VMEM((n_all, RET_DH, 2 * RET_DH), BF16)],
        compiler_params=_cparams(("parallel", "parallel")),
        name="retention",
    )(log_g, u, u, u, u, cos, sin, w_out)


def _outproj_kernel(ym_ref, yl_ref, yr_ref, h_ref, gt_ref, w_ref, lng_ref, lnb_ref, o_ref,
                    *, tm, ctx, alpha, skip, ctx_row):
    b = pl.program_id(0)
    j = pl.program_id(1) + skip

    def project(r0):
        rs = slice(r0, r0 + SUB_ROWS)
        y = jnp.concatenate([ym_ref[rs, :], yl_ref[rs, :], yr_ref[rs, :]], axis=1)
        return jnp.dot(y, w_ref[...], preferred_element_type=F32)

    starts = list(range(0, tm, SUB_ROWS))
    acc_next = project(starts[0])
    for i, r0 in enumerate(starts):
        acc = acc_next
        if i + 1 < len(starts):
            acc_next = project(starts[i + 1])
        rs = slice(r0, r0 + SUB_ROWS)
        rows = j * tm + r0 + lax.broadcasted_iota(jnp.int32, (SUB_ROWS, 1), 0)
        gt = jnp.where(rows < ctx, gt_ref[ctx_row:ctx_row + 1, :], gt_ref[pl.ds(b, 1), :])
        z = alpha * h_ref[rs, :] + gt * acc
        mu = jnp.mean(z, axis=-1, keepdims=True)
        zc = z - mu
        var = jnp.mean(zc * zc, axis=-1, keepdims=True)
        o_ref[rs, :] = zc * lax.rsqrt(var + LN_EPS) * lng_ref[...] + lnb_ref[...]


def _outproj_call(ym, yl, yr, h, mod, w, lng, lnb, l, ctx, tm, alpha, latent_only):
    B, S, D = h.shape
    skip = ctx // tm if latent_only else 0
    assert skip * tm == (ctx if latent_only else 0)
    n_out = S // tm - skip
    return pl.pallas_call(
        functools.partial(_outproj_kernel, tm=tm, ctx=ctx, alpha=alpha, skip=skip, ctx_row=B),
        out_shape=jax.ShapeDtypeStruct((B, n_out * tm, D), F32),
        grid=(B, n_out),
        in_specs=[
            pl.BlockSpec((None, tm, MLA_W), lambda b, j: (b, j + skip, 0)),
            pl.BlockSpec((None, tm, LRU_W), lambda b, j: (b, j + skip, 0)),
            pl.BlockSpec((None, tm, RET_W), lambda b, j: (b, j + skip, 0)),
            pl.BlockSpec((None, tm, D), lambda b, j: (b, j + skip, 0)),
            pl.BlockSpec((8, D), lambda b, j: (0, 2)),
            pl.BlockSpec((D, D), lambda b, j: (0, 0), pipeline_mode=pl.Buffered(1)),
            pl.BlockSpec((None, 1, D), lambda b, j: (l, 0, 0)),
            pl.BlockSpec((None, 1, D), lambda b, j: (l, 0, 0)),
        ],
        out_specs=pl.BlockSpec((None, tm, D), lambda b, j: (b, j, 0)),
        compiler_params=_cparams(("parallel", "parallel")),
        name="out_proj_ln",
    )(ym, yl, yr, h, mod, w, lng, lnb)


def _rope_tables(rows, dim, ctx):
    row = jnp.repeat(jnp.arange(rows, dtype=F32), GRID_W)
    col = jnp.tile(jnp.arange(GRID_W, dtype=F32), rows)
    quarter = dim // 4
    inv = ROPE_BASE ** (-jnp.arange(quarter, dtype=F32) / quarter)
    ang = jnp.stack([row[:, None] * inv, col[:, None] * inv], axis=1)
    cos, sin = jnp.cos(ang), jnp.sin(ang)
    pad = ((0, 0), (0, LANES // 2 - 2 * quarter))
    cos_h = jnp.pad(jnp.concatenate([cos[:, 0], cos[:, 1]], axis=-1), pad)
    sin_h = jnp.pad(jnp.concatenate([sin[:, 0], sin[:, 1]], axis=-1), pad)
    cos_l = jnp.concatenate([cos_h, cos_h], axis=-1)
    sin_l = jnp.concatenate([-sin_h, sin_h], axis=-1)
    cos_c = jnp.ones((ctx, LANES), F32)
    sin_c = jnp.zeros((ctx, LANES), F32)
    return jnp.concatenate([cos_c, cos_l], axis=0), jnp.concatenate([sin_c, sin_l], axis=0)


def _rotary_lane_order(dim):
    quarter = dim // 4
    return ((0, 0), (quarter, LANES // 2), (2 * quarter, quarter), (3 * quarter, LANES // 2 + quarter))


def _rotary_moves(src, dst, dim, groups):
    return tuple((src + g * dim + s, dst + g * LANES + d, dim // 4)
                 for g in range(groups) for s, d in _rotary_lane_order(dim))


W_IN_MOVES = ((0, U_QLAT, 512), (512, U_KVLAT, 256), (832, U_MLA_G, 1024),
              (1856, U_LRU_X, 512), (2368, U_LRU_G, 512), (3904, U_RET_V, 512), (4416, U_RET_G, 512)
              ) + _rotary_moves(768, U_KR, MLA_ROPE, 1) + _rotary_moves(2880, U_RET_Q, RET_DH, RET_HEADS
              ) + _rotary_moves(3392, U_RET_K, RET_DH, RET_HEADS)
W_IN_ZERO = tuple((U_KR + o, LANES // 2 - MLA_ROPE // 2) for o in (MLA_ROPE // 2, LANES // 2 + MLA_ROPE // 2)
                  ) + ((U_USED, U_W - U_USED),)


def _w_in_layout_kernel(w_ref, o_ref):
    for src, dst, width in W_IN_MOVES:
        o_ref[dst:dst + width, :] = w_ref[src:src + width, :].astype(BF16)
    for dst, width in W_IN_ZERO:
        o_ref[dst:dst + width, :] = jnp.zeros((width, o_ref.shape[1]), BF16)


def _layout_w_in(w_in_t, layer):
    _, W, D = w_in_t.shape
    tc = 512
    return pl.pallas_call(
        _w_in_layout_kernel,
        out_shape=jax.ShapeDtypeStruct((U_W, D), BF16),
        grid=(D // tc,),
        in_specs=[pl.BlockSpec((None, W, tc), lambda r: (layer, 0, r))],
        out_specs=pl.BlockSpec((U_W, tc), lambda r: (0, r)),
        compiler_params=_cparams(("parallel",)),
        name="w_in_layout",
    )(w_in_t)


def _layout_w_uq(w_uq):
    L = w_uq.shape[0]
    w = w_uq.reshape(L, MLA_Q_RANK, MLA_HEADS, MLA_NOPE + MLA_ROPE)
    quarter = MLA_ROPE // 4
    pieces, lane = [w[..., :MLA_NOPE]], 0
    for s, d in sorted(_rotary_lane_order(MLA_ROPE), key=lambda sd: sd[1]):
        pieces += [jnp.zeros(w.shape[:-1] + (d - lane,), w.dtype),
                   w[..., MLA_NOPE + s:MLA_NOPE + s + quarter]]
        lane = d + quarter
    pieces.append(jnp.zeros(w.shape[:-1] + (LANES - lane,), w.dtype))
    w = jnp.concatenate(pieces, axis=-1)
    return w.reshape(L, MLA_Q_RANK, MLA_HEADS * MLA_HP).astype(BF16)


def _layout_w_ukv(w_ukv):
    L = w_ukv.shape[0]
    w = w_ukv.reshape(L, MLA_KV_RANK, MLA_HEADS, MLA_NOPE + MLA_V)
    wk = w[..., :MLA_NOPE].reshape(L, MLA_KV_RANK, MLA_W).astype(BF16)
    wvt = w[..., MLA_NOPE:].reshape(L, MLA_KV_RANK, MLA_W).transpose(0, 2, 1).astype(BF16)
    return wk, wvt


def _layout_lru_gates(w_r, b_r, w_i, b_i):
    L = w_r.shape[0]
    ncb = LRU_W // LRU_CB
    per = LRU_BLOCKS // ncb
    eye = jnp.eye(per, dtype=w_r.dtype)

    def diag_blocks(w):
        w = w.reshape(L, 2, ncb, per, LRU_BW, LRU_BW)
        return jnp.einsum("ldcgij,gh->ldcgihj", w, eye).reshape(L, 2, ncb, LRU_CB, LRU_CB)

    wg = jnp.concatenate([diag_blocks(w_r), diag_blocks(w_i)], axis=-1).astype(BF16)
    bg = jnp.concatenate([b_r.reshape(L, 2, ncb, 1, LRU_CB), b_i.reshape(L, 2, ncb, 1, LRU_CB)], axis=-1)
    return 0.5 * wg, 0.5 * bg


def kernel(x, c, ctx, c_ctx, w_ada, b_ada, w_in, mla_q_norm_g, mla_kv_norm_g, mla_w_uq, mla_w_ukv,
           lru_conv_w, lru_conv_b, lru_w_r, lru_b_r, lru_w_i, lru_b_i, lru_lambda, ret_decay,
           w_out, ln_g, ln_b):
    B, T, D = x.shape
    L = w_in.shape[0]
    n_ctx = ctx.shape[1]
    S = n_ctx + T
    assert D == 2 * MLA_W and w_in.shape[2] == 4928 and n_ctx % 256 == 0 and T % 256 == 0
    assert B + 1 <= 8
    alpha = (2 * L) ** 0.25
    tm = 768 if S % 768 == 0 else 256

    cos_m, sin_m = _rope_tables(T // GRID_W, MLA_ROPE, n_ctx)
    cos_r, sin_r = _rope_tables(T // GRID_W, RET_DH, n_ctx)

    cond = jnp.zeros((8, D), F32).at[:B].set(c).at[B].set(c_ctx)
    b_ada3 = b_ada[:, None, :]
    mod = _ada_call(cond, w_ada, b_ada3, 0)

    h = _ln0_call(ctx, x)
    log_g = jax.nn.log_sigmoid(ret_decay.astype(F32)).reshape(2 * L, RET_HEADS)

    w_in_t = jnp.swapaxes(w_in, 1, 2)
    w_t = _layout_w_in(w_in_t, 0)
    wuq = _layout_w_uq(mla_w_uq)
    wk, wvt = _layout_w_ukv(mla_w_ukv)
    wg, bg = _layout_lru_gates(lru_w_r, lru_b_r, lru_w_i, lru_b_i)
    gq, gkv = mla_q_norm_g[:, None, :], mla_kv_norm_g[:, None, :]
    conv_b, lng, lnb = lru_conv_b[:, None, :], ln_g[:, None, :], ln_b[:, None, :]

    for l in range(L):
        last = l == L - 1
        u = _inproj_call(h, mod, w_t, n_ctx, tm)
        q, k, vt = _mla_prep_call(u, gq, gkv, wuq, wk, wvt, l, cos_m, sin_m)
        y_mla, w_t = _attn_call(q, k, vt, u, n_ctx, *(() if last else (w_in_t, l + 1)))
        y_lru, mod_next = _lru_call(u, lru_conv_w, conv_b, wg, bg, lru_lambda, l, n_ctx,
                                    None if last else (cond, w_ada, b_ada3))
        y_ret, w_out_b = _ret_call(u, log_g, l, cos_r, sin_r, n_ctx, w_out)
        h = _outproj_call(y_mla, y_lru, y_ret, h, mod, w_out_b, lng, lnb, l, n_ctx,
                          256 if last else tm, alpha, last)
        mod = mod_next
    return h
```

```python
import functools

import jax
import jax.numpy as jnp
from jax import lax
from jax.experimental import pallas as pl
from jax.experimental.pallas import tpu as pltpu

F32 = jnp.float32
BF16 = jnp.bfloat16

GRID_W = 64
MLA_V = 128
MLA_NOPE = 128
MLA_ROPE = 64
MLA_HEADS = 8
MLA_W = MLA_HEADS * MLA_V
MLA_Q_RANK = 512
MLA_KV_RANK = 256
MLA_SCALE = (MLA_NOPE + MLA_ROPE) ** -0.5
LOG2_E = 1.4426950408889634
MLA_Q_SCALE = MLA_SCALE * LOG2_E
MLA_HP = 256
LRU_W = 512
LRU_BLOCKS = 8
LRU_BW = LRU_W // LRU_BLOCKS
LRU_CONV = 4
LRU_C = 8.0
LRU_CB = 256
RET_HEADS = 4
RET_DH = 128
RET_W = RET_HEADS * RET_DH
RET_CHUNK = 128
RET_K_SCALE = RET_DH ** -0.5
ROPE_BASE = 10000.0
LN_EPS = 1e-5
RMS_EPS = 1e-6
LANES = 128
BF16_ROWS = 16
KEY_CHUNK = 768
ATTN_BUFS = 3
ATTN_HEADS = 2
SUB_ROWS = 256

U_QLAT = 0
U_LRU_X = 512
U_LRU_G = 1024
U_RET_Q = 1536
U_RET_K = 2048
U_RET_V = 2560
U_RET_G = 3072
U_MLA_G = 3584
U_KVLAT = 4608
U_KR = 4864
U_USED = 4992
MXU_COLS = 256
U_CHUNKS = 2
U_TN = -(-U_USED // (U_CHUNKS * MXU_COLS)) * MXU_COLS
U_W = U_CHUNKS * U_TN

V7X_VMEM_BYTES = 64 * 1024 * 1024
VMEM_LIMIT = V7X_VMEM_BYTES // 8 * 7


def _cparams(sem):
    return pltpu.CompilerParams(dimension_semantics=sem, vmem_limit_bytes=VMEM_LIMIT)


def _dot_nt(a, b):
    return lax.dot_general(a, b, (((1,), (1,)), ((), ())), preferred_element_type=F32)


def _sigmoid(x):
    return 0.5 * jnp.tanh(0.5 * x) + 0.5


def _silu(x):
    return x * _sigmoid(x)


def _softplus(x):
    return jnp.maximum(x, 0.0) + jnp.log1p(jnp.exp(-jnp.abs(x)))


def _rope(x, cos, sin):
    return x * cos + pltpu.roll(x, LANES // 2, axis=x.ndim - 1) * sin


def _ada_kernel(cond_ref, w_ref, b_ref, o_ref):
    c = cond_ref[...]
    a = _silu(c).astype(BF16)
    o_ref[...] = jnp.dot(a, w_ref[...].astype(BF16), preferred_element_type=F32) + b_ref[...]


def _ada_call(cond, w_ada, b_ada3, layer):
    _, D, D3 = w_ada.shape
    tn = 1024
    return pl.pallas_call(
        _ada_kernel,
        out_shape=jax.ShapeDtypeStruct((8, D3), F32),
        grid=(D3 // tn,),
        in_specs=[
            pl.BlockSpec((8, D), lambda n: (0, 0)),
            pl.BlockSpec((None, D, tn), lambda n: (layer, 0, n)),
            pl.BlockSpec((None, 1, tn), lambda n: (layer, 0, n)),
        ],
        out_specs=pl.BlockSpec((8, tn), lambda n: (0, n)),
        compiler_params=_cparams(("parallel",)),
        name="ada_mod",
    )(cond, w_ada, b_ada3)


def _ln0_kernel(c_ref, *refs):
    x_refs, o_ref = refs[:-1], refs[-1]
    rows = c_ref.shape[0]

    def norm(rs, ref):
        x = ref[...]
        mu = jnp.mean(x, axis=-1, keepdims=True)
        xc = x - mu
        var = jnp.mean(xc * xc, axis=-1, keepdims=True)
        o_ref[rs, :] = xc * lax.rsqrt(var + LN_EPS)

    j = pl.program_id(1)
    for k, x_ref in enumerate(x_refs):
        rs = slice(k * rows, (k + 1) * rows)
        if k == 0:
            pl.when(j == 0)(functools.partial(norm, rs, c_ref))
            pl.when(j > 0)(functools.partial(norm, rs, x_ref))
        else:
            norm(rs, x_ref)


def _ln0_call(ctx, x):
    B, T, D = x.shape
    rows = ctx.shape[1]
    n_blocks = (rows + T) // rows
    group = next(g for g in (3, 2, 1) if n_blocks % g == 0)
    assert T % rows == 0

    def x_spec(k):
        return pl.BlockSpec((None, rows, D), lambda b, j: (b, jnp.maximum(j * group + k - 1, 0), 0))

    return pl.pallas_call(
        _ln0_kernel,
        out_shape=jax.ShapeDtypeStruct((B, rows + T, D), F32),
        grid=(B, n_blocks // group),
        in_specs=[pl.BlockSpec((None, rows, D), lambda b, j: (b, 0, 0))] + [x_spec(k) for k in range(group)],
        out_specs=pl.BlockSpec((None, group * rows, D), lambda b, j: (b, j, 0)),
        compiler_params=_cparams(("parallel", "parallel")),
        name="ln_entry",
    )(ctx, *([x] * group))


def _inproj_kernel(h_ref, sh_ref, sc_ref, *rest, tm, ctx, ctx_row):
    w_refs, (u_ref, xs_ref) = rest[:U_CHUNKS], rest[U_CHUNKS:]
    b = pl.program_id(0)
    j = pl.program_id(1)
    n = pl.program_id(2)

    @pl.when(n == 0)
    def _():
        for r0 in range(0, tm, SUB_ROWS):
            rs = slice(r0, r0 + SUB_ROWS)
            rows = j * tm + r0 + lax.broadcasted_iota(jnp.int32, (SUB_ROWS, 1), 0)
            is_ctx = rows < ctx
            sh = jnp.where(is_ctx, sh_ref[ctx_row:ctx_row + 1, :], sh_ref[pl.ds(b, 1), :])
            sc = jnp.where(is_ctx, sc_ref[ctx_row:ctx_row + 1, :], sc_ref[pl.ds(b, 1), :])
            xs = (h_ref[rs, :] * (1.0 + sc) + sh).astype(BF16)
            xs_ref[rs, :] = xs
            u_ref[rs, :] = _dot_nt(xs, w_refs[0][...]).astype(u_ref.dtype)

    for c in range(1, U_CHUNKS):
        @pl.when(n == c)
        def _():
            u_ref[...] = _dot_nt(xs_ref[...], w_refs[c][...]).astype(u_ref.dtype)


def _inproj_call(h, mod, w_t, ctx, tm):
    B, S, D = h.shape
    return pl.pallas_call(
        functools.partial(_inproj_kernel, tm=tm, ctx=ctx, ctx_row=B),
        out_shape=jax.ShapeDtypeStruct((B, S, U_W), F32),
        grid=(B, S // tm, U_CHUNKS),
        in_specs=[
            pl.BlockSpec((None, tm, D), lambda b, j, n: (b, j, 0)),
            pl.BlockSpec((8, D), lambda b, j, n: (0, 0)),
            pl.BlockSpec((8, D), lambda b, j, n: (0, 1)),
        ] + [
            pl.BlockSpec((U_TN, D), functools.partial(lambda b, j, n, c: (c, 0), c=c),
                         pipeline_mode=pl.Buffered(1))
            for c in range(U_CHUNKS)
        ],
        out_specs=pl.BlockSpec((None, tm, U_TN), lambda b, j, n: (b, j, n)),
        scratch_shapes=[pltpu.VMEM((tm, D), BF16)],
        compiler_params=_cparams(("parallel", "parallel", "arbitrary")),
        name="in_proj",
    )(h, mod, mod, *([w_t] * U_CHUNKS))


def _rms(x, g):
    return x * lax.rsqrt(jnp.mean(x * x, axis=-1, keepdims=True) + RMS_EPS) * g


def _mla_prep_kernel(ql_ref, kvl_ref, kr_ref, gq_ref, gkv_ref, wuq_ref, wk_ref, wvt_ref,
                     cos_ref, sin_ref, q_ref, k_ref, vt_ref):
    cos = cos_ref[...]
    sin = sin_ref[...]

    zq = _rms(ql_ref[...].astype(F32), gq_ref[...]).astype(BF16)
    q = jnp.dot(zq, wuq_ref[...], preferred_element_type=F32)
    for hd in range(MLA_HEADS):
        c0 = hd * MLA_HP
        q_ref[:, c0:c0 + MLA_NOPE] = (q[:, c0:c0 + MLA_NOPE] * MLA_Q_SCALE).astype(q_ref.dtype)
        qr = _rope(q[:, c0 + MLA_NOPE:c0 + MLA_HP], cos, sin)
        q_ref[:, c0 + MLA_NOPE:c0 + MLA_HP] = (qr * MLA_Q_SCALE).astype(q_ref.dtype)

    zk = _rms(kvl_ref[...].astype(F32), gkv_ref[...]).astype(BF16)
    kn = jnp.dot(zk, wk_ref[...], preferred_element_type=F32)
    kr = _rope(kr_ref[...].astype(F32), cos, sin).astype(k_ref.dtype)
    for hd in range(MLA_HEADS):
        c0 = hd * MLA_HP
        k_ref[:, c0:c0 + MLA_NOPE] = kn[:, hd * MLA_NOPE:(hd + 1) * MLA_NOPE].astype(k_ref.dtype)
        k_ref[:, c0 + MLA_NOPE:c0 + MLA_HP] = kr
    vt = lax.dot_general(wvt_ref[...], zk, (((1,), (1,)), ((), ())), preferred_element_type=F32)
    vt_ref[...] = vt.astype(vt_ref.dtype)


def _mla_prep_call(u, gq, gkv, wuq, wk, wvt, l, cos, sin):
    B, S, _ = u.shape
    tp = 768 if S % 768 == 0 else 256
    HW = MLA_HEADS * MLA_HP
    return pl.pallas_call(
        _mla_prep_kernel,
        out_shape=(
            jax.ShapeDtypeStruct((B, S, HW), BF16),
            jax.ShapeDtypeStruct((B, S, HW), BF16),
            jax.ShapeDtypeStruct((B, MLA_W, S), BF16),
        ),
        grid=(B, S // tp),
        in_specs=[
            pl.BlockSpec((None, tp, MLA_Q_RANK), lambda b, j: (b, j, U_QLAT // MLA_Q_RANK)),
            pl.BlockSpec((None, tp, MLA_KV_RANK), lambda b, j: (b, j, U_KVLAT // MLA_KV_RANK)),
            pl.BlockSpec((None, tp, LANES), lambda b, j: (b, j, U_KR // LANES)),
            pl.BlockSpec((None, 1, MLA_Q_RANK), lambda b, j: (l, 0, 0)),
            pl.BlockSpec((None, 1, MLA_KV_RANK), lambda b, j: (l, 0, 0)),
            pl.BlockSpec((None, MLA_Q_RANK, HW), lambda b, j: (l, 0, 0)),
            pl.BlockSpec((None, MLA_KV_RANK, MLA_W), lambda b, j: (l, 0, 0)),
            pl.BlockSpec((None, MLA_W, MLA_KV_RANK), lambda b, j: (l, 0, 0)),
            pl.BlockSpec((tp, LANES), lambda b, j: (j, 0)),
            pl.BlockSpec((tp, LANES), lambda b, j: (j, 0)),
        ],
        out_specs=(
            pl.BlockSpec((None, tp, HW), lambda b, j: (b, j, 0)),
            pl.BlockSpec((None, tp, HW), lambda b, j: (b, j, 0)),
            pl.BlockSpec((None, MLA_W, tp), lambda b, j: (b, 0, j)),
        ),
        compiler_params=_cparams(("parallel", "parallel")),
        name="mla_prep",
    )(u, u, u, gq, gkv, wuq, wk, wvt, cos, sin)


def _attn_kernel(*refs, ctx, tq, heads, relayout):
    if relayout:
        q_ref, k_ref, vt_ref, g_ref, w_src_ref, y_ref, w_dst_ref = refs[:7]
        _w_in_layout_kernel(w_src_ref, w_dst_ref)
    else:
        q_ref, k_ref, vt_ref, g_ref, y_ref = refs[:5]
    st_ref, m_ref, p_ref, vt1_ref = refs[-4:]
    S = k_ref.shape[0]
    n_lat = (S - ctx) // tq
    assert n_lat >= 2 and n_lat * tq == S - ctx and S % KEY_CHUNK == 0
    items = [(hd, blk) for hd in range(heads) for blk in range(1, n_lat + 1)]

    for hd in range(heads):
        vt1_ref[hd, 0:MLA_V, :] = vt_ref[hd * MLA_V:(hd + 1) * MLA_V, :]
        vt1_ref[hd, MLA_V:, :] = jnp.ones((BF16_ROWS, S), BF16)

    def rows_of(blk):
        return slice(0, ctx) if blk == 0 else slice(ctx + (blk - 1) * tq, ctx + blk * tq)

    def qk_cols(hd):
        return slice(hd * MLA_HP, (hd + 1) * MLA_HP)

    def numerators(st, m):
        return jnp.exp2(st - m).astype(BF16)

    def write_out(hd, blk, acc):
        rows, cols = rows_of(blk), slice(hd * MLA_V, (hd + 1) * MLA_V)
        o = (acc[0:MLA_V, :] / acc[MLA_V:MLA_V + 1, :]).T
        y_ref[rows, cols] = (o * _silu(g_ref[rows, cols].astype(F32))).astype(y_ref.dtype)

    def step(score=None, num=None, att=None):
        def pick(i):
            return (None, None, None) if i is None or i >= len(items) else (i % ATTN_BUFS,) + items[i]

        (sb, s_hd, s_blk), (nb, _, _), (ab, a_hd, a_blk) = pick(score), pick(num), pick(att)
        qb = None if sb is None else q_ref[rows_of(s_blk), qk_cols(s_hd)]
        m_num = None if nb is None else m_ref[nb]
        m_run, acc = None, None
        for c0 in range(0, S, KEY_CHUNK):
            ks = slice(c0, c0 + KEY_CHUNK)
            if nb is not None:
                p_ref[nb, ks, :] = numerators(st_ref[nb, ks, :], m_num)
            if sb is not None:
                st = _dot_nt(k_ref[ks, qk_cols(s_hd)], qb)
                st_ref[sb, ks, :] = st
                m_c = jnp.max(st, axis=0, keepdims=True)
                m_run = m_c if m_run is None else jnp.maximum(m_run, m_c)
            if ab is not None:
                part = jnp.dot(vt1_ref[a_hd, :, ks], p_ref[ab, ks, :], preferred_element_type=F32)
                acc = part if acc is None else acc + part
        if sb is not None:
            m_ref[sb] = m_run
        if ab is not None:
            write_out(a_hd, a_blk, acc)

    step(score=0)
    step(score=1, num=0)
    for hd in range(heads):
        st = _dot_nt(k_ref[0:ctx, qk_cols(hd)], q_ref[rows_of(0), qk_cols(hd)])
        p = numerators(st, jnp.max(st, axis=0, keepdims=True))
        write_out(hd, 0, jnp.dot(vt1_ref[hd, :, 0:ctx], p, preferred_element_type=F32))
    for t in range(len(items)):
        step(score=t + 2, num=t + 1, att=t)


def _attn_call(q, k, vt, u, ctx, w_in_t=None, next_layer=None):
    B, S, _ = q.shape
    tq = next(t for t in (512, 256) if (S - ctx) % t == 0 and (S - ctx) // t >= 2)
    nh = ATTN_HEADS
    gcol = U_MLA_G // (nh * MLA_V)
    assert MLA_HEADS % nh == 0 and U_MLA_G % (nh * MLA_V) == 0
    n_h = MLA_HEADS // nh
    in_specs = [
        pl.BlockSpec((None, S, nh * MLA_HP), lambda b, h: (b, 0, h)),
        pl.BlockSpec((None, S, nh * MLA_HP), lambda b, h: (b, 0, h)),
        pl.BlockSpec((None, nh * MLA_V, S), lambda b, h: (b, h, 0)),
        pl.BlockSpec((None, S, nh * MLA_V), lambda b, h: (b, 0, gcol + h)),
    ]
    out_shape = [jax.ShapeDtypeStruct((B, S, MLA_W), BF16)]
    out_specs = [pl.BlockSpec((None, S, nh * MLA_V), lambda b, h: (b, 0, h))]
    args = [q, k, vt, u]
    relayout = w_in_t is not None
    if relayout:
        _, W, D = w_in_t.shape
        tc = D // (B * n_h)
        assert tc % LANES == 0
        in_specs.append(pl.BlockSpec((None, W, tc), lambda b, h: (next_layer, 0, b * n_h + h)))
        out_shape.append(jax.ShapeDtypeStruct((U_W, D), BF16))
        out_specs.append(pl.BlockSpec((U_W, tc), lambda b, h: (0, b * n_h + h)))
        args.append(w_in_t)
    out = pl.pallas_call(
        functools.partial(_attn_kernel, ctx=ctx, tq=tq, heads=nh, relayout=relayout),
        out_shape=out_shape,
        grid=(B, n_h),
        in_specs=in_specs,
        out_specs=out_specs,
        scratch_shapes=[pltpu.VMEM((ATTN_BUFS, S, tq), F32), pltpu.VMEM((ATTN_BUFS, 1, tq), F32),
                        pltpu.VMEM((ATTN_BUFS, S, tq), BF16),
                        pltpu.VMEM((nh, MLA_V + BF16_ROWS, S), BF16)],
        compiler_params=_cparams(("parallel", "parallel")),
        name="mla_attn",
    )(*args)
    return (out[0], out[1]) if relayout else (out[0], None)


def _lru_kernel(*refs, ctx, ada):
    x_ref, g_ref, cw_ref, cb_ref, wg_ref, bg_ref, lam_ref = refs[:7]
    af_ref, bf_ref, ab_ref, bb_ref, hlf_ref, plf_ref, hlb_ref, plb_ref = refs[-8:]
    if ada:
        cond_ref, wa_ref, ba_ref, y_ref, mod_ref = refs[7:12]
        _ada_kernel(cond_ref, wa_ref, ba_ref, mod_ref)
    else:
        y_ref = refs[7]
    S, CB = x_ref.shape
    NG = CB // LANES
    x = x_ref[...].astype(F32)
    rows = lax.broadcasted_iota(jnp.int32, (S, 1), 0)
    in_ctx = rows < ctx

    xc = x * cw_ref[2:3, :] + cb_ref[...]
    for tap, off in ((0, -2), (1, -1), (3, 1)):
        xs = pltpu.roll(x, (-off) % S, axis=0)
        src = rows + off
        valid = (src >= 0) & (src < S) & ((src < ctx) == in_ctx)
        xc = xc + jnp.where(valid, xs, 0.0) * cw_ref[tap:tap + 1, :]

    xb = xc.astype(BF16)
    xh = 0.5 * xc
    for d, (a_ref, b_ref) in enumerate(((af_ref, bf_ref), (ab_ref, bb_ref))):
        t = jnp.tanh(jnp.dot(xb, wg_ref[d], preferred_element_type=F32) + bg_ref[d])
        t_r, t_i = t[:, :CB], t[:, CB:]
        c2 = (-0.5 * LRU_C * LOG2_E) * _softplus(-lam_ref[d:d + 1, :])
        a = jnp.exp2(t_r * c2 + c2)
        w = (1.0 - a) * (1.0 + a)
        b = jnp.where(w > 0.0, w * lax.rsqrt(w), 0.0) * (t_i * xh + xh)
        for g in range(NG):
            gs = slice(g * LANES, (g + 1) * LANES)
            a_ref[g] = a[:, gs]
            b_ref[g] = b[:, gs]

    R = 8
    zeros, ones = jnp.zeros((R, LANES), F32), jnp.ones((R, LANES), F32)
    row = lax.broadcasted_iota(jnp.int32, (R, LANES), 0)

    def tile_scan(a, b, enter, rev):
        for d in (1, 2, 4):
            shift = R - d if rev else d
            keep = (row < R - d) if rev else (row >= d)
            a_s = pltpu.roll(a, shift, axis=0)
            b_s = pltpu.roll(b, shift, axis=0)
            b = jnp.where(keep, a * b_s, 0.0) + b
            a = jnp.where(keep, a * a_s, a)
        return a * enter + b

    def scan_part(base, n_rows, enter):
        seg = n_rows // R - 1
        assert seg % 2 == 1
        last = pl.ds(base + R * seg, R)

        def strided(i):
            return pl.ds(base + i, R, stride=seg)

        def packed(i):
            return pl.ds(pl.multiple_of(base + i * R, R), R)

        def local(i, carry):
            ib = seg - 1 - i
            out = []
            for g, (hf, pf, hb, pb) in enumerate(carry):
                a = af_ref[g, strided(i), :]
                hf = a * hf + bf_ref[g, strided(i), :]
                pf = a * pf
                hlf_ref[g, packed(i), :] = hf
                plf_ref[g, packed(i), :] = pf
                a = ab_ref[g, strided(ib), :]
                hb = a * hb + bb_ref[g, strided(ib), :]
                pb = a * pb
                hlb_ref[g, packed(ib), :] = hb
                plb_ref[g, packed(ib), :] = pb
                out.append((hf, pf, hb, pb))
            return tuple(out)

        ends = lax.fori_loop(0, seg, local, ((zeros, ones, zeros, ones),) * NG, unroll=4)

        leave, fixes = [], []
        for g, ((hf, pf, hb, pb), (c_f, c_b)) in enumerate(zip(ends, enter)):
            rows_f, rows_b = [], [None] * R
            for s in range(R):
                rows_f.append(c_f)
                c_f = hf[s:s + 1, :] + pf[s:s + 1, :] * c_f
            tail_f = tile_scan(af_ref[g, last, :], bf_ref[g, last, :], c_f, False)
            tail_b = tile_scan(ab_ref[g, last, :], bb_ref[g, last, :], c_b, True)
            c_f, c_b = tail_f[R - 1:R, :], tail_b[0:1, :]
            for s in reversed(range(R)):
                rows_b[s] = c_b
                c_b = hb[s:s + 1, :] + pb[s:s + 1, :] * c_b
            af_ref[g, last, :] = tail_f
            ab_ref[g, last, :] = tail_b
            leave.append((c_f, c_b))
            fixes.append((jnp.concatenate(rows_f, axis=0), jnp.concatenate(rows_b, axis=0)))

        def fix(i, carry):
            for g, (cf, cb) in enumerate(fixes):
                af_ref[g, strided(i), :] = hlf_ref[g, packed(i), :] + plf_ref[g, packed(i), :] * cf
                ab_ref[g, strided(i), :] = hlb_ref[g, packed(i), :] + plb_ref[g, packed(i), :] * cb
            return carry

        lax.fori_loop(0, seg, fix, 0, unroll=4)
        return leave

    row0 = jnp.zeros((1, LANES), F32)
    scan_part(ctx, S - ctx, scan_part(0, ctx, [(row0, row0)] * NG))

    g_all = g_ref[...].astype(F32)
    for g in range(NG):
        gs = slice(g * LANES, (g + 1) * LANES)
        y_ref[:, gs] = ((af_ref[g] + ab_ref[g]) * _silu(g_all[:, gs])).astype(y_ref.dtype)


def _lru_call(u, cw, cb, wg, bg, lam, l, ctx, ada_next=None):
    B, S, _ = u.shape
    ncb = LRU_W // LRU_CB
    in_specs = [
        pl.BlockSpec((None, S, LRU_CB), lambda b, c: (b, 0, U_LRU_X // LRU_CB + c)),
        pl.BlockSpec((None, S, LRU_CB), lambda b, c: (b, 0, U_LRU_G // LRU_CB + c)),
        pl.BlockSpec((None, LRU_CONV, LRU_CB), lambda b, c: (l, 0, c)),
        pl.BlockSpec((None, 1, LRU_CB), lambda b, c: (l, 0, c)),
        pl.BlockSpec((None, 2, None, LRU_CB, 2 * LRU_CB), lambda b, c: (l, 0, c, 0, 0)),
        pl.BlockSpec((None, 2, None, 1, 2 * LRU_CB), lambda b, c: (l, 0, c, 0, 0)),
        pl.BlockSpec((None, 2, LRU_CB), lambda b, c: (l, 0, c)),
    ]
    out_shape = [jax.ShapeDtypeStruct((B, S, LRU_W), BF16)]
    out_specs = [pl.BlockSpec((None, S, LRU_CB), lambda b, c: (b, 0, c))]
    args = [u, u, cw, cb, wg, bg, lam]
    if ada_next is not None:
        cond, w_ada, b_ada3 = ada_next
        _, D, D3 = w_ada.shape
        tn = D3 // (B * ncb)
        assert tn % LANES == 0
        in_specs += [
            pl.BlockSpec((8, D), lambda b, c: (0, 0)),
            pl.BlockSpec((None, D, tn), lambda b, c: (l + 1, 0, b * ncb + c)),
            pl.BlockSpec((None, 1, tn), lambda b, c: (l + 1, 0, b * ncb + c)),
        ]
        out_shape.append(jax.ShapeDtypeStruct((8, D3), F32))
        out_specs.append(pl.BlockSpec((8, tn), lambda b, c: (0, b * ncb + c)))
        args += [cond, w_ada, b_ada3]
    out = pl.pallas_call(
        functools.partial(_lru_kernel, ctx=ctx, ada=ada_next is not None),
        out_shape=out_shape,
        grid=(B, ncb),
        in_specs=in_specs,
        out_specs=out_specs,
        scratch_shapes=[pltpu.VMEM((LRU_CB // LANES, S, LANES), F32) for _ in range(8)],
        compiler_params=_cparams(("parallel", "parallel")),
        name="rg_lru",
    )(*args)
    return (out[0], out[1]) if ada_next is not None else (out[0], None)


def _ret_kernel(lg_ref, q_ref, k_ref, v_ref, g_ref, cos_ref, sin_ref, wo_src_ref, y_ref, wo_dst_ref,
                ks_ref, kv_ref, st_ref, *, n_ctx, layer):
    wo_dst_ref[...] = wo_src_ref[...].astype(BF16)
    C = RET_CHUNK
    S = q_ref.shape[0]
    n_all = S // C
    unroll = next(f for f in (9, 6, 3, 2, 1) if n_all % f == 0)
    hd = pl.program_id(1)
    lgf = lg_ref[2 * layer, hd]
    lgb = lg_ref[2 * layer + 1, hd]

    ri = lax.broadcasted_iota(jnp.int32, (C, C), 0).astype(F32)
    ci = lax.broadcasted_iota(jnp.int32, (C, C), 1).astype(F32)
    diff = ri - ci
    decay = jnp.where(diff >= 0, jnp.exp(jnp.maximum(diff, 0.0) * lgf),
                      jnp.exp(jnp.maximum(-diff, 0.0) * lgb))
    zeta_f = jnp.exp((C - 1 - ri) * lgf)
    xi_f = jnp.exp((ri + 1) * lgf)
    gc_f = jnp.exp(jnp.full((C, C), C, F32) * lgf)
    zeta_b = jnp.exp(ri * lgb)
    xi_b = jnp.exp((C - ri) * lgb)
    gc_b = jnp.exp(jnp.full((C, C), C, F32) * lgb)
    def rows_of(c):
        return pl.ds(pl.multiple_of(c * C, C), C)

    def chunk_kv(c, carry):
        sl = rows_of(c)
        k = _rope(k_ref[sl, :].astype(F32), cos_ref[sl, :], sin_ref[sl, :]) * RET_K_SCALE
        ks_ref[sl, :] = k.astype(BF16)
        kz = jnp.concatenate([k * zeta_f, k * zeta_b], axis=1).astype(BF16)
        kv_ref[c] = lax.dot_general(v_ref[sl, :].astype(BF16), kz, (((0,), (0,)), ((), ())),
                                    preferred_element_type=F32)
        return carry

    lax.fori_loop(0, n_all, chunk_kv, 0, unroll=unroll)

    def fwd_state(c, r):
        st_ref[c, :, 0:C] = r.astype(BF16)
        return gc_f * r + kv_ref[c, :, 0:C]

    def bwd_state(i, r):
        c = jnp.where(i < n_ctx, n_ctx - 1 - i, n_all - 1 - (i - n_ctx))
        st_ref[c, :, C:2 * C] = r.astype(BF16)
        return gc_b * r + kv_ref[c, :, C:2 * C]

    lax.fori_loop(0, n_all, fwd_state, jnp.zeros((C, C), F32))
    lax.fori_loop(0, n_all, bwd_state, jnp.zeros((C, C), F32))

    group = next(f for f in (6, 3, 2, 1) if n_all % f == 0)

    def chunks_out(i, carry):
        cs = [i * group + g for g in range(group)]
        sls = [rows_of(c) for c in cs]
        qs = [_rope(q_ref[sl, :].astype(F32), cos_ref[sl, :], sin_ref[sl, :]) for sl in sls]
        ss = [_dot_nt(q.astype(BF16), ks_ref[sl, :]) for q, sl in zip(qs, sls)]
        qxs = [jnp.concatenate([q * xi_f, q * xi_b], axis=1).astype(BF16) for q in qs]
        os = [jnp.dot((s * decay).astype(BF16), v_ref[sl, :].astype(BF16), preferred_element_type=F32)
              + _dot_nt(qx, st_ref[c]) for s, sl, qx, c in zip(ss, sls, qxs, cs)]
        for o, sl in zip(os, sls):
            mu = jnp.mean(o, axis=-1, keepdims=True)
            oc = o - mu
            var = jnp.mean(oc * oc, axis=-1, keepdims=True)
            y = oc * lax.rsqrt(var + LN_EPS) * _silu(g_ref[sl, :].astype(F32))
            y_ref[sl, :] = y.astype(y_ref.dtype)
        return carry

    lax.fori_loop(0, n_all // group, chunks_out, 0)


def _ret_call(u, log_g, l, cos, sin, ctx, w_out):
    B, S, _ = u.shape
    n_all = S // RET_CHUNK
    D = w_out.shape[1]
    tc = D // (B * RET_HEADS)
    assert tc % LANES == 0

    def col(base):
        return lambda b, h: (b, 0, base // RET_DH + h)

    return pl.pallas_call(
        functools.partial(_ret_kernel, n_ctx=ctx // RET_CHUNK, layer=l),
        out_shape=(jax.ShapeDtypeStruct((B, S, RET_W), BF16), jax.ShapeDtypeStruct((D, D), BF16)),
        grid=(B, RET_HEADS),
        in_specs=[
            pl.BlockSpec(memory_space=pltpu.SMEM),
            pl.BlockSpec((None, S, RET_DH), col(U_RET_Q)),
            pl.BlockSpec((None, S, RET_DH), col(U_RET_K)),
            pl.BlockSpec((None, S, RET_DH), col(U_RET_V)),
            pl.BlockSpec((None, S, RET_DH), col(U_RET_G)),
            pl.BlockSpec((S, RET_DH), lambda b, h: (0, 0)),
            pl.BlockSpec((S, RET_DH), lambda b, h: (0, 0)),
            pl.BlockSpec((None, D, tc), lambda b, h: (l, 0, b * RET_HEADS + h)),
        ],
        out_specs=(pl.BlockSpec((None, S, RET_DH), lambda b, h: (b, 0, h)),
                   pl.BlockSpec((D, tc), lambda b, h: (0, b * RET_HEADS + h))),
        scratch_shapes=[pltpu.VMEM((S, RET_DH), BF16),
                        pltpu.VMEM((n_all, RET_DH, 2 * RET_DH), F32),
                        pltpu.VMEM((n_all, RET_DH, 2 * RET_DH), BF16)],
        compiler_params=_cparams(("parallel", "parallel")),
        name="retention",
    )(log_g, u, u, u, u, cos, sin, w_out)


def _outproj_kernel(ym_ref, yl_ref, yr_ref, h_ref, gt_ref, w_ref, lng_ref, lnb_ref, o_ref,
                    *, tm, ctx, alpha, skip, ctx_row):
    b = pl.program_id(0)
    j = pl.program_id(1) + skip

    def project(r0):
        rs = slice(r0, r0 + SUB_ROWS)
        y = jnp.concatenate([ym_ref[rs, :], yl_ref[rs, :], yr_ref[rs, :]], axis=1)
        return jnp.dot(y, w_ref[...], preferred_element_type=F32)

    starts = list(range(0, tm, SUB_ROWS))
    acc_next = project(starts[0])
    for i, r0 in enumerate(starts):
        acc = acc_next
        if i + 1 < len(starts):
            acc_next = project(starts[i + 1])
        rs = slice(r0, r0 + SUB_ROWS)
        rows = j * tm + r0 + lax.broadcasted_iota(jnp.int32, (SUB_ROWS, 1), 0)
        gt = jnp.where(rows < ctx, gt_ref[ctx_row:ctx_row + 1, :], gt_ref[pl.ds(b, 1), :])
        z = alpha * h_ref[rs, :] + gt * acc
        mu = jnp.mean(z, axis=-1, keepdims=True)
        zc = z - mu
        var = jnp.mean(zc * zc, axis=-1, keepdims=True)
        o_ref[rs, :] = zc * lax.rsqrt(var + LN_EPS) * lng_ref[...] + lnb_ref[...]


def _outproj_call(ym, yl, yr, h, mod, w, lng, lnb, l, ctx, tm, alpha, latent_only):
    B, S, D = h.shape
    skip = ctx // tm if latent_only else 0
    assert skip * tm == (ctx if latent_only else 0)
    n_out = S // tm - skip
    return pl.pallas_call(
        functools.partial(_outproj_kernel, tm=tm, ctx=ctx, alpha=alpha, skip=skip, ctx_row=B),
        out_shape=jax.ShapeDtypeStruct((B, n_out * tm, D), F32),
        grid=(B, n_out),
        in_specs=[
            pl.BlockSpec((None, tm, MLA_W), lambda b, j: (b, j + skip, 0)),
            pl.BlockSpec((None, tm, LRU_W), lambda b, j: (b, j + skip, 0)),
            pl.BlockSpec((None, tm, RET_W), lambda b, j: (b, j + skip, 0)),
            pl.BlockSpec((None, tm, D), lambda b, j: (b, j + skip, 0)),
            pl.BlockSpec((8, D), lambda b, j: (0, 2)),
            pl.BlockSpec((D, D), lambda b, j: (0, 0), pipeline_mode=pl.Buffered(1)),
            pl.BlockSpec((None, 1, D), lambda b, j: (l, 0, 0)),
            pl.BlockSpec((None, 1, D), lambda b, j: (l, 0, 0)),
        ],
        out_specs=pl.BlockSpec((None, tm, D), lambda b, j: (b, j, 0)),
        compiler_params=_cparams(("parallel", "parallel")),
        name="out_proj_ln",
    )(ym, yl, yr, h, mod, w, lng, lnb)


def _rope_tables(rows, dim, ctx):
    row = jnp.repeat(jnp.arange(rows, dtype=F32), GRID_W)
    col = jnp.tile(jnp.arange(GRID_W, dtype=F32), rows)
    quarter = dim // 4
    inv = ROPE_BASE ** (-jnp.arange(quarter, dtype=F32) / quarter)
    ang = jnp.stack([row[:, None] * inv, col[:, None] * inv], axis=1)
    cos, sin = jnp.cos(ang), jnp.sin(ang)
    pad = ((0, 0), (0, LANES // 2 - 2 * quarter))
    cos_h = jnp.pad(jnp.concatenate([cos[:, 0], cos[:, 1]], axis=-1), pad)
    sin_h = jnp.pad(jnp.concatenate([sin[:, 0], sin[:, 1]], axis=-1), pad)
    cos_l = jnp.concatenate([cos_h, cos_h], axis=-1)
    sin_l = jnp.concatenate([-sin_h, sin_h], axis=-1)
    cos_c = jnp.ones((ctx, LANES), F32)
    sin_c = jnp.zeros((ctx, LANES), F32)
    return jnp.concatenate([cos_c, cos_l], axis=0), jnp.concatenate([sin_c, sin_l], axis=0)


def _rotary_lane_order(dim):
    quarter = dim // 4
    return ((0, 0), (quarter, LANES // 2), (2 * quarter, quarter), (3 * quarter, LANES // 2 + quarter))


def _rotary_moves(src, dst, dim, groups):
    return tuple((src + g * dim + s, dst + g * LANES + d, dim // 4)
                 for g in range(groups) for s, d in _rotary_lane_order(dim))


W_IN_MOVES = ((0, U_QLAT, 512), (512, U_KVLAT, 256), (832, U_MLA_G, 1024),
              (1856, U_LRU_X, 512), (2368, U_LRU_G, 512), (3904, U_RET_V, 512), (4416, U_RET_G, 512)
              ) + _rotary_moves(768, U_KR, MLA_ROPE, 1) + _rotary_moves(2880, U_RET_Q, RET_DH, RET_HEADS
              ) + _rotary_moves(3392, U_RET_K, RET_DH, RET_HEADS)
W_IN_ZERO = tuple((U_KR + o, LANES // 2 - MLA_ROPE // 2) for o in (MLA_ROPE // 2, LANES // 2 + MLA_ROPE // 2)
                  ) + ((U_USED, U_W - U_USED),)


def _w_in_layout_kernel(w_ref, o_ref):
    for src, dst, width in W_IN_MOVES:
        o_ref[dst:dst + width, :] = w_ref[src:src + width, :].astype(BF16)
    for dst, width in W_IN_ZERO:
        o_ref[dst:dst + width, :] = jnp.zeros((width, o_ref.shape[1]), BF16)


def _layout_w_in(w_in_t, layer):
    _, W, D = w_in_t.shape
    tc = 512
    return pl.pallas_call(
        _w_in_layout_kernel,
        out_shape=jax.ShapeDtypeStruct((U_W, D), BF16),
        grid=(D // tc,),
        in_specs=[pl.BlockSpec((None, W, tc), lambda r: (layer, 0, r))],
        out_specs=pl.BlockSpec((U_W, tc), lambda r: (0, r)),
        compiler_params=_cparams(("parallel",)),
        name="w_in_layout",
    )(w_in_t)


def _layout_w_uq(w_uq):
    L = w_uq.shape[0]
    w = w_uq.reshape(L, MLA_Q_RANK, MLA_HEADS, MLA_NOPE + MLA_ROPE)
    quarter = MLA_ROPE // 4
    pieces, lane = [w[..., :MLA_NOPE]], 0
    for s, d in sorted(_rotary_lane_order(MLA_ROPE), key=lambda sd: sd[1]):
        pieces += [jnp.zeros(w.shape[:-1] + (d - lane,), w.dtype),
                   w[..., MLA_NOPE + s:MLA_NOPE + s + quarter]]
        lane = d + quarter
    pieces.append(jnp.zeros(w.shape[:-1] + (LANES - lane,), w.dtype))
    w = jnp.concatenate(pieces, axis=-1)
    return w.reshape(L, MLA_Q_RANK, MLA_HEADS * MLA_HP).astype(BF16)


def _layout_w_ukv(w_ukv):
    L = w_ukv.shape[0]
    w = w_ukv.reshape(L, MLA_KV_RANK, MLA_HEADS, MLA_NOPE + MLA_V)
    wk = w[..., :MLA_NOPE].reshape(L, MLA_KV_RANK, MLA_W).astype(BF16)
    wvt = w[..., MLA_NOPE:].reshape(L, MLA_KV_RANK, MLA_W).transpose(0, 2, 1).astype(BF16)
    return wk, wvt


def _layout_lru_gates(w_r, b_r, w_i, b_i):
    L = w_r.shape[0]
    ncb = LRU_W // LRU_CB
    per = LRU_BLOCKS // ncb
    eye = jnp.eye(per, dtype=w_r.dtype)

    def diag_blocks(w):
        w = w.reshape(L, 2, ncb, per, LRU_BW, LRU_BW)
        return jnp.einsum("ldcgij,gh->ldcgihj", w, eye).reshape(L, 2, ncb, LRU_CB, LRU_CB)

    wg = jnp.concatenate([diag_blocks(w_r), diag_blocks(w_i)], axis=-1).astype(BF16)
    bg = jnp.concatenate([b_r.reshape(L, 2, ncb, 1, LRU_CB), b_i.reshape(L, 2, ncb, 1, LRU_CB)], axis=-1)
    return 0.5 * wg, 0.5 * bg


def kernel(x, c, ctx, c_ctx, w_ada, b_ada, w_in, mla_q_norm_g, mla_kv_norm_g, mla_w_uq, mla_w_ukv,
           lru_conv_w, lru_conv_b, lru_w_r, lru_b_r, lru_w_i, lru_b_i, lru_lambda, ret_decay,
           w_out, ln_g, ln_b):
    B, T, D = x.shape
    L = w_in.shape[0]
    n_ctx = ctx.shape[1]
    S = n_ctx + T
    assert D == 2 * MLA_W and w_in.shape[2] == 4928 and n_ctx % 256 == 0 and T % 256 == 0
    assert B + 1 <= 8
    alpha = (2 * L) ** 0.25
    tm = 768 if S % 768 == 0 else 256

    cos_m, sin_m = _rope_tables(T // GRID_W, MLA_ROPE, n_ctx)
    cos_r, sin_r = _rope_tables(T // GRID_W, RET_DH, n_ctx)

    cond = jnp.zeros((8, D), F32).at[:B].set(c).at[B].set(c_ctx)
    b_ada3 = b_ada[:, None, :]
    mod = _ada_call(cond, w_ada, b_ada3, 0)

    h = _ln0_call(ctx, x)
    log_g = jax.nn.log_sigmoid(ret_decay.astype(F32)).reshape(2 * L, RET_HEADS)

    w_in_t = jnp.swapaxes(w_in, 1, 2)
    w_t = _layout_w_in(w_in_t, 0)
    wuq = _layout_w_uq(mla_w_uq)
    wk, wvt = _layout_w_ukv(mla_w_ukv)
    wg, bg = _layout_lru_gates(lru_w_r, lru_b_r, lru_w_i, lru_b_i)
    gq, gkv = mla_q_norm_g[:, None, :], mla_kv_norm_g[:, None, :]
    conv_b, lng, lnb = lru_conv_b[:, None, :], ln_g[:, None, :], ln_b[:, None, :]

    for l in range(L):
        last = l == L - 1
        u = _inproj_call(h, mod, w_t, n_ctx, tm)
        q, k, vt = _mla_prep_call(u, gq, gkv, wuq, wk, wvt, l, cos_m, sin_m)
        y_mla, w_t = _attn_call(q, k, vt, u, n_ctx, *(() if last else (w_in_t, l + 1)))
        y_lru, mod_next = _lru_call(u, lru_conv_w, conv_b, wg, bg, lru_lambda, l, n_ctx,
                                    None if last else (cond, w_ada, b_ada3))
        y_ret, w_out_b = _ret_call(u, log_g, l, cos_r, sin_r, n_ctx, w_out)
        h = _outproj_call(y_mla, y_lru, y_ret, h, mod, w_out_b, lng, lnb, l, n_ctx,
                          256 if last else tm, alpha, last)
        mod = mod_next
    return h
```
